```python
import jax, jax.numpy as jnp
from jax import lax
import numpy as np

D_MODEL = 1024
BATCH = 16
SEQ = 2048
DEPTH = 2
DEC_BATCH = 32
DEC_SEQ = 32
PAST_LEN = 1024

CHUNK = 64
HEAD_DIM = 64
H_A = 8
KV_A = 2
WINDOW_A = 128
N_PREV_A = WINDOW_A // CHUNK
H_B = 8
N_PREV_B = 8
BAND_B = N_PREV_B * CHUNK
MAX_REL = 128
ROPE_THETA = 10000.0
D_LRU = D_MODEL
N_BLOCKS = 16
BLOCK = D_LRU // N_BLOCKS
CONV_W = 4
C_GATE = 8.0
N_AB = (DEPTH + 1) // 2
N_LRU = DEPTH // 2
EPS = 1e-6
PROJ_SIZES = (H_A * HEAD_DIM, KV_A * HEAD_DIM, KV_A * HEAD_DIM, H_A * HEAD_DIM,
              H_B * HEAD_DIM, H_B * HEAD_DIM, H_B * HEAD_DIM, H_B * HEAD_DIM)
IN_AB = sum(PROJ_SIZES)
MIX_AB = (H_A + H_B) * HEAD_DIM

kernel_name = 'hybrid_stream_swa_band_rglru_step'


def _split_cols(y, sizes):
    offs, acc = [], 0
    for s in sizes[:-1]:
        acc += s
        offs.append(acc)
    return jnp.split(y, offs, axis=-1)


def rms_norm(x, g):
    xf = x.astype(jnp.float32)
    y = xf * lax.rsqrt(jnp.mean(xf * xf, axis=-1, keepdims=True) + EPS)
    return (y * g.astype(jnp.float32)).astype(x.dtype)


def rope(x, pos):
    half = x.shape[-1] // 2
    inv = ROPE_THETA ** (-jnp.arange(half, dtype=jnp.float32) / half)
    ang = pos.astype(jnp.float32)[:, None] * inv[None, :]
    cos = jnp.cos(ang)[None, :, None, :]
    sin = jnp.sin(ang)[None, :, None, :]
    x1 = x[..., :half].astype(jnp.float32)
    x2 = x[..., half:].astype(jnp.float32)
    return jnp.concatenate([x1 * cos - x2 * sin, x2 * cos + x1 * sin], axis=-1).astype(x.dtype)


def attend(q, k, v, q_pos, k_pos, valid, bias_table, sinks):
    _, tq, kvh, g, dh = q.shape
    tk = k.shape[1]
    s = jnp.einsum('bqhgd,bshd->bhgqs', q, k).astype(jnp.float32) * (dh ** -0.5)
    if bias_table is not None:
        rel = jnp.clip(q_pos[:, None] - k_pos[None, :], -MAX_REL, MAX_REL) + MAX_REL
        bias = bias_table.astype(jnp.float32)[:, rel]
        s = s + bias.reshape(kvh, g, tq, tk)
    if valid is not None:
        s = jnp.where(valid, s, -1e30)
    if sinks is not None:
        sink = sinks.astype(jnp.float32).reshape(kvh, g)[None, :, :, None, None]
        m = jnp.maximum(jnp.max(s, axis=-1, keepdims=True), sink)
        p = jnp.exp(s - m)
        p = p / (jnp.sum(p, axis=-1, keepdims=True) + jnp.exp(sink - m))
    else:
        p = jax.nn.softmax(s, axis=-1)
    return jnp.einsum('bhgqs,bshd->bqhgd', p.astype(v.dtype), v)


def prompt_band(q, k, v, n_prev, bias_table, sinks):
    b, s_len, kvh, g, dh = q.shape
    nc = s_len // CHUNK
    pad = n_prev * CHUNK
    span = pad + CHUNK
    kp = jnp.pad(k, ((0, 0), (pad, 0), (0, 0), (0, 0)))
    vp = jnp.pad(v, ((0, 0), (pad, 0), (0, 0), (0, 0)))
    qc = jnp.moveaxis(q.reshape(b, nc, CHUNK, kvh, g, dh), 1, 0)

    def one_chunk(args):
        c, qb = args
        start = c * CHUNK
        kb = lax.dynamic_slice_in_dim(kp, start, span, axis=1)
        vb = lax.dynamic_slice_in_dim(vp, start, span, axis=1)
        q_pos = start + jnp.arange(CHUNK, dtype=jnp.int32)
        k_pos = start - pad + jnp.arange(span, dtype=jnp.int32)
        return attend(qb, kb, vb, q_pos, k_pos, k_pos >= 0, bias_table, sinks)

    out = lax.map(one_chunk, (jnp.arange(nc, dtype=jnp.int32), qc))
    return jnp.moveaxis(out, 0, 1).reshape(b, s_len, kvh, g, dh)


def ab_project(h, pos, w_in):
    b, t, _ = h.shape
    qa, ka, va, ga, qb, kb, vb, gb = _split_cols(h @ w_in, PROJ_SIZES)
    qa = rope(qa.reshape(b, t, H_A, HEAD_DIM), pos).reshape(b, t, KV_A, H_A // KV_A, HEAD_DIM)
    ka = rope(ka.reshape(b, t, KV_A, HEAD_DIM), pos)
    va = va.reshape(b, t, KV_A, HEAD_DIM)
    qb = qb.reshape(b, t, H_B, 1, HEAD_DIM)
    kb = kb.reshape(b, t, H_B, HEAD_DIM)
    vb = vb.reshape(b, t, H_B, HEAD_DIM)
    return qa, ka, va, ga, qb, kb, vb, gb


def ab_output(oa, ob, ga, gb, w_out):
    b, t = oa.shape[0], oa.shape[1]
    y = jnp.concatenate([oa.reshape(b, t, -1) * jax.nn.silu(ga),
                         ob.reshape(b, t, -1) * jax.nn.silu(gb)], axis=-1)
    return y @ w_out


def last_rows(x, n):
    extra = max(0, n - x.shape[1])
    if extra > 0:
        x = jnp.pad(x, ((0, 0), (extra, 0)) + ((0, 0),) * (x.ndim - 2))
    return x[:, x.shape[1] - n:]


def _lin_combine(c1, c2):
    a1, b1 = c1
    a2, b2 = c2
    return a1 * a2, a2 * b1 + b2


def lru_mixer(h, conv_state, h0, w_in, conv_w, conv_b, wa, ba, wx, bx, lam, w_out):
    b, t, _ = h.shape
    xb, z = jnp.split(h @ w_in, 2, axis=-1)
    xp = jnp.concatenate([conv_state.astype(xb.dtype), xb], axis=1)
    xc = conv_b.astype(xb.dtype) + xp[:, 0:t] * conv_w[0]
    for tap in range(1, CONV_W):
        xc = xc + xp[:, tap:tap + t] * conv_w[tap]
    xr = xc.reshape(b, t, N_BLOCKS, BLOCK)
    r = jax.nn.sigmoid(jnp.einsum('btni,nij->btnj', xr, wa).reshape(b, t, D_LRU).astype(jnp.float32)
                       + ba.astype(jnp.float32))
    gi = jax.nn.sigmoid(jnp.einsum('btni,nij->btnj', xr, wx).reshape(b, t, D_LRU).astype(jnp.float32)
                        + bx.astype(jnp.float32))
    log_a = C_GATE * r * jax.nn.log_sigmoid(lam.astype(jnp.float32))
    a = jnp.exp(log_a)
    u = jnp.sqrt(-jnp.expm1(2.0 * log_a)) * (gi * xc.astype(jnp.float32))
    u = u.at[:, 0].add(a[:, 0] * h0.astype(jnp.float32))
    _, hs = lax.associative_scan(_lin_combine, (a, u), axis=1)
    y = hs.astype(h.dtype) * jax.nn.silu(z)
    return y @ w_out, hs[:, -1], xp[:, xp.shape[1] - (CONV_W - 1):]


def setup_inputs(seed: int = 0) -> dict:
    key = jax.random.key(seed)
    ks = jax.random.split(key, 24)
    wa_rows = min(WINDOW_A, PAST_LEN)
    wb_rows = min(BAND_B, PAST_LEN)
    f32 = jnp.float32
    nrm = lambda k, shape, sc: jax.random.normal(k, shape, f32) * sc
    ac = jax.random.uniform(ks[22], (N_LRU, D_LRU), f32, minval=0.9, maxval=0.999)
    a_base = ac ** (1.0 / C_GATE)
    return {
        'x_prompt': nrm(ks[0], (BATCH, SEQ, D_MODEL), 1.0),
        'x_sample': nrm(ks[1], (DEC_BATCH, DEC_SEQ, D_MODEL), 1.0),
        'cache_a_k': nrm(ks[2], (N_AB, DEC_BATCH, wa_rows, KV_A, HEAD_DIM), 1.0),
        'cache_a_v': nrm(ks[3], (N_AB, DEC_BATCH, wa_rows, KV_A, HEAD_DIM), 1.0),
        'cache_b_k': nrm(ks[4], (N_AB, DEC_BATCH, wb_rows, H_B, HEAD_DIM), 1.0),
        'cache_b_v': nrm(ks[5], (N_AB, DEC_BATCH, wb_rows, H_B, HEAD_DIM), 1.0),
        'state_c_h': nrm(ks[6], (N_LRU, DEC_BATCH, D_LRU), 0.5),
        'state_c_conv': nrm(ks[7], (N_LRU, DEC_BATCH, CONV_W - 1, D_LRU), 1.0),
        'ln_pre': 1.0 + nrm(ks[8], (DEPTH, D_MODEL), 0.05),
        'ln_post': 1.0 + nrm(ks[9], (DEPTH, D_MODEL), 0.05),
        'w_in_ab': nrm(ks[10], (N_AB, D_MODEL, IN_AB), D_MODEL ** -0.5),
        'sinks_a': nrm(ks[11], (N_AB, H_A), 0.5),
        'relpos_b': nrm(ks[12], (N_AB, H_B, 2 * MAX_REL + 1), 0.1),
        'w_out_ab': nrm(ks[13], (N_AB, MIX_AB, D_MODEL), MIX_AB ** -0.5),
        'w_in_c': nrm(ks[14], (N_LRU, D_MODEL, 2 * D_LRU), D_MODEL ** -0.5),
        'conv_c_w': nrm(ks[15], (N_LRU, CONV_W, D_LRU), CONV_W ** -0.5),
        'conv_c_b': nrm(ks[16], (N_LRU, D_LRU), 0.01),
        'gate_c_wa': nrm(ks[17], (N_LRU, N_BLOCKS, BLOCK, BLOCK), BLOCK ** -0.5),
        'gate_c_ba': nrm(ks[18], (N_LRU, D_LRU), 0.01),
        'gate_c_wx': nrm(ks[19], (N_LRU, N_BLOCKS, BLOCK, BLOCK), BLOCK ** -0.5),
        'gate_c_bx': nrm(ks[20], (N_LRU, D_LRU), 0.01),
        'lambda_c': jnp.log(a_base) - jnp.log1p(-a_base),
        'w_out_c': nrm(ks[21], (N_LRU, D_LRU, D_MODEL), D_LRU ** -0.5),
    }


def reference(x_prompt, x_sample, cache_a_k, cache_a_v, cache_b_k, cache_b_v, state_c_h, state_c_conv,
              ln_pre, ln_post, w_in_ab, sinks_a, relpos_b, w_out_ab, w_in_c, conv_c_w, conv_c_b,
              gate_c_wa, gate_c_ba, gate_c_wx, gate_c_bx, lambda_c, w_out_c):
    bp, s_len = x_prompt.shape[0], x_prompt.shape[1]
    t = x_sample.shape[1]
    wa_rows = cache_a_k.shape[2]
    wb_rows = cache_b_k.shape[2]
    pos_p = jnp.arange(s_len, dtype=jnp.int32)
    pos_s = PAST_LEN + jnp.arange(t, dtype=jnp.int32)
    kpos_a = PAST_LEN - wa_rows + jnp.arange(wa_rows + t, dtype=jnp.int32)
    kpos_b = PAST_LEN - wb_rows + jnp.arange(wb_rows + t, dtype=jnp.int32)
    nak_p, nav_p, nbk_p, nbv_p, nch_p, ncc_p = [], [], [], [], [], []
    nak_s, nav_s, nbk_s, nbv_s, nch_s, ncc_s = [], [], [], [], [], []
    yp, ys = x_prompt, x_sample
    for l in range(DEPTH):
        j = l // 2
        hp = rms_norm(yp, ln_pre[l])
        hs = rms_norm(ys, ln_pre[l])
        if l % 2 == 0:
            qa, ka, va, ga, qb, kb, vb, gb = ab_project(hp, pos_p, w_in_ab[j])
            oa = prompt_band(qa, ka, va, N_PREV_A, None, sinks_a[j])
            ob = prompt_band(qb, kb, vb, N_PREV_B, relpos_b[j], None)
            mp = ab_output(oa, ob, ga, gb, w_out_ab[j])
            nak_p.append(last_rows(ka, wa_rows))
            nav_p.append(last_rows(va, wa_rows))
            nbk_p.append(last_rows(kb, wb_rows))
            nbv_p.append(last_rows(vb, wb_rows))
            qa, ka, va, ga, qb, kb, vb, gb = ab_project(hs, pos_s, w_in_ab[j])
            ka = jnp.concatenate([cache_a_k[j].astype(ka.dtype), ka], axis=1)
            va = jnp.concatenate([cache_a_v[j].astype(va.dtype), va], axis=1)
            kb = jnp.concatenate([cache_b_k[j].astype(kb.dtype), kb], axis=1)
            vb = jnp.concatenate([cache_b_v[j].astype(vb.dtype), vb], axis=1)
            oa = attend(qa, ka, va, pos_s, kpos_a, None, None, sinks_a[j])
            ob = attend(qb, kb, vb, pos_s, kpos_b, None, relpos_b[j], None)
            ms = ab_output(oa, ob, ga, gb, w_out_ab[j])
            nak_s.append(ka[:, t:])
            nav_s.append(va[:, t:])
            nbk_s.append(kb[:, t:])
            nbv_s.append(vb[:, t:])
        else:
            mp, h_last, conv_last = lru_mixer(
                hp, jnp.zeros((bp, CONV_W - 1, D_LRU), hp.dtype), jnp.zeros((bp, D_LRU), jnp.float32),
                w_in_c[j], conv_c_w[j], conv_c_b[j], gate_c_wa[j], gate_c_ba[j], gate_c_wx[j], gate_c_bx[j],
                lambda_c[j], w_out_c[j])
            nch_p.append(h_last)
            ncc_p.append(conv_last)
            ms, h_last, conv_last = lru_mixer(
                hs, state_c_conv[j], state_c_h[j],
                w_in_c[j], conv_c_w[j], conv_c_b[j], gate_c_wa[j], gate_c_ba[j], gate_c_wx[j], gate_c_bx[j],
                lambda_c[j], w_out_c[j])
            nch_s.append(h_last)
            ncc_s.append(conv_last)
        yp = yp + rms_norm(mp, ln_post[l])
        ys = ys + rms_norm(ms, ln_post[l])
    return (yp, ys,
            jnp.stack(nak_p), jnp.stack(nav_p), jnp.stack(nbk_p), jnp.stack(nbv_p),
            jnp.stack(nch_p), jnp.stack(ncc_p),
            jnp.stack(nak_s), jnp.stack(nav_s), jnp.stack(nbk_s), jnp.stack(nbv_s),
            jnp.stack(nch_s), jnp.stack(ncc_s))
```

```python
import functools

import jax
import jax.numpy as jnp
from jax import lax
from jax.experimental import pallas as pl
from jax.experimental.pallas import tpu as pltpu

F32 = jnp.float32
BF16 = jnp.bfloat16

D_MODEL = 1024
CHUNK = 64
HEAD_DIM = 64
H_A = 8
KV_A = 2
N_PREV_A = 2
H_B = 8
N_PREV_B = 8
MAX_REL = 128
ROPE_THETA = 10000.0
D_LRU = D_MODEL
N_BLOCKS = 16
BLOCK = D_LRU // N_BLOCKS
CONV_W = 4
C_GATE = 8.0
EPS = 1e-6
PAST_LEN = 1024
NEG = -1e30

LANES = 128
MXU_DIM = 256
VMEM_LIMIT = 56 * 1024 * 1024

QA_W = H_A * HEAD_DIM
KA_W = KV_A * HEAD_DIM
B_W = H_B * HEAD_DIM
MIX_W = QA_W + B_W
GRP = 2 * CHUNK
WIN_A = (N_PREV_A + 2) * CHUNK
WIN_B = (N_PREV_B + 2) * CHUNK
PAD_ROWS = N_PREV_B * CHUNK
TOEP_W = WIN_B + LANES

PROJ_TM = 512
ATT_TQ = 256
LRU_TL = 256


def _params(n_axes):
    return pltpu.CompilerParams(
        dimension_semantics=("arbitrary",) * n_axes,
        vmem_limit_bytes=VMEM_LIMIT)


def _rms(x, g):
    ms = jnp.mean(x * x, axis=-1, keepdims=True)
    return x * lax.rsqrt(ms + EPS) * g


def _silu(x):
    return x * jax.nn.sigmoid(x)


def _proj0_body(x_ref, g_ref, w_ref, cos_ref, sin_ref,
                qa_ref, ka_ref, va_ref, ga_ref, qb_ref, kb_ref, vb_ref, gb_ref):
    h = _rms(x_ref[...], g_ref[...]).astype(BF16)
    cos = cos_ref[...]
    sin = sin_ref[...]
    tm = cos.shape[0]
    lane = lax.broadcasted_iota(jnp.int32, (tm, LANES), 1)
    lower = (lane & (HEAD_DIM - 1)) < (HEAD_DIM // 2)

    def mm(lo, hi):
        return jnp.dot(h, w_ref[:, lo:hi], preferred_element_type=F32)

    def rope(t):
        partner = jnp.where(lower, pltpu.roll(t, LANES - HEAD_DIM // 2, 1),
                            pltpu.roll(t, HEAD_DIM // 2, 1))
        return t * cos + partner * sin

    o = 0
    t = mm(o, o + QA_W)
    for k in range(QA_W // LANES):
        sl = slice(k * LANES, (k + 1) * LANES)
        qa_ref[:, sl] = (rope(t[:, sl]) * (HEAD_DIM ** -0.5)).astype(BF16)
    o += QA_W
    ka_ref[...] = rope(mm(o, o + KA_W)).astype(BF16)
    o += KA_W
    va_ref[...] = mm(o, o + KA_W).astype(BF16)
    o += KA_W
    ga_ref[...] = _silu(mm(o, o + QA_W)).astype(BF16)
    o += QA_W
    qb_ref[...] = (mm(o, o + B_W) * (HEAD_DIM ** -0.5)).astype(BF16)
    o += B_W
    kb_ref[...] = mm(o, o + B_W).astype(BF16)
    o += B_W
    vb_ref[...] = mm(o, o + B_W).astype(BF16)
    o += B_W
    gb_ref[...] = _silu(mm(o, o + B_W)).astype(BF16)


def _proj0(x2d, g, w, cos, sin, tm):
    n = x2d.shape[0]
    n_pos_blocks = cos.shape[0] // tm
    widths = (QA_W, KA_W, KA_W, QA_W, B_W, B_W, B_W, B_W)
    row = lambda i: (i, 0)
    fixed = lambda i: (0, 0)
    pos = lambda i: (i % n_pos_blocks, 0)
    return pl.pallas_call(
        _proj0_body,
        grid=(n // tm,),
        in_specs=[pl.BlockSpec((tm, D_MODEL), row),
                  pl.BlockSpec((1, D_MODEL), fixed),
                  pl.BlockSpec(w.shape, fixed),
                  pl.BlockSpec((tm, LANES), pos),
                  pl.BlockSpec((tm, LANES), pos)],
        out_specs=[pl.BlockSpec((tm, wd), row) for wd in widths],
        out_shape=[jax.ShapeDtypeStruct((n, wd), BF16) for wd in widths],
        compiler_params=_params(1),
        name="proj0",
    )(x2d, g, w, cos, sin)


def _bias_body(r0_ref, bp_ref, bs_ref, band_ref, *, t_sample, keys_sample):
    x = jnp.broadcast_to(r0_ref[0], (GRP, TOEP_W))
    row = lax.broadcasted_iota(jnp.int32, (GRP, TOEP_W), 0)
    for b in range(GRP.bit_length() - 1):
        x = jnp.where(((row >> b) & 1) == 1, pltpu.roll(x, 1 << b, 1), x)
    x = x[:, :WIN_B]
    r = lax.broadcasted_iota(jnp.int32, (GRP, WIN_B), 0)
    c = lax.broadcasted_iota(jnp.int32, (GRP, WIN_B), 1)
    ci = r // CHUNK
    cj = c // CHUNK
    bp_ref[0] = jnp.where((cj >= ci) & (cj <= ci + N_PREV_B), x, NEG)
    cs = lax.broadcasted_iota(jnp.int32, (t_sample, WIN_B), 1)
    bs_ref[0] = jnp.where(cs < keys_sample, x[:t_sample], NEG)
    ra = lax.broadcasted_iota(jnp.int32, (GRP, WIN_A), 0) // CHUNK
    ca = lax.broadcasted_iota(jnp.int32, (GRP, WIN_A), 1) // CHUNK
    band_ref[...] = jnp.where((ca >= ra) & (ca <= ra + N_PREV_A), 0.0, NEG)


def _build_bias(relpos, t_sample, keys_sample):
    far = relpos[:, 2 * MAX_REL:]
    n_far = PAD_ROWS - MAX_REL + 1
    r0 = jnp.concatenate(
        [jnp.broadcast_to(far, (H_B, n_far)),
         relpos[:, 2 * MAX_REL - 1:0:-1],
         jnp.broadcast_to(far, (H_B, TOEP_W - n_far - (2 * MAX_REL - 1)))], axis=1)
    r0 = r0.reshape(H_B, 1, TOEP_W)
    return pl.pallas_call(
        functools.partial(_bias_body, t_sample=t_sample, keys_sample=keys_sample),
        grid=(H_B,),
        in_specs=[pl.BlockSpec((1, 1, TOEP_W), lambda h: (h, 0, 0))],
        out_specs=[pl.BlockSpec((1, GRP, WIN_B), lambda h: (h, 0, 0)),
                   pl.BlockSpec((1, t_sample, WIN_B), lambda h: (h, 0, 0)),
                   pl.BlockSpec((GRP, WIN_A), lambda h: (0, 0))],
        out_shape=[jax.ShapeDtypeStruct((H_B, GRP, WIN_B), F32),
                   jax.ShapeDtypeStruct((H_B, t_sample, WIN_B), F32),
                   jax.ShapeDtypeStruct((GRP, WIN_A), F32)],
        compiler_params=_params(1),
        name="relbias",
    )(r0)


def _nt_dot(a, b):
    return lax.dot_general(a, b, (((1,), (1,)), ((), ())), preferred_element_type=F32)


def _attend_group(qa, qb, ka, va, kb_ref, vb_ref, win_b, sink_ref, bias_ref, mask_a, col_ok_a,
                  col_ok_b):
    n = qa.shape[0]
    lane = lax.broadcasted_iota(jnp.int32, (n, LANES), 1)
    lo = lane < HEAD_DIM
    zero = jnp.zeros((n, LANES), BF16)

    rows = []
    for p in range(QA_W // LANES):
        tile = qa[:, p * LANES:(p + 1) * LANES]
        rows.append(jnp.where(lo, tile, zero))
        rows.append(jnp.where(lo, zero, tile))
    s_all = _nt_dot(jnp.concatenate(rows, axis=0), ka)
    probs, inv = [], []
    for h in range(H_A):
        s = s_all[h * n:(h + 1) * n]
        if mask_a is not None:
            s = s + mask_a
        if col_ok_a is not None:
            s = jnp.where(col_ok_a, s, NEG)
        sink = sink_ref[h]
        m = jnp.maximum(jnp.max(s, axis=-1, keepdims=True), sink)
        e = jnp.exp(s - m)
        den = jnp.sum(e, axis=-1, keepdims=True) + jnp.exp(sink - m)
        probs.append(e.astype(BF16))
        inv.append(1.0 / den)
    r_all = jnp.dot(jnp.concatenate(probs, axis=0), va, preferred_element_type=F32)
    o_a = []
    for p in range(QA_W // LANES):
        r0 = r_all[(2 * p) * n:(2 * p + 1) * n] * inv[2 * p]
        r1 = r_all[(2 * p + 1) * n:(2 * p + 2) * n] * inv[2 * p + 1]
        o_a.append(jnp.where(lo, r0, r1))
    o_a = jnp.concatenate(o_a, axis=1)

    o_b = []
    for p in range(B_W // LANES):
        sl = slice(p * LANES, (p + 1) * LANES)
        tile = qb[:, sl]
        lhs = jnp.concatenate([jnp.where(lo, tile, zero), jnp.where(lo, zero, tile)], axis=0)
        s2 = _nt_dot(lhs, kb_ref[win_b, sl])
        halves = []
        for j in range(2):
            s = s2[j * n:(j + 1) * n] + bias_ref[2 * p + j]
            if col_ok_b is not None:
                s = jnp.where(col_ok_b, s, NEG)
            m = jnp.max(s, axis=-1, keepdims=True)
            e = jnp.exp(s - m)
            den = jnp.sum(e, axis=-1, keepdims=True)
            r = jnp.dot(e.astype(BF16), vb_ref[win_b, sl], preferred_element_type=F32)
            halves.append(r * (1.0 / den))
        o_b.append(jnp.where(lo, halves[0], halves[1]))
    o_b = jnp.concatenate(o_b, axis=1)
    return o_a, o_b


def _mix_out(o_a, o_b, ga, gb, x, w_ref, g_ref):
    mix = jnp.concatenate([o_a * ga.astype(F32), o_b * gb.astype(F32)], axis=1).astype(BF16)
    m = jnp.dot(mix, w_ref[...], preferred_element_type=F32)
    return x + _rms(m, g_ref[...])


def _attn_prompt_body(sink_ref, x_ref, qa_ref, ga_ref, qb_ref, gb_ref, ka_ref, va_ref, kb_ref,
                      vb_ref, bias_ref, band_ref, w_ref, g_ref, y_ref,
                      kap, vap, kbp, vbp, *, seq):
    t = pl.program_id(1)
    tq = x_ref.shape[1]

    @pl.when(t == 0)
    def _():
        for dst, src in ((kap, ka_ref), (vap, va_ref), (kbp, kb_ref), (vbp, vb_ref)):
            dst[:PAD_ROWS, :] = jnp.zeros((PAD_ROWS, dst.shape[1]), BF16)
            dst[PAD_ROWS:, :] = src[0]

    band = band_ref[...]
    col_a = lax.broadcasted_iota(jnp.int32, (1, WIN_A), 1)
    col_b = lax.broadcasted_iota(jnp.int32, (1, WIN_B), 1)
    for sub in range(tq // GRP):
        start = pl.multiple_of(t * tq + sub * GRP, GRP)
        rs = slice(sub * GRP, (sub + 1) * GRP)
        win_a = pl.ds(pl.multiple_of(start + (PAD_ROWS - N_PREV_A * CHUNK), GRP), WIN_A)
        win_b = pl.ds(start, WIN_B)
        ok_a = col_a >= N_PREV_A * CHUNK - start
        ok_b = col_b >= N_PREV_B * CHUNK - start
        o_a, o_b = _attend_group(qa_ref[0, rs, :], qb_ref[0, rs, :], kap[win_a, :], vap[win_a, :],
                                 kbp, vbp, win_b, sink_ref, bias_ref, band, ok_a, ok_b)
        y_ref[0, rs, :] = _mix_out(o_a, o_b, ga_ref[0, rs, :], gb_ref[0, rs, :], x_ref[0, rs, :],
                                   w_ref, g_ref)


def _attn_prompt(sinks, x, qa, ga, qb, gb, ka, va, kb, vb, bias, band, w_out, g_post, tq):
    b, seq, _ = x.shape
    tile = lambda w: pl.BlockSpec((1, tq, w), lambda i, j: (i, j, 0))
    whole = lambda w: pl.BlockSpec((1, seq, w), lambda i, j: (i, 0, 0))
    fixed = lambda shape: pl.BlockSpec(shape, lambda i, j: (0,) * len(shape))
    return pl.pallas_call(
        functools.partial(_attn_prompt_body, seq=seq),
        grid=(b, seq // tq),
        in_specs=[pl.BlockSpec(memory_space=pltpu.SMEM),
                  tile(D_MODEL), tile(QA_W), tile(QA_W), tile(B_W), tile(B_W),
                  whole(KA_W), whole(KA_W), whole(B_W), whole(B_W),
                  fixed(bias.shape), fixed(band.shape), fixed(w_out.shape), fixed((1, D_MODEL))],
        out_specs=tile(D_MODEL),
        out_shape=jax.ShapeDtypeStruct(x.shape, F32),
        scratch_shapes=[pltpu.VMEM((PAD_ROWS + seq, KA_W), BF16),
                        pltpu.VMEM((PAD_ROWS + seq, KA_W), BF16),
                        pltpu.VMEM((PAD_ROWS + seq, B_W), BF16),
                        pltpu.VMEM((PAD_ROWS + seq, B_W), BF16)],
        compiler_params=_params(2),
        name="attn_prompt",
    )(sinks, x, qa, ga, qb, gb, ka, va, kb, vb, bias, band, w_out, g_post)


def _attn_sample_body(sink_ref, x_ref, qa_ref, ga_ref, qb_ref, gb_ref, ka_ref, va_ref, kb_ref,
                      vb_ref, cak_ref, cav_ref, cbk_ref, cbv_ref, bias_ref, w_ref, g_ref, y_ref,
                      kas, vas, kbs, vbs):
    t = x_ref.shape[1]
    wa = cak_ref.shape[1]
    wb = cbk_ref.shape[1]

    @pl.when(pl.program_id(0) == 0)
    def _():
        for dst in (kas, vas, kbs, vbs):
            dst[...] = jnp.zeros(dst.shape, BF16)

    for dst, cache, new, w in ((kas, cak_ref, ka_ref, wa), (vas, cav_ref, va_ref, wa),
                               (kbs, cbk_ref, kb_ref, wb), (vbs, cbv_ref, vb_ref, wb)):
        dst[:w, :] = cache[0].astype(BF16)
        dst[w:w + t, :] = new[0]

    ok_a = lax.broadcasted_iota(jnp.int32, (1, WIN_A), 1) < wa + t
    o_a, o_b = _attend_group(qa_ref[0], qb_ref[0], kas[...], vas[...], kbs, vbs,
                             pl.ds(0, WIN_B), sink_ref, bias_ref, None, ok_a, None)
    y_ref[0] = _mix_out(o_a, o_b, ga_ref[0], gb_ref[0], x_ref[0], w_ref, g_ref)


def _attn_sample(sinks, x, qa, ga, qb, gb, ka, va, kb, vb, cak, cav, cbk, cbv, bias, w_out, g_post):
    b, t, _ = x.shape
    per = lambda a: pl.BlockSpec((1,) + a.shape[1:], lambda i: (i, 0, 0))
    fixed = lambda shape: pl.BlockSpec(shape, lambda i: (0,) * len(shape))
    arrs = (x, qa, ga, qb, gb, ka, va, kb, vb, cak, cav, cbk, cbv)
    return pl.pallas_call(
        _attn_sample_body,
        grid=(b,),
        in_specs=[pl.BlockSpec(memory_space=pltpu.SMEM)] + [per(a) for a in arrs]
                 + [fixed(bias.shape), fixed(w_out.shape), fixed((1, D_MODEL))],
        out_specs=per(x),
        out_shape=jax.ShapeDtypeStruct(x.shape, F32),
        scratch_shapes=[pltpu.VMEM((WIN_A, KA_W), BF16), pltpu.VMEM((WIN_A, KA_W), BF16),
                        pltpu.VMEM((WIN_B, B_W), BF16), pltpu.VMEM((WIN_B, B_W), BF16)],
        compiler_params=_params(1),
        name="attn_sample",
    )(sinks, *arrs, bias, w_out, g_post)


def _lru_body(x_ref, h0_ref, c0_ref, gpre_ref, win_ref, cw_ref, cb_ref, wa_ref, ba_ref, wx_ref,
              bx_ref, lam_ref, wout_ref, gpost_ref, y_ref, hl_ref, cl_ref, h_s, tail_s):
    t = pl.program_id(1)
    tl = x_ref.shape[1]
    n_tail = CONV_W - 1

    @pl.when(t == 0)
    def _():
        h_s[...] = h0_ref[0]
        tail_s[...] = jnp.zeros(tail_s.shape, F32)
        tail_s[8 - n_tail:, :] = c0_ref[0]

    x = x_ref[0]
    h = _rms(x, gpre_ref[...]).astype(BF16)
    xz = jnp.dot(h, win_ref[...], preferred_element_type=F32)
    xb = xz[:, :D_LRU]
    z = xz[:, D_LRU:]

    row = lax.broadcasted_iota(jnp.int32, (tl, D_LRU), 0)
    tail = tail_s[...]
    cw = cw_ref[...]
    xc = cb_ref[...] + xb * cw[CONV_W - 1:CONV_W]
    for k in range(1, CONV_W):
        prev = jnp.concatenate([pltpu.roll(tail, k, 0)] + [jnp.zeros((tl - 8, D_LRU), F32)], axis=0) \
            if tl > 8 else pltpu.roll(tail, k, 0)
        shifted = jnp.where(row < k, prev, pltpu.roll(xb, k, 0))
        xc = xc + shifted * cw[CONV_W - 1 - k:CONV_W - k]
    tail_s[...] = xb[tl - 8:, :]

    xcb = xc.astype(BF16)
    nb = D_LRU // MXU_DIM
    pre_a = jnp.concatenate(
        [jnp.dot(xcb[:, g * MXU_DIM:(g + 1) * MXU_DIM], wa_ref[g], preferred_element_type=F32)
         for g in range(nb)], axis=1)
    pre_x = jnp.concatenate(
        [jnp.dot(xcb[:, g * MXU_DIM:(g + 1) * MXU_DIM], wx_ref[g], preferred_element_type=F32)
         for g in range(nb)], axis=1)
    r = jax.nn.sigmoid(pre_a + ba_ref[...])
    gi = jax.nn.sigmoid(pre_x + bx_ref[...])
    lam = lam_ref[...]
    log_sig = jnp.minimum(lam, 0.0) - jnp.log1p(jnp.exp(-jnp.abs(lam)))
    log_a = C_GATE * r * log_sig
    a = jnp.exp(log_a)
    u = jnp.sqrt(jnp.tanh(-log_a) * (a * a + 1.0)) * (gi * xc)

    sub = row & 7
    for d in (1, 2, 4):
        keep = sub >= d
        u = jnp.where(keep, a * pltpu.roll(u, d, 0) + u, u)
        a = jnp.where(keep, a * pltpu.roll(a, d, 0), a)
    hprev = h_s[...]
    outs = []
    for g in range(tl // 8):
        hg = a[g * 8:(g + 1) * 8] * hprev + u[g * 8:(g + 1) * 8]
        outs.append(hg)
        hprev = hg[7:8]
    h_s[...] = hprev
    hs = jnp.concatenate(outs, axis=0)

    yl = (hs * _silu(z)).astype(BF16)
    m = jnp.dot(yl, wout_ref[...], preferred_element_type=F32)
    y_ref[0] = x + _rms(m, gpost_ref[...])

    @pl.when(t == pl.num_programs(1) - 1)
    def _():
        hl_ref[0] = hprev
        cl_ref[0] = xb[tl - n_tail:, :]


def _lru(x, h0, c0, gpre, w_in, cw, cb, wa, ba, wx, bx, lam, w_out, gpost, tl):
    b, seq, _ = x.shape
    n_tail = CONV_W - 1
    fixed = lambda a: pl.BlockSpec(a.shape, lambda i, j: (0,) * a.ndim)
    weights = (gpre, w_in, cw, cb, wa, ba, wx, bx, lam, w_out, gpost)
    return pl.pallas_call(
        _lru_body,
        grid=(b, seq // tl),
        in_specs=[pl.BlockSpec((1, tl, D_MODEL), lambda i, j: (i, j, 0)),
                  pl.BlockSpec((1, 1, D_LRU), lambda i, j: (i, 0, 0)),
                  pl.BlockSpec((1, n_tail, D_LRU), lambda i, j: (i, 0, 0))]
                 + [fixed(a) for a in weights],
        out_specs=[pl.BlockSpec((1, tl, D_MODEL), lambda i, j: (i, j, 0)),
                   pl.BlockSpec((1, 1, D_LRU), lambda i, j: (i, 0, 0)),
                   pl.BlockSpec((1, n_tail, D_LRU), lambda i, j: (i, 0, 0))],
        out_shape=[jax.ShapeDtypeStruct(x.shape, F32),
                   jax.ShapeDtypeStruct((b, 1, D_LRU), F32),
                   jax.ShapeDtypeStruct((b, n_tail, D_LRU), F32)],
        scratch_shapes=[pltpu.VMEM((1, D_LRU), F32), pltpu.VMEM((8, D_LRU), F32)],
        compiler_params=_params(2),
        name="lru",
    )(x, h0, c0, *weights)


def _rope_tables(pos):
    half = HEAD_DIM // 2
    inv = ROPE_THETA ** (-jnp.arange(half, dtype=F32) / half)
    ang = pos.astype(F32)[:, None] * inv[None, :]
    cos = jnp.tile(jnp.cos(ang), (1, LANES // half))
    sin = jnp.sin(ang)
    sin = jnp.tile(jnp.concatenate([-sin, sin], axis=1), (1, LANES // HEAD_DIM))
    return cos, sin


def _interleave_groups(w, axis, unit=HEAD_DIM):
    shape = w.shape
    per = H_A // KV_A
    w = w.reshape(shape[:axis] + (KV_A, per, unit) + shape[axis + 1:])
    w = jnp.swapaxes(w, axis, axis + 1)
    return w.reshape(shape)


def _block_diag(w):
    per = MXU_DIM // BLOCK
    w = w.reshape(N_BLOCKS // per, per, BLOCK, BLOCK)
    eye = jnp.eye(per, dtype=w.dtype)
    w = w[:, :, :, None, :] * eye[None, :, None, :, None]
    return w.reshape(N_BLOCKS // per, MXU_DIM, MXU_DIM)


def kernel(x_prompt, x_sample, cache_a_k, cache_a_v, cache_b_k, cache_b_v, state_c_h, state_c_conv,
           ln_pre, ln_post, w_in_ab, sinks_a, relpos_b, w_out_ab, w_in_c, conv_c_w, conv_c_b,
           gate_c_wa, gate_c_ba, gate_c_wx, gate_c_bx, lambda_c, w_out_c):
    bp, s_len, _ = x_prompt.shape
    bs, t_s, _ = x_sample.shape
    wa_rows = cache_a_k.shape[2]
    wb_rows = cache_b_k.shape[2]
    assert wa_rows + t_s <= WIN_A and wb_rows + t_s <= WIN_B and wb_rows == PAD_ROWS

    w_in = w_in_ab[0]
    w_in = jnp.concatenate([_interleave_groups(w_in[:, :QA_W], 1), w_in[:, QA_W:QA_W + 2 * KA_W],
                            _interleave_groups(w_in[:, QA_W + 2 * KA_W:2 * QA_W + 2 * KA_W], 1),
                            w_in[:, 2 * QA_W + 2 * KA_W:]], axis=1).astype(BF16)
    w_out = jnp.concatenate([_interleave_groups(w_out_ab[0, :QA_W], 0), w_out_ab[0, QA_W:]],
                            axis=0).astype(BF16)
    sinks = _interleave_groups(sinks_a[0], 0, unit=1)
    g_pre0 = ln_pre[0].reshape(1, D_MODEL)
    g_post0 = ln_post[0].reshape(1, D_MODEL)
    bias_p, bias_s, band = _build_bias(relpos_b[0], t_s, wb_rows + t_s)

    cos_p, sin_p = _rope_tables(jnp.arange(s_len, dtype=jnp.int32))
    proj = _proj0(x_prompt.reshape(bp * s_len, D_MODEL), g_pre0, w_in, cos_p, sin_p, PROJ_TM)
    qa, ka, va, ga, qb, kb, vb, gb = [a.reshape(bp, s_len, a.shape[-1]) for a in proj]
    y0_p = _attn_prompt(sinks, x_prompt, qa, ga, qb, gb, ka, va, kb, vb, bias_p, band, w_out,
                        g_post0, ATT_TQ)
    nak_p = ka[:, s_len - wa_rows:].astype(F32).reshape(1, bp, wa_rows, KV_A, HEAD_DIM)
    nav_p = va[:, s_len - wa_rows:].astype(F32).reshape(1, bp, wa_rows, KV_A, HEAD_DIM)
    nbk_p = kb[:, s_len - wb_rows:].astype(F32).reshape(1, bp, wb_rows, H_B, HEAD_DIM)
    nbv_p = vb[:, s_len - wb_rows:].astype(F32).reshape(1, bp, wb_rows, H_B, HEAD_DIM)

    n_s = bs * t_s
    tm_s = min(PROJ_TM, n_s)
    cos_s, sin_s = _rope_tables(PAST_LEN + jnp.arange(t_s, dtype=jnp.int32))
    cos_s = jnp.tile(cos_s, (tm_s // t_s, 1))
    sin_s = jnp.tile(sin_s, (tm_s // t_s, 1))
    proj = _proj0(x_sample.reshape(n_s, D_MODEL), g_pre0, w_in, cos_s, sin_s, tm_s)
    qa, ka, va, ga, qb, kb, vb, gb = [a.reshape(bs, t_s, a.shape[-1]) for a in proj]
    cak = cache_a_k[0].reshape(bs, wa_rows, KA_W)
    cav = cache_a_v[0].reshape(bs, wa_rows, KA_W)
    cbk = cache_b_k[0].reshape(bs, wb_rows, B_W)
    cbv = cache_b_v[0].reshape(bs, wb_rows, B_W)
    y0_s = _attn_sample(sinks, x_sample, qa, ga, qb, gb, ka, va, kb, vb, cak, cav, cbk, cbv,
                        bias_s, w_out, g_post0)
    roll_in = lambda cache, new, heads: jnp.concatenate(
        [cache[:, t_s:], new.astype(F32)], axis=1).reshape(1, bs, cache.shape[1], heads, HEAD_DIM)
    nak_s = roll_in(cak, ka, KV_A)
    nav_s = roll_in(cav, va, KV_A)
    nbk_s = roll_in(cbk, kb, H_B)
    nbv_s = roll_in(cbv, vb, H_B)

    row = lambda v: v.reshape(1, -1)
    lru_w = (row(ln_pre[1]), w_in_c[0].astype(BF16), conv_c_w[0], row(conv_c_b[0]),
             _block_diag(gate_c_wa[0]).astype(BF16), row(gate_c_ba[0]),
             _block_diag(gate_c_wx[0]).astype(BF16), row(gate_c_bx[0]),
             row(lambda_c[0]), w_out_c[0].astype(BF16), row(ln_post[1]))
    y1_p, hl_p, cl_p = _lru(y0_p, jnp.zeros((bp, 1, D_LRU), F32),
                            jnp.zeros((bp, CONV_W - 1, D_LRU), F32), *lru_w, LRU_TL)
    y1_s, hl_s, cl_s = _lru(y0_s, state_c_h[0].reshape(bs, 1, D_LRU), state_c_conv[0], *lru_w, t_s)

    return (y1_p, y1_s, nak_p, nav_p, nbk_p, nbv_p,
            hl_p.reshape(1, bp, D_LRU), cl_p.reshape(1, bp, CONV_W - 1, D_LRU),
            nak_s, nav_s, nbk_s, nbv_s,
            hl_s.reshape(1, bs, D_LRU), cl_s.reshape(1, bs, CONV_W - 1, D_LRU))
```

```python
import functools

import jax
import jax.numpy as jnp
import numpy as np
from jax import lax
from jax.experimental import pallas as pl
from jax.experimental.pallas import tpu as pltpu

F32 = jnp.float32
BF16 = jnp.bfloat16

D_MODEL = 1024
CHUNK = 64
HEAD_DIM = 64
H_A = 8
KV_A = 2
N_PREV_A = 2
H_B = 8
N_PREV_B = 8
MAX_REL = 128
ROPE_THETA = 10000.0
D_LRU = D_MODEL
N_BLOCKS = 16
BLOCK = D_LRU // N_BLOCKS
CONV_W = 4
C_GATE = 8.0
EPS = 1e-6
PAST_LEN = 1024
NEG = -1e30

LANES = 128
MXU_DIM = 256
VMEM_LIMIT = 56 * 1024 * 1024

QA_W = H_A * HEAD_DIM
KA_W = KV_A * HEAD_DIM
B_W = H_B * HEAD_DIM
MIX_W = QA_W + B_W
GRP = 2 * CHUNK
WIN_A = (N_PREV_A + 2) * CHUNK
WIN_B = (N_PREV_B + 2) * CHUNK
PAD_ROWS = N_PREV_B * CHUNK
TOEP_W = WIN_B + LANES

PROJ_TM = 512
ATT_TQ = 256


def _params(n_axes):
    return pltpu.CompilerParams(
        dimension_semantics=("arbitrary",) * n_axes,
        vmem_limit_bytes=VMEM_LIMIT)


def _rms(x, g):
    ms = jnp.mean(x * x, axis=-1, keepdims=True)
    return x * lax.rsqrt(ms + EPS) * g


def _silu(x):
    return x * jax.nn.sigmoid(x)


def _proj0_body(x_ref, g_ref, w_ref, cos_ref, sin_ref,
                qa_ref, ka_ref, va_ref, ga_ref, qb_ref, kb_ref, vb_ref, gb_ref):
    h = _rms(x_ref[...], g_ref[...]).astype(BF16)
    cos = cos_ref[...]
    sin = sin_ref[...]
    tm = cos.shape[0]
    lane = lax.broadcasted_iota(jnp.int32, (tm, LANES), 1)
    lower = (lane & (HEAD_DIM - 1)) < (HEAD_DIM // 2)

    def mm(lo, hi):
        return jnp.dot(h, w_ref[:, lo:hi], preferred_element_type=F32)

    def rope(t):
        partner = jnp.where(lower, pltpu.roll(t, LANES - HEAD_DIM // 2, 1),
                            pltpu.roll(t, HEAD_DIM // 2, 1))
        return t * cos + partner * sin

    o = 0
    t = mm(o, o + QA_W)
    for k in range(QA_W // LANES):
        sl = slice(k * LANES, (k + 1) * LANES)
        qa_ref[:, sl] = (rope(t[:, sl]) * (HEAD_DIM ** -0.5)).astype(BF16)
    o += QA_W
    ka_ref[...] = rope(mm(o, o + KA_W)).astype(BF16)
    o += KA_W
    va_ref[...] = mm(o, o + KA_W).astype(BF16)
    o += KA_W
    ga_ref[...] = _silu(mm(o, o + QA_W)).astype(BF16)
    o += QA_W
    qb_ref[...] = (mm(o, o + B_W) * (HEAD_DIM ** -0.5)).astype(BF16)
    o += B_W
    kb_ref[...] = mm(o, o + B_W).astype(BF16)
    o += B_W
    vb_ref[...] = mm(o, o + B_W).astype(BF16)
    o += B_W
    gb_ref[...] = _silu(mm(o, o + B_W)).astype(BF16)


def _proj0(x2d, g, w, cos, sin, tm):
    n = x2d.shape[0]
    n_pos_blocks = cos.shape[0] // tm
    widths = (QA_W, KA_W, KA_W, QA_W, B_W, B_W, B_W, B_W)
    row = lambda i: (i, 0)
    fixed = lambda i: (0, 0)
    pos = lambda i: (i % n_pos_blocks, 0)
    return pl.pallas_call(
        _proj0_body,
        grid=(n // tm,),
        in_specs=[pl.BlockSpec((tm, D_MODEL), row),
                  pl.BlockSpec((1, D_MODEL), fixed),
                  pl.BlockSpec(w.shape, fixed),
                  pl.BlockSpec((tm, LANES), pos),
                  pl.BlockSpec((tm, LANES), pos)],
        out_specs=[pl.BlockSpec((tm, wd), row) for wd in widths],
        out_shape=[jax.ShapeDtypeStruct((n, wd), BF16) for wd in widths],
        compiler_params=_params(1),
        name="proj0",
    )(x2d, g, w, cos, sin)


def _bias_body(r0_ref, bp_ref, bs_ref, band_ref, *, t_sample, keys_sample):
    x = jnp.broadcast_to(r0_ref[0], (GRP, TOEP_W))
    row = lax.broadcasted_iota(jnp.int32, (GRP, TOEP_W), 0)
    for b in range(GRP.bit_length() - 1):
        x = jnp.where(((row >> b) & 1) == 1, pltpu.roll(x, 1 << b, 1), x)
    x = x[:, :WIN_B]
    r = lax.broadcasted_iota(jnp.int32, (GRP, WIN_B), 0)
    c = lax.broadcasted_iota(jnp.int32, (GRP, WIN_B), 1)
    ci = r // CHUNK
    cj = c // CHUNK
    bp_ref[0] = jnp.where((cj >= ci) & (cj <= ci + N_PREV_B), x, NEG)
    cs = lax.broadcasted_iota(jnp.int32, (t_sample, WIN_B), 1)
    bs_ref[0] = jnp.where(cs < keys_sample, x[:t_sample], NEG)
    ra = lax.broadcasted_iota(jnp.int32, (GRP, WIN_A), 0) // CHUNK
    ca = lax.broadcasted_iota(jnp.int32, (GRP, WIN_A), 1) // CHUNK
    band_ref[...] = jnp.where((ca >= ra) & (ca <= ra + N_PREV_A), 0.0, NEG)


def _build_bias(relpos, t_sample, keys_sample):
    far = relpos[:, 2 * MAX_REL:]
    n_far = PAD_ROWS - MAX_REL + 1
    r0 = jnp.concatenate(
        [jnp.broadcast_to(far, (H_B, n_far)),
         relpos[:, 2 * MAX_REL - 1:0:-1],
         jnp.broadcast_to(far, (H_B, TOEP_W - n_far - (2 * MAX_REL - 1)))], axis=1)
    r0 = r0.reshape(H_B, 1, TOEP_W)
    return pl.pallas_call(
        functools.partial(_bias_body, t_sample=t_sample, keys_sample=keys_sample),
        grid=(H_B,),
        in_specs=[pl.BlockSpec((1, 1, TOEP_W), lambda h: (h, 0, 0))],
        out_specs=[pl.BlockSpec((1, GRP, WIN_B), lambda h: (h, 0, 0)),
                   pl.BlockSpec((1, t_sample, WIN_B), lambda h: (h, 0, 0)),
                   pl.BlockSpec((GRP, WIN_A), lambda h: (0, 0))],
        out_shape=[jax.ShapeDtypeStruct((H_B, GRP, WIN_B), F32),
                   jax.ShapeDtypeStruct((H_B, t_sample, WIN_B), F32),
                   jax.ShapeDtypeStruct((GRP, WIN_A), F32)],
        compiler_params=_params(1),
        name="relbias",
    )(r0)


def _nt_dot(a, b):
    return lax.dot_general(a, b, (((1,), (1,)), ((), ())), preferred_element_type=F32)


def _attend_group(qa, qb, ka, va, kb_ref, vb_ref, win_b, sink_ref, bias_ref, mask_a, col_ok_a,
                  col_ok_b):
    n = qa.shape[0]
    lane = lax.broadcasted_iota(jnp.int32, (n, LANES), 1)
    lo = lane < HEAD_DIM
    zero = jnp.zeros((n, LANES), BF16)

    rows = []
    for p in range(QA_W // LANES):
        tile = qa[:, p * LANES:(p + 1) * LANES]
        rows.append(jnp.where(lo, tile, zero))
        rows.append(jnp.where(lo, zero, tile))
    s_all = _nt_dot(jnp.concatenate(rows, axis=0), ka)
    probs, inv = [], []
    for h in range(H_A):
        s = s_all[h * n:(h + 1) * n]
        if mask_a is not None:
            s = s + mask_a
        if col_ok_a is not None:
            s = jnp.where(col_ok_a, s, NEG)
        sink = sink_ref[h]
        m = jnp.maximum(jnp.max(s, axis=-1, keepdims=True), sink)
        e = jnp.exp(s - m)
        den = jnp.sum(e, axis=-1, keepdims=True) + jnp.exp(sink - m)
        probs.append(e.astype(BF16))
        inv.append(1.0 / den)
    r_all = jnp.dot(jnp.concatenate(probs, axis=0), va, preferred_element_type=F32)
    o_a = []
    for p in range(QA_W // LANES):
        r0 = r_all[(2 * p) * n:(2 * p + 1) * n] * inv[2 * p]
        r1 = r_all[(2 * p + 1) * n:(2 * p + 2) * n] * inv[2 * p + 1]
        o_a.append(jnp.where(lo, r0, r1))
    o_a = jnp.concatenate(o_a, axis=1)

    o_b = []
    for p in range(B_W // LANES):
        sl = slice(p * LANES, (p + 1) * LANES)
        tile = qb[:, sl]
        lhs = jnp.concatenate([jnp.where(lo, tile, zero), jnp.where(lo, zero, tile)], axis=0)
        s2 = _nt_dot(lhs, kb_ref[win_b, sl])
        halves = []
        for j in range(2):
            s = s2[j * n:(j + 1) * n] + bias_ref[2 * p + j]
            if col_ok_b is not None:
                s = jnp.where(col_ok_b, s, NEG)
            m = jnp.max(s, axis=-1, keepdims=True)
            e = jnp.exp(s - m)
            den = jnp.sum(e, axis=-1, keepdims=True)
            r = jnp.dot(e.astype(BF16), vb_ref[win_b, sl], preferred_element_type=F32)
            halves.append(r * (1.0 / den))
        o_b.append(jnp.where(lo, halves[0], halves[1]))
    o_b = jnp.concatenate(o_b, axis=1)
    return o_a, o_b


def _mix_out(o_a, o_b, ga, gb, x, w_ref, g_ref):
    mix = jnp.concatenate([o_a * ga.astype(F32), o_b * gb.astype(F32)], axis=1).astype(BF16)
    m = jnp.dot(mix, w_ref[...], preferred_element_type=F32)
    return x + _rms(m, g_ref[...])


def _attn_prompt_body(sink_ref, x_ref, qa_ref, ga_ref, qb_ref, gb_ref, ka_ref, va_ref, kb_ref,
                      vb_ref, bias_ref, band_ref, w_ref, g_ref, y_ref,
                      kap, vap, kbp, vbp, *, seq):
    t = pl.program_id(1)
    tq = x_ref.shape[1]

    @pl.when(t == 0)
    def _():
        for dst, src in ((kap, ka_ref), (vap, va_ref), (kbp, kb_ref), (vbp, vb_ref)):
            dst[:PAD_ROWS, :] = jnp.zeros((PAD_ROWS, dst.shape[1]), BF16)
            dst[PAD_ROWS:, :] = src[0]

    band = band_ref[...]
    col_a = lax.broadcasted_iota(jnp.int32, (1, WIN_A), 1)
    col_b = lax.broadcasted_iota(jnp.int32, (1, WIN_B), 1)
    for sub in range(tq // GRP):
        start = pl.multiple_of(t * tq + sub * GRP, GRP)
        rs = slice(sub * GRP, (sub + 1) * GRP)
        win_a = pl.ds(pl.multiple_of(start + (PAD_ROWS - N_PREV_A * CHUNK), GRP), WIN_A)
        win_b = pl.ds(start, WIN_B)
        ok_a = col_a >= N_PREV_A * CHUNK - start
        ok_b = col_b >= N_PREV_B * CHUNK - start
        o_a, o_b = _attend_group(qa_ref[0, rs, :], qb_ref[0, rs, :], kap[win_a, :], vap[win_a, :],
                                 kbp, vbp, win_b, sink_ref, bias_ref, band, ok_a, ok_b)
        y_ref[0, rs, :] = _mix_out(o_a, o_b, ga_ref[0, rs, :], gb_ref[0, rs, :], x_ref[0, rs, :],
                                   w_ref, g_ref)


def _attn_prompt(sinks, x, qa, ga, qb, gb, ka, va, kb, vb, bias, band, w_out, g_post, tq):
    b, seq, _ = x.shape
    tile = lambda w: pl.BlockSpec((1, tq, w), lambda i, j: (i, j, 0))
    whole = lambda w: pl.BlockSpec((1, seq, w), lambda i, j: (i, 0, 0))
    fixed = lambda shape: pl.BlockSpec(shape, lambda i, j: (0,) * len(shape))
    return pl.pallas_call(
        functools.partial(_attn_prompt_body, seq=seq),
        grid=(b, seq // tq),
        in_specs=[pl.BlockSpec(memory_space=pltpu.SMEM),
                  tile(D_MODEL), tile(QA_W), tile(QA_W), tile(B_W), tile(B_W),
                  whole(KA_W), whole(KA_W), whole(B_W), whole(B_W),
                  fixed(bias.shape), fixed(band.shape), fixed(w_out.shape), fixed((1, D_MODEL))],
        out_specs=tile(D_MODEL),
        out_shape=jax.ShapeDtypeStruct(x.shape, F32),
        scratch_shapes=[pltpu.VMEM((PAD_ROWS + seq, KA_W), BF16),
                        pltpu.VMEM((PAD_ROWS + seq, KA_W), BF16),
                        pltpu.VMEM((PAD_ROWS + seq, B_W), BF16),
                        pltpu.VMEM((PAD_ROWS + seq, B_W), BF16)],
        compiler_params=_params(2),
        name="attn_prompt",
    )(sinks, x, qa, ga, qb, gb, ka, va, kb, vb, bias, band, w_out, g_post)


SAMPLE_REQS = 4


def _softmax2(s_c, s_n, sink):
    m = jnp.maximum(jnp.max(s_c, axis=-1, keepdims=True), jnp.max(s_n, axis=-1, keepdims=True))
    if sink is not None:
        m = jnp.maximum(m, sink)
    e_c = jnp.exp(s_c - m)
    e_n = jnp.exp(s_n - m)
    den = jnp.sum(e_c, axis=-1, keepdims=True) + jnp.sum(e_n, axis=-1, keepdims=True)
    if sink is not None:
        den = den + jnp.exp(sink - m)
    return e_c.astype(BF16), e_n.astype(BF16), 1.0 / den


def _attn_sample_body(sink_ref, x_ref, qa_ref, ga_ref, qb_ref, gb_ref, ka_ref, va_ref, kb_ref,
                      vb_ref, cak_ref, cav_ref, cbk_ref, cbv_ref, bias_ref, w_ref, g_ref, y_ref):
    nreq, t, _ = x_ref.shape
    wb = cbk_ref.shape[2]
    lane = lax.broadcasted_iota(jnp.int32, (t, LANES), 1)
    lo = lane < HEAD_DIM
    zero = jnp.zeros((t, LANES), BF16)
    halves = lambda tile: (jnp.where(lo, tile, zero), jnp.where(lo, zero, tile))

    mix = []
    for r in range(nreq):
        qa = qa_ref[r]
        lhs = jnp.concatenate([h for p in range(QA_W // LANES)
                               for h in halves(qa[:, p * LANES:(p + 1) * LANES])], axis=0)
        s_c = jnp.dot(lhs, cak_ref[r].astype(BF16), preferred_element_type=F32)
        s_n = _nt_dot(lhs, ka_ref[r])
        cav = cav_ref[r].astype(BF16)
        va = va_ref[r]
        o_a = []
        for p in range(QA_W // LANES):
            pair = []
            for j in range(2):
                h = 2 * p + j
                rs = slice(h * t, (h + 1) * t)
                e_c, e_n, inv = _softmax2(s_c[rs], s_n[rs], sink_ref[h])
                o = _nt_dot(e_c, cav) + jnp.dot(e_n, va, preferred_element_type=F32)
                pair.append(o * inv)
            o_a.append(jnp.where(lo, pair[0], pair[1]))
        qb = qb_ref[r]
        o_b = []
        for p in range(B_W // LANES):
            sl = slice(p * LANES, (p + 1) * LANES)
            lhs = jnp.concatenate(halves(qb[:, sl]), axis=0)
            s_c = jnp.dot(lhs, cbk_ref[r, sl, :].astype(BF16), preferred_element_type=F32)
            s_n = _nt_dot(lhs, kb_ref[r, :, sl])
            cbv = cbv_ref[r, sl, :].astype(BF16)
            vb = vb_ref[r, :, sl]
            pair = []
            for j in range(2):
                rs = slice(j * t, (j + 1) * t)
                bias = bias_ref[2 * p + j]
                e_c, e_n, inv = _softmax2(s_c[rs] + bias[:, :wb], s_n[rs] + bias[:, wb:wb + t], None)
                o = _nt_dot(e_c, cbv) + jnp.dot(e_n, vb, preferred_element_type=F32)
                pair.append(o * inv)
            o_b.append(jnp.where(lo, pair[0], pair[1]))
        mix.append(jnp.concatenate(
            [jnp.concatenate(o_a, axis=1) * ga_ref[r].astype(F32),
             jnp.concatenate(o_b, axis=1) * gb_ref[r].astype(F32)], axis=1).astype(BF16))
    m = jnp.dot(jnp.concatenate(mix, axis=0), w_ref[...], preferred_element_type=F32)
    y = x_ref[...].reshape(nreq * t, D_MODEL) + _rms(m, g_ref[...])
    y_ref[...] = y.reshape(nreq, t, D_MODEL)


def _attn_sample(sinks, x, qa, ga, qb, gb, ka, va, kb, vb, cak, cav, cbk, cbv, bias, w_out, g_post):
    b = x.shape[0]
    per = lambda a: pl.BlockSpec((SAMPLE_REQS,) + a.shape[1:], lambda i: (i, 0, 0))
    fixed = lambda shape: pl.BlockSpec(shape, lambda i: (0,) * len(shape))
    arrs = (x, qa, ga, qb, gb, ka, va, kb, vb, cak, cav, cbk, cbv)
    return pl.pallas_call(
        _attn_sample_body,
        grid=(b // SAMPLE_REQS,),
        in_specs=[pl.BlockSpec(memory_space=pltpu.SMEM)] + [per(a) for a in arrs]
                 + [fixed(bias.shape), fixed(w_out.shape), fixed((1, D_MODEL))],
        out_specs=per(x),
        out_shape=jax.ShapeDtypeStruct(x.shape, F32),
        compiler_params=_params(1),
        name="attn_sample",
    )(sinks, *arrs, bias, w_out, g_post)


LRU_SEG = 8
LRU_STEPS = 32
LRU_TILE = LRU_SEG * LRU_STEPS
N_TAIL = CONV_W - 1


def _lru_body(x_ref, h0_ref, c0_ref, p_ref, pt_ref, gpre_ref, win_ref, cw_ref, cb_ref, wa_ref, ba_ref,
              wx_ref, bx_ref, lam_ref, wout_ref, gpost_ref, y_ref, hl_ref, cl_ref, h_s, tail_s, *,
              chained):
    t = pl.program_id(1)
    grp = lambda v, g, n=1: v[g * LRU_SEG:(g + n) * LRU_SEG]

    if chained:
        @pl.when(t == 0)
        def _():
            h_s[...] = h0_ref[0]
            tail_s[...] = jnp.zeros(tail_s.shape, F32)
            for j in range(N_TAIL):
                tail_s[j * LRU_SEG + LRU_SEG - 1:(j + 1) * LRU_SEG, :] = c0_ref[0, j:j + 1, :]

    x = x_ref[0]
    h = _rms(x, gpre_ref[...]).astype(BF16)
    hp = jnp.dot(p_ref[...], h, preferred_element_type=F32).astype(BF16)
    xz = jnp.dot(hp, win_ref[...], preferred_element_type=F32)
    xb = xz[:, :D_LRU]
    z = xz[:, D_LRU:]

    if chained:
        sub = lax.broadcasted_iota(jnp.int32, (LRU_SEG, D_LRU), 0)
        before = [jnp.where(sub == 0, pltpu.roll(grp(tail_s[...], j), 1, 0),
                            pltpu.roll(grp(xb, LRU_STEPS - N_TAIL + j), 1, 0))
                  for j in range(N_TAIL)]
        tail_s[...] = grp(xb, LRU_STEPS - N_TAIL, N_TAIL)
    else:
        before = [c0_ref[j] for j in range(N_TAIL)]

    cw = cw_ref[...]
    xc = cb_ref[...] + xb * cw[CONV_W - 1:CONV_W]
    for k in range(1, CONV_W):
        shifted = jnp.concatenate(before[N_TAIL - k:] + [grp(xb, 0, LRU_STEPS - k)], axis=0)
        xc = xc + shifted * cw[CONV_W - 1 - k:CONV_W - k]

    xcb = xc.astype(BF16)
    nb = D_LRU // MXU_DIM
    pre_a = jnp.concatenate(
        [jnp.dot(xcb[:, g * MXU_DIM:(g + 1) * MXU_DIM], wa_ref[g], preferred_element_type=F32)
         for g in range(nb)], axis=1)
    pre_x = jnp.concatenate(
        [jnp.dot(xcb[:, g * MXU_DIM:(g + 1) * MXU_DIM], wx_ref[g], preferred_element_type=F32)
         for g in range(nb)], axis=1)
    r = jax.nn.sigmoid(pre_a + ba_ref[...])
    gi = jax.nn.sigmoid(pre_x + bx_ref[...])
    lam = lam_ref[...]
    log_sig = jnp.minimum(lam, 0.0) - jnp.log1p(jnp.exp(-jnp.abs(lam)))
    log_a = r * (C_GATE * log_sig)
    a = jnp.exp(log_a)
    u = jnp.sqrt(jnp.tanh(-log_a) * (a * a + 1.0)) * (gi * xc)

    hc, ac = grp(u, 0), grp(a, 0)
    h_loc, a_loc = [hc], [ac]
    for g in range(1, LRU_STEPS):
        ag = grp(a, g)
        hc = ag * hc + grp(u, g)
        ac = ag * ac
        h_loc.append(hc)
        a_loc.append(ac)

    if chained:
        c = h_s[...]
        rows = []
        for s in range(LRU_SEG):
            rows.append(c)
            c = ac[s:s + 1] * c + hc[s:s + 1]
        h_s[...] = c
        carry = jnp.concatenate(rows, axis=0)
    else:
        carry = h0_ref[0]
    hs = jnp.concatenate([hl + al * carry for hl, al in zip(h_loc, a_loc)], axis=0)

    yl = (hs * _silu(z)).astype(BF16)
    yl = jnp.dot(pt_ref[...], yl, preferred_element_type=F32).astype(BF16)
    m = jnp.dot(yl, wout_ref[...], preferred_element_type=F32)
    y_ref[0] = x + _rms(m, gpost_ref[...])

    if chained:
        @pl.when(t == pl.num_programs(1) - 1)
        def _():
            hl_ref[0] = c
            last = LRU_SEG - 1
            cl_ref[0] = jnp.concatenate(
                [grp(xb, LRU_STEPS - N_TAIL + j)[last:last + 1] for j in range(N_TAIL)], axis=0)
    else:
        hl_ref[0] = grp(hs, LRU_STEPS - 1)
        for j in range(N_TAIL):
            cl_ref[j] = grp(xb, LRU_STEPS - N_TAIL + j)


def _lru(x, h0, c0, weights, chained):
    nb, seq, _ = x.shape
    fixed = lambda a: pl.BlockSpec(a.shape, lambda i, j: (0,) * a.ndim)
    perm = np.zeros((LRU_TILE, LRU_TILE), np.float32)
    rho = np.arange(LRU_TILE)
    perm[rho, (rho % LRU_SEG) * LRU_STEPS + rho // LRU_SEG] = 1.0
    perm_mats = (jnp.asarray(perm, BF16), jnp.asarray(perm.T, BF16))
    if chained:
        state_specs = [pl.BlockSpec((1, 1, D_LRU), lambda i, j: (i, 0, 0)),
                       pl.BlockSpec((1, N_TAIL, D_LRU), lambda i, j: (i, 0, 0))]
        state_shapes = [jax.ShapeDtypeStruct((nb, 1, D_LRU), F32),
                        jax.ShapeDtypeStruct((nb, N_TAIL, D_LRU), F32)]
    else:
        assert seq == LRU_TILE
        state_specs = [pl.BlockSpec((1, LRU_SEG, D_LRU), lambda i, j: (i, 0, 0)),
                       pl.BlockSpec((N_TAIL, LRU_SEG, D_LRU), lambda i, j: (0, i, 0))]
        state_shapes = [jax.ShapeDtypeStruct((nb, LRU_SEG, D_LRU), F32),
                        jax.ShapeDtypeStruct((N_TAIL, nb * LRU_SEG, D_LRU), F32)]
    tile = pl.BlockSpec((1, LRU_TILE, D_MODEL), lambda i, j: (i, j, 0))
    return pl.pallas_call(
        functools.partial(_lru_body, chained=chained),
        grid=(nb, seq // LRU_TILE),
        in_specs=[tile] + state_specs + [fixed(a) for a in perm_mats + weights],
        out_specs=[tile] + state_specs,
        out_shape=[jax.ShapeDtypeStruct(x.shape, F32)] + state_shapes,
        scratch_shapes=[pltpu.VMEM((1, D_LRU), F32), pltpu.VMEM((N_TAIL * LRU_SEG, D_LRU), F32)],
        compiler_params=_params(2),
        name="lru_chained" if chained else "lru_batched",
    )(x, h0, c0, *perm_mats, *weights)


def _rope_tables(pos):
    half = HEAD_DIM // 2
    inv = ROPE_THETA ** (-jnp.arange(half, dtype=F32) / half)
    ang = pos.astype(F32)[:, None] * inv[None, :]
    cos = jnp.tile(jnp.cos(ang), (1, LANES // half))
    sin = jnp.sin(ang)
    sin = jnp.tile(jnp.concatenate([-sin, sin], axis=1), (1, LANES // HEAD_DIM))
    return cos, sin


def _interleave_groups(w, axis, unit=HEAD_DIM):
    shape = w.shape
    per = H_A // KV_A
    w = w.reshape(shape[:axis] + (KV_A, per, unit) + shape[axis + 1:])
    w = jnp.swapaxes(w, axis, axis + 1)
    return w.reshape(shape)


def _block_diag(w):
    per = MXU_DIM // BLOCK
    w = w.reshape(N_BLOCKS // per, per, BLOCK, BLOCK)
    eye = jnp.eye(per, dtype=w.dtype)
    w = w[:, :, :, None, :] * eye[None, :, None, :, None]
    return w.reshape(N_BLOCKS // per, MXU_DIM, MXU_DIM)


def kernel(x_prompt, x_sample, cache_a_k, cache_a_v, cache_b_k, cache_b_v, state_c_h, state_c_conv,
           ln_pre, ln_post, w_in_ab, sinks_a, relpos_b, w_out_ab, w_in_c, conv_c_w, conv_c_b,
           gate_c_wa, gate_c_ba, gate_c_wx, gate_c_bx, lambda_c, w_out_c):
    bp, s_len, _ = x_prompt.shape
    bs, t_s, _ = x_sample.shape
    wa_rows = cache_a_k.shape[2]
    wb_rows = cache_b_k.shape[2]
    assert wa_rows + t_s <= WIN_A and wb_rows + t_s <= WIN_B and wb_rows == PAD_ROWS

    w_in = w_in_ab[0]
    w_in = jnp.concatenate([_interleave_groups(w_in[:, :QA_W], 1), w_in[:, QA_W:QA_W + 2 * KA_W],
                            _interleave_groups(w_in[:, QA_W + 2 * KA_W:2 * QA_W + 2 * KA_W], 1),
                            w_in[:, 2 * QA_W + 2 * KA_W:]], axis=1).astype(BF16)
    w_out = jnp.concatenate([_interleave_groups(w_out_ab[0, :QA_W], 0), w_out_ab[0, QA_W:]],
                            axis=0).astype(BF16)
    sinks = _interleave_groups(sinks_a[0], 0, unit=1)
    g_pre0 = ln_pre[0].reshape(1, D_MODEL)
    g_post0 = ln_post[0].reshape(1, D_MODEL)
    bias_p, bias_s, band = _build_bias(relpos_b[0], t_s, wb_rows + t_s)

    cos_p, sin_p = _rope_tables(jnp.arange(s_len, dtype=jnp.int32))
    proj = _proj0(x_prompt.reshape(bp * s_len, D_MODEL), g_pre0, w_in, cos_p, sin_p, PROJ_TM)
    qa, ka, va, ga, qb, kb, vb, gb = [a.reshape(bp, s_len, a.shape[-1]) for a in proj]
    y0_p = _attn_prompt(sinks, x_prompt, qa, ga, qb, gb, ka, va, kb, vb, bias_p, band, w_out,
                        g_post0, ATT_TQ)
    nak_p = ka[:, s_len - wa_rows:].astype(F32).reshape(1, bp, wa_rows, KV_A, HEAD_DIM)
    nav_p = va[:, s_len - wa_rows:].astype(F32).reshape(1, bp, wa_rows, KV_A, HEAD_DIM)
    nbk_p = kb[:, s_len - wb_rows:].astype(F32).reshape(1, bp, wb_rows, H_B, HEAD_DIM)
    nbv_p = vb[:, s_len - wb_rows:].astype(F32).reshape(1, bp, wb_rows, H_B, HEAD_DIM)

    n_s = bs * t_s
    tm_s = min(PROJ_TM, n_s)
    cos_s, sin_s = _rope_tables(PAST_LEN + jnp.arange(t_s, dtype=jnp.int32))
    cos_s = jnp.tile(cos_s, (tm_s // t_s, 1))
    sin_s = jnp.tile(sin_s, (tm_s // t_s, 1))
    proj = _proj0(x_sample.reshape(n_s, D_MODEL), g_pre0, w_in, cos_s, sin_s, tm_s)
    qa, ka, va, ga, qb, kb, vb, gb = [a.reshape(bs, t_s, a.shape[-1]) for a in proj]
    fmaj = lambda c: jnp.transpose(c[0], (0, 2, 3, 1))
    cak, cav, cbk, cbv = fmaj(cache_a_k), fmaj(cache_a_v), fmaj(cache_b_k), fmaj(cache_b_v)
    flat = lambda c: c.reshape(bs, c.shape[1] * HEAD_DIM, c.shape[3])
    y0_s = _attn_sample(sinks, x_sample, qa, ga, qb, gb, ka, va, kb, vb, flat(cak), flat(cav),
                        flat(cbk), flat(cbv), bias_s, w_out, g_post0)

    def roll_in(cache, new):
        heads = cache.shape[1]
        new = jnp.transpose(new.reshape(bs, t_s, heads, HEAD_DIM), (0, 2, 3, 1)).astype(F32)
        out = jnp.concatenate([cache[..., t_s:], new], axis=-1)
        return jnp.transpose(out, (0, 3, 1, 2))[None]

    nak_s = roll_in(cak, ka)
    nav_s = roll_in(cav, va)
    nbk_s = roll_in(cbk, kb)
    nbv_s = roll_in(cbv, vb)

    row = lambda v: v.reshape(1, -1)
    lru_w = (row(ln_pre[1]), w_in_c[0].astype(BF16), conv_c_w[0], row(conv_c_b[0]),
             _block_diag(gate_c_wa[0]).astype(BF16), row(gate_c_ba[0]),
             _block_diag(gate_c_wx[0]).astype(BF16), row(gate_c_bx[0]),
             row(lambda_c[0]), w_out_c[0].astype(BF16), row(ln_post[1]))
    y1_p, hl_p, cl_p = _lru(y0_p, jnp.zeros((bp, 1, D_LRU), F32), jnp.zeros((bp, N_TAIL, D_LRU), F32),
                            lru_w, chained=True)
    assert t_s == LRU_STEPS and bs % LRU_SEG == 0
    y1_s, hl_s, cl_s = _lru(y0_s.reshape(bs // LRU_SEG, LRU_TILE, D_MODEL),
                            state_c_h[0].reshape(bs // LRU_SEG, LRU_SEG, D_LRU),
                            jnp.transpose(state_c_conv[0], (1, 0, 2)), lru_w, chained=False)
    y1_s = y1_s.reshape(bs, t_s, D_MODEL)
    cl_s = jnp.transpose(cl_s, (1, 0, 2))

    return (y1_p, y1_s, nak_p, nav_p, nbk_p, nbv_p,
            hl_p.reshape(1, bp, D_LRU), cl_p.reshape(1, bp, N_TAIL, D_LRU),
            nak_s, nav_s, nbk_s, nbv_s,
            hl_s.reshape(1, bs, D_LRU), cl_s.reshape(1, bs, N_TAIL, D_LRU))
```

```python
import functools

import jax
import jax.numpy as jnp
import numpy as np
from jax import lax
from jax.experimental import pallas as pl
from jax.experimental.pallas import tpu as pltpu

F32 = jnp.float32
BF16 = jnp.bfloat16

D_MODEL = 1024
CHUNK = 64
HEAD_DIM = 64
H_A = 8
KV_A = 2
N_PREV_A = 2
H_B = 8
N_PREV_B = 8
MAX_REL = 128
ROPE_THETA = 10000.0
D_LRU = D_MODEL
N_BLOCKS = 16
BLOCK = D_LRU // N_BLOCKS
CONV_W = 4
C_GATE = 8.0
EPS = 1e-6
PAST_LEN = 1024
NEG = -1e30

LANES = 128
MXU_DIM = 256
VMEM_LIMIT = 56 * 1024 * 1024

QA_W = H_A * HEAD_DIM
KA_W = KV_A * HEAD_DIM
B_W = H_B * HEAD_DIM
MIX_W = QA_W + B_W
GRP = 2 * CHUNK
WIN_A = (N_PREV_A + 2) * CHUNK
WIN_B = (N_PREV_B + 2) * CHUNK
PAD_ROWS = N_PREV_B * CHUNK
TOEP_W = WIN_B + LANES

PROJ_TM = 512
ATT_TQ = 256


def _params(n_axes):
    return pltpu.CompilerParams(
        dimension_semantics=("arbitrary",) * n_axes,
        vmem_limit_bytes=VMEM_LIMIT)


def _rms(x, g):
    ms = jnp.mean(x * x, axis=-1, keepdims=True)
    return x * lax.rsqrt(ms + EPS) * g


def _silu(x):
    return x * jax.nn.sigmoid(x)


def _proj0_body(x_ref, g_ref, w_ref, cos_ref, sin_ref,
                qa_ref, ka_ref, va_ref, ga_ref, qb_ref, kb_ref, vb_ref, gb_ref, *cache_refs,
                tiles_per_seq, wa_rows):
    h = _rms(x_ref[...], g_ref[...]).astype(BF16)
    cos = cos_ref[...]
    sin = sin_ref[...]
    tm = cos.shape[0]
    lane = lax.broadcasted_iota(jnp.int32, (tm, LANES), 1)
    lower = (lane & (HEAD_DIM - 1)) < (HEAD_DIM // 2)

    def mm(lo, hi):
        return jnp.dot(h, w_ref[:, lo:hi], preferred_element_type=F32)

    def rope(t):
        partner = jnp.where(lower, pltpu.roll(t, LANES - HEAD_DIM // 2, 1),
                            pltpu.roll(t, HEAD_DIM // 2, 1))
        return t * cos + partner * sin

    o = 0
    t = mm(o, o + QA_W)
    for k in range(QA_W // LANES):
        sl = slice(k * LANES, (k + 1) * LANES)
        qa_ref[:, sl] = (rope(t[:, sl]) * (HEAD_DIM ** -0.5)).astype(BF16)
    o += QA_W
    ka = rope(mm(o, o + KA_W))
    ka_ref[...] = ka.astype(BF16)
    o += KA_W
    va = mm(o, o + KA_W)
    va_ref[...] = va.astype(BF16)
    o += KA_W
    ga_ref[...] = _silu(mm(o, o + QA_W)).astype(BF16)
    o += QA_W
    qb_ref[...] = (mm(o, o + B_W) * (HEAD_DIM ** -0.5)).astype(BF16)
    o += B_W
    kb = mm(o, o + B_W)
    kb_ref[...] = kb.astype(BF16)
    o += B_W
    vb = mm(o, o + B_W)
    vb_ref[...] = vb.astype(BF16)
    o += B_W
    gb_ref[...] = _silu(mm(o, o + B_W)).astype(BF16)

    if cache_refs:
        @pl.when(pl.program_id(0) % tiles_per_seq == tiles_per_seq - 1)
        def _():
            kat_ref, vat_ref, kbt_ref, vbt_ref = cache_refs
            kat_ref[0] = ka[tm - wa_rows:].T
            vat_ref[0] = va[tm - wa_rows:].T
            kbt_ref[0] = kb.T
            vbt_ref[0] = vb.T


def _proj0(x2d, g, w, cos, sin, tm, seq=None, wa_rows=None):
    n = x2d.shape[0]
    n_pos_blocks = cos.shape[0] // tm
    widths = (QA_W, KA_W, KA_W, QA_W, B_W, B_W, B_W, B_W)
    row = lambda i: (i, 0)
    fixed = lambda i: (0, 0)
    pos = lambda i: (i % n_pos_blocks, 0)
    out_specs = [pl.BlockSpec((tm, wd), row) for wd in widths]
    out_shape = [jax.ShapeDtypeStruct((n, wd), BF16) for wd in widths]
    tiles_per_seq = None
    if seq is not None:
        tiles_per_seq = seq // tm
        for feat, keys in ((KA_W, wa_rows), (KA_W, wa_rows), (B_W, tm), (B_W, tm)):
            out_specs.append(pl.BlockSpec((1, feat, keys), lambda i: (i // tiles_per_seq, 0, 0)))
            out_shape.append(jax.ShapeDtypeStruct((n // seq, feat, keys), F32))
    return pl.pallas_call(
        functools.partial(_proj0_body, tiles_per_seq=tiles_per_seq, wa_rows=wa_rows),
        grid=(n // tm,),
        in_specs=[pl.BlockSpec((tm, D_MODEL), row),
                  pl.BlockSpec((1, D_MODEL), fixed),
                  pl.BlockSpec(w.shape, fixed),
                  pl.BlockSpec((tm, LANES), pos),
                  pl.BlockSpec((tm, LANES), pos)],
        out_specs=out_specs,
        out_shape=out_shape,
        compiler_params=_params(1),
        name="proj0",
    )(x2d, g, w, cos, sin)


def _bias_body(r0_ref, bp_ref, bs_ref, band_ref, *, t_sample, keys_sample):
    x = jnp.broadcast_to(r0_ref[0], (GRP, TOEP_W))
    row = lax.broadcasted_iota(jnp.int32, (GRP, TOEP_W), 0)
    for b in range(GRP.bit_length() - 1):
        x = jnp.where(((row >> b) & 1) == 1, pltpu.roll(x, 1 << b, 1), x)
    x = x[:, :WIN_B]
    r = lax.broadcasted_iota(jnp.int32, (GRP, WIN_B), 0)
    c = lax.broadcasted_iota(jnp.int32, (GRP, WIN_B), 1)
    ci = r // CHUNK
    cj = c // CHUNK
    bp_ref[0] = jnp.where((cj >= ci) & (cj <= ci + N_PREV_B), x, NEG)
    cs = lax.broadcasted_iota(jnp.int32, (t_sample, WIN_B), 1)
    bs_ref[0] = jnp.where(cs < keys_sample, x[:t_sample], NEG)
    ra = lax.broadcasted_iota(jnp.int32, (GRP, WIN_A), 0) // CHUNK
    ca = lax.broadcasted_iota(jnp.int32, (GRP, WIN_A), 1) // CHUNK
    band_ref[...] = jnp.where((ca >= ra) & (ca <= ra + N_PREV_A), 0.0, NEG)


def _build_bias(relpos, t_sample, keys_sample):
    far = relpos[:, 2 * MAX_REL:]
    n_far = PAD_ROWS - MAX_REL + 1
    r0 = jnp.concatenate(
        [jnp.broadcast_to(far, (H_B, n_far)),
         relpos[:, 2 * MAX_REL - 1:0:-1],
         jnp.broadcast_to(far, (H_B, TOEP_W - n_far - (2 * MAX_REL - 1)))], axis=1)
    r0 = r0.reshape(H_B, 1, TOEP_W)
    return pl.pallas_call(
        functools.partial(_bias_body, t_sample=t_sample, keys_sample=keys_sample),
        grid=(H_B,),
        in_specs=[pl.BlockSpec((1, 1, TOEP_W), lambda h: (h, 0, 0))],
        out_specs=[pl.BlockSpec((1, GRP, WIN_B), lambda h: (h, 0, 0)),
                   pl.BlockSpec((1, t_sample, WIN_B), lambda h: (h, 0, 0)),
                   pl.BlockSpec((GRP, WIN_A), lambda h: (0, 0))],
        out_shape=[jax.ShapeDtypeStruct((H_B, GRP, WIN_B), F32),
                   jax.ShapeDtypeStruct((H_B, t_sample, WIN_B), F32),
                   jax.ShapeDtypeStruct((GRP, WIN_A), F32)],
        compiler_params=_params(1),
        name="relbias",
    )(r0)


def _nt_dot(a, b):
    return lax.dot_general(a, b, (((1,), (1,)), ((), ())), preferred_element_type=F32)


def _attend_group(qa, qb, ka, va, kb_ref, vb_ref, win_b, sink_ref, bias_ref, mask_a, col_ok_a,
                  col_ok_b):
    n = qa.shape[0]
    lane = lax.broadcasted_iota(jnp.int32, (n, LANES), 1)
    lo = lane < HEAD_DIM
    zero = jnp.zeros((n, LANES), BF16)

    rows = []
    for p in range(QA_W // LANES):
        tile = qa[:, p * LANES:(p + 1) * LANES]
        rows.append(jnp.where(lo, tile, zero))
        rows.append(jnp.where(lo, zero, tile))
    s_all = _nt_dot(jnp.concatenate(rows, axis=0), ka)
    probs, inv = [], []
    for h in range(H_A):
        s = s_all[h * n:(h + 1) * n]
        if mask_a is not None:
            s = s + mask_a
        if col_ok_a is not None:
            s = jnp.where(col_ok_a, s, NEG)
        sink = sink_ref[h]
        m = jnp.maximum(jnp.max(s, axis=-1, keepdims=True), sink)
        e = jnp.exp(s - m)
        den = jnp.sum(e, axis=-1, keepdims=True) + jnp.exp(sink - m)
        probs.append(e.astype(BF16))
        inv.append(1.0 / den)
    r_all = jnp.dot(jnp.concatenate(probs, axis=0), va, preferred_element_type=F32)
    o_a = []
    for p in range(QA_W // LANES):
        r0 = r_all[(2 * p) * n:(2 * p + 1) * n] * inv[2 * p]
        r1 = r_all[(2 * p + 1) * n:(2 * p + 2) * n] * inv[2 * p + 1]
        o_a.append(jnp.where(lo, r0, r1))
    o_a = jnp.concatenate(o_a, axis=1)

    o_b = []
    for p in range(B_W // LANES):
        sl = slice(p * LANES, (p + 1) * LANES)
        tile = qb[:, sl]
        lhs = jnp.concatenate([jnp.where(lo, tile, zero), jnp.where(lo, zero, tile)], axis=0)
        s2 = _nt_dot(lhs, kb_ref[win_b, sl])
        halves = []
        for j in range(2):
            s = s2[j * n:(j + 1) * n] + bias_ref[2 * p + j]
            if col_ok_b is not None:
                s = jnp.where(col_ok_b, s, NEG)
            m = jnp.max(s, axis=-1, keepdims=True)
            e = jnp.exp(s - m)
            den = jnp.sum(e, axis=-1, keepdims=True)
            r = jnp.dot(e.astype(BF16), vb_ref[win_b, sl], preferred_element_type=F32)
            halves.append(r * (1.0 / den))
        o_b.append(jnp.where(lo, halves[0], halves[1]))
    o_b = jnp.concatenate(o_b, axis=1)
    return o_a, o_b


def _mix_out(o_a, o_b, ga, gb, x, w_ref, g_ref):
    mix = jnp.concatenate([o_a * ga.astype(F32), o_b * gb.astype(F32)], axis=1).astype(BF16)
    m = jnp.dot(mix, w_ref[...], preferred_element_type=F32)
    return x + _rms(m, g_ref[...])


def _attn_prompt_body(sink_ref, x_ref, qa_ref, ga_ref, qb_ref, gb_ref, ka_ref, va_ref, kb_ref,
                      vb_ref, bias_ref, band_ref, w_ref, g_ref, y_ref,
                      kap, vap, kbp, vbp, *, seq):
    t = pl.program_id(1)
    tq = x_ref.shape[1]

    @pl.when(t == 0)
    def _():
        for dst, src in ((kap, ka_ref), (vap, va_ref), (kbp, kb_ref), (vbp, vb_ref)):
            dst[:PAD_ROWS, :] = jnp.zeros((PAD_ROWS, dst.shape[1]), BF16)
            dst[PAD_ROWS:, :] = src[0]

    band = band_ref[...]
    col_a = lax.broadcasted_iota(jnp.int32, (1, WIN_A), 1)
    col_b = lax.broadcasted_iota(jnp.int32, (1, WIN_B), 1)
    for sub in range(tq // GRP):
        start = pl.multiple_of(t * tq + sub * GRP, GRP)
        rs = slice(sub * GRP, (sub + 1) * GRP)
        win_a = pl.ds(pl.multiple_of(start + (PAD_ROWS - N_PREV_A * CHUNK), GRP), WIN_A)
        win_b = pl.ds(start, WIN_B)
        ok_a = col_a >= N_PREV_A * CHUNK - start
        ok_b = col_b >= N_PREV_B * CHUNK - start
        o_a, o_b = _attend_group(qa_ref[0, rs, :], qb_ref[0, rs, :], kap[win_a, :], vap[win_a, :],
                                 kbp, vbp, win_b, sink_ref, bias_ref, band, ok_a, ok_b)
        y_ref[0, rs, :] = _mix_out(o_a, o_b, ga_ref[0, rs, :], gb_ref[0, rs, :], x_ref[0, rs, :],
                                   w_ref, g_ref)


def _attn_prompt(sinks, x, qa, ga, qb, gb, ka, va, kb, vb, bias, band, w_out, g_post, tq):
    b, seq, _ = x.shape
    tile = lambda w: pl.BlockSpec((1, tq, w), lambda i, j: (i, j, 0))
    whole = lambda w: pl.BlockSpec((1, seq, w), lambda i, j: (i, 0, 0))
    fixed = lambda shape: pl.BlockSpec(shape, lambda i, j: (0,) * len(shape))
    return pl.pallas_call(
        functools.partial(_attn_prompt_body, seq=seq),
        grid=(b, seq // tq),
        in_specs=[pl.BlockSpec(memory_space=pltpu.SMEM),
                  tile(D_MODEL), tile(QA_W), tile(QA_W), tile(B_W), tile(B_W),
                  whole(KA_W), whole(KA_W), whole(B_W), whole(B_W),
                  fixed(bias.shape), fixed(band.shape), fixed(w_out.shape), fixed((1, D_MODEL))],
        out_specs=tile(D_MODEL),
        out_shape=jax.ShapeDtypeStruct(x.shape, F32),
        scratch_shapes=[pltpu.VMEM((PAD_ROWS + seq, KA_W), BF16),
                        pltpu.VMEM((PAD_ROWS + seq, KA_W), BF16),
                        pltpu.VMEM((PAD_ROWS + seq, B_W), BF16),
                        pltpu.VMEM((PAD_ROWS + seq, B_W), BF16)],
        compiler_params=_params(2),
        name="attn_prompt",
    )(sinks, x, qa, ga, qb, gb, ka, va, kb, vb, bias, band, w_out, g_post)


SAMPLE_REQS = 4


def _softmax2(s_c, s_n, sink):
    m = jnp.maximum(jnp.max(s_c, axis=-1, keepdims=True), jnp.max(s_n, axis=-1, keepdims=True))
    if sink is not None:
        m = jnp.maximum(m, sink)
    e_c = jnp.exp(s_c - m)
    e_n = jnp.exp(s_n - m)
    den = jnp.sum(e_c, axis=-1, keepdims=True) + jnp.sum(e_n, axis=-1, keepdims=True)
    if sink is not None:
        den = den + jnp.exp(sink - m)
    return e_c.astype(BF16), e_n.astype(BF16), 1.0 / den


def _roll_in(cache, new):
    t, w = new.shape[0], cache.shape[1]
    lane = lax.broadcasted_iota(jnp.int32, (cache.shape[0], LANES), 1)
    padded = jnp.concatenate([jnp.zeros((LANES - t, new.shape[1]), F32), new.astype(F32)], axis=0)
    rolled = pltpu.roll(cache, w - t, 1)
    last = jnp.where(lane >= LANES - t, padded.T, rolled[:, w - LANES:])
    return last if w == LANES else jnp.concatenate([rolled[:, :w - LANES], last], axis=1)


def _attn_sample_body(sink_ref, x_ref, qa_ref, ga_ref, qb_ref, gb_ref, ka_ref, va_ref, kb_ref,
                      vb_ref, cak_ref, cav_ref, cbk_ref, cbv_ref, bias_ref, w_ref, g_ref,
                      y_ref, nak_ref, nav_ref, nbk_ref, nbv_ref):
    nreq, t, _ = x_ref.shape
    for r in range(nreq):
        for dst, cache, new in ((nak_ref, cak_ref, ka_ref), (nav_ref, cav_ref, va_ref),
                                (nbk_ref, cbk_ref, kb_ref), (nbv_ref, cbv_ref, vb_ref)):
            dst[r] = _roll_in(cache[r], new[r])

    wb = cbk_ref.shape[2]
    lane = lax.broadcasted_iota(jnp.int32, (t, LANES), 1)
    lo = lane < HEAD_DIM
    zero = jnp.zeros((t, LANES), BF16)
    halves = lambda tile: (jnp.where(lo, tile, zero), jnp.where(lo, zero, tile))

    mix = []
    for r in range(nreq):
        qa = qa_ref[r]
        lhs = jnp.concatenate([h for p in range(QA_W // LANES)
                               for h in halves(qa[:, p * LANES:(p + 1) * LANES])], axis=0)
        s_c = jnp.dot(lhs, cak_ref[r].astype(BF16), preferred_element_type=F32)
        s_n = _nt_dot(lhs, ka_ref[r])
        cav = cav_ref[r].astype(BF16)
        va = va_ref[r]
        o_a = []
        for p in range(QA_W // LANES):
            pair = []
            for j in range(2):
                h = 2 * p + j
                rs = slice(h * t, (h + 1) * t)
                e_c, e_n, inv = _softmax2(s_c[rs], s_n[rs], sink_ref[h])
                o = _nt_dot(e_c, cav) + jnp.dot(e_n, va, preferred_element_type=F32)
                pair.append(o * inv)
            o_a.append(jnp.where(lo, pair[0], pair[1]))
        qb = qb_ref[r]
        o_b = []
        for p in range(B_W // LANES):
            sl = slice(p * LANES, (p + 1) * LANES)
            lhs = jnp.concatenate(halves(qb[:, sl]), axis=0)
            s_c = jnp.dot(lhs, cbk_ref[r, sl, :].astype(BF16), preferred_element_type=F32)
            s_n = _nt_dot(lhs, kb_ref[r, :, sl])
            cbv = cbv_ref[r, sl, :].astype(BF16)
            vb = vb_ref[r, :, sl]
            pair = []
            for j in range(2):
                rs = slice(j * t, (j + 1) * t)
                bias = bias_ref[2 * p + j]
                e_c, e_n, inv = _softmax2(s_c[rs] + bias[:, :wb], s_n[rs] + bias[:, wb:wb + t], None)
                o = _nt_dot(e_c, cbv) + jnp.dot(e_n, vb, preferred_element_type=F32)
                pair.append(o * inv)
            o_b.append(jnp.where(lo, pair[0], pair[1]))
        mix.append(jnp.concatenate(
            [jnp.concatenate(o_a, axis=1) * ga_ref[r].astype(F32),
             jnp.concatenate(o_b, axis=1) * gb_ref[r].astype(F32)], axis=1).astype(BF16))
    m = jnp.dot(jnp.concatenate(mix, axis=0), w_ref[...], preferred_element_type=F32)
    y = x_ref[...].reshape(nreq * t, D_MODEL) + _rms(m, g_ref[...])
    y_ref[...] = y.reshape(nreq, t, D_MODEL)


def _attn_sample(sinks, x, qa, ga, qb, gb, ka, va, kb, vb, cak, cav, cbk, cbv, bias, w_out, g_post):
    b = x.shape[0]
    per = lambda a: pl.BlockSpec((SAMPLE_REQS,) + a.shape[1:], lambda i: (i, 0, 0))
    fixed = lambda shape: pl.BlockSpec(shape, lambda i: (0,) * len(shape))
    arrs = (x, qa, ga, qb, gb, ka, va, kb, vb, cak, cav, cbk, cbv)
    return pl.pallas_call(
        _attn_sample_body,
        grid=(b // SAMPLE_REQS,),
        in_specs=[pl.BlockSpec(memory_space=pltpu.SMEM)] + [per(a) for a in arrs]
                 + [fixed(bias.shape), fixed(w_out.shape), fixed((1, D_MODEL))],
        out_specs=[per(a) for a in (x, cak, cav, cbk, cbv)],
        out_shape=[jax.ShapeDtypeStruct(a.shape, F32) for a in (x, cak, cav, cbk, cbv)],
        compiler_params=_params(1),
        name="attn_sample",
    )(sinks, *arrs, bias, w_out, g_post)


LRU_SEG = 8
LRU_STEPS = 32
LRU_TILE = LRU_SEG * LRU_STEPS
N_TAIL = CONV_W - 1
LRU_TILES_PER_STEP = 2
LRU_CB = D_LRU
N_CB = D_LRU // LRU_CB
LRU_LOOKAHEAD = 2


def _lru_block(xb, z, tail, carry_in, cb, chained, w):
    cw_ref, cb_ref, wa_ref, ba_ref, wx_ref, bx_ref, lam_ref = w[4:11]
    cs = slice(cb * LRU_CB, (cb + 1) * LRU_CB)
    grp = lambda v, g, n=1: v[g * LRU_SEG:(g + n) * LRU_SEG]

    if chained:
        sub = lax.broadcasted_iota(jnp.int32, (LRU_SEG, LRU_CB), 0)
        before = [jnp.where(sub == 0, pltpu.roll(grp(tail, j), 1, 0),
                            pltpu.roll(grp(xb, LRU_STEPS - N_TAIL + j), 1, 0))
                  for j in range(N_TAIL)]
    else:
        before = tail
    xb_tail = grp(xb, LRU_STEPS - N_TAIL, N_TAIL)

    cw = cw_ref[:, cs]
    xc = cb_ref[:, cs] + xb * cw[CONV_W - 1:CONV_W]
    for k in range(1, CONV_W):
        shifted = jnp.concatenate(before[N_TAIL - k:] + [grp(xb, 0, LRU_STEPS - k)], axis=0)
        xc = xc + shifted * cw[CONV_W - 1 - k:CONV_W - k]

    xcb = xc.astype(BF16)
    per = LRU_CB // MXU_DIM

    def gate(w_ref):
        return jnp.concatenate(
            [jnp.dot(xcb[:, j * MXU_DIM:(j + 1) * MXU_DIM], w_ref[cb * per + j],
                     preferred_element_type=F32) for j in range(per)], axis=1)

    r = jax.nn.sigmoid(gate(wa_ref) + ba_ref[:, cs])
    gi = jax.nn.sigmoid(gate(wx_ref) + bx_ref[:, cs])
    lam = lam_ref[:, cs]
    log_sig = jnp.minimum(lam, 0.0) - jnp.log1p(jnp.exp(-jnp.abs(lam)))
    log_a = r * (C_GATE * log_sig)
    a = jnp.exp(log_a)
    u = jnp.sqrt(jnp.tanh(-log_a) * (a * a + 1.0)) * (gi * xc)

    hc, ac = grp(u, 0), grp(a, 0)
    h_loc, a_loc = [hc], [ac]
    for g in range(1, LRU_STEPS):
        ag = grp(a, g)
        hc = ag * hc + grp(u, g)
        ac = ag * ac
        h_loc.append(hc)
        a_loc.append(ac)

    if chained:
        c = carry_in
        rows = []
        for s in range(LRU_SEG):
            rows.append(c)
            c = ac[s:s + 1] * c + hc[s:s + 1]
        carry = jnp.concatenate(rows, axis=0)
    else:
        c = None
        carry = carry_in
    h_groups = [hl + al * carry for hl, al in zip(h_loc, a_loc)]
    yl = (jnp.concatenate(h_groups, axis=0) * _silu(z)).astype(BF16)
    return yl, c, xb_tail, h_groups[-1]


def _lru_run(load_x, store_y, n_tiles, tails, carries, chained, w):
    p_ref, pt_ref, gpre_ref, win_ref = w[:4]
    wout_ref, gpost_ref = w[11:13]
    hp = {}

    def in_proj(unit):
        ti, cb = divmod(unit, N_CB)
        if ti not in hp:
            h = _rms(load_x(ti), gpre_ref[...]).astype(BF16)
            hp[ti] = jnp.dot(p_ref[...], h, preferred_element_type=F32).astype(BF16)
        lo = cb * LRU_CB
        xb = jnp.dot(hp[ti], win_ref[:, lo:lo + LRU_CB], preferred_element_type=F32)
        z = jnp.dot(hp[ti], win_ref[:, D_LRU + lo:D_LRU + lo + LRU_CB], preferred_element_type=F32)
        return xb, z

    units = n_tiles * N_CB
    ready = {u: in_proj(u) for u in range(min(LRU_LOOKAHEAD, units))}
    h_last = [None] * N_CB
    m = None
    for unit in range(units):
        if unit + LRU_LOOKAHEAD < units:
            ready[unit + LRU_LOOKAHEAD] = in_proj(unit + LRU_LOOKAHEAD)
        ti, cb = divmod(unit, N_CB)
        xb, z = ready.pop(unit)
        yl, carries[cb], tails[cb], h_last[cb] = _lru_block(xb, z, tails[cb], carries[cb], cb,
                                                            chained, w)
        yl = jnp.dot(pt_ref[...], yl, preferred_element_type=F32).astype(BF16)
        part = jnp.dot(yl, wout_ref[cb * LRU_CB:(cb + 1) * LRU_CB, :], preferred_element_type=F32)
        m = part if cb == 0 else m + part
        if cb == N_CB - 1:
            store_y(ti, load_x(ti) + _rms(m, gpost_ref[...]))
    return h_last


def _lru_body(x_ref, h0_ref, c0_ref, *rest, chained):
    w, (y_ref, hl_ref, cl_ref, h_s, tail_s) = rest[:-5], rest[-5:]
    t = pl.program_id(1)
    n_tiles = x_ref.shape[1] // LRU_TILE
    blocks = [slice(cb * LRU_CB, (cb + 1) * LRU_CB) for cb in range(N_CB)]
    rows = lambda ti: slice(ti * LRU_TILE, (ti + 1) * LRU_TILE)
    load_x = lambda ti: x_ref[0, rows(ti), :]

    def store_y(ti, y):
        y_ref[0, rows(ti), :] = y

    if not chained:
        tails = [[c0_ref[j, :, cs] for j in range(N_TAIL)] for cs in blocks]
        carries = [h0_ref[0, :, cs] for cs in blocks]
        h_last = _lru_run(load_x, store_y, n_tiles, tails, carries, False, w)
        hl_ref[0] = jnp.concatenate(h_last, axis=1)
        cl_ref[...] = jnp.concatenate(tails, axis=1).reshape(N_TAIL, LRU_SEG, D_LRU)
        return

    @pl.when(t == 0)
    def _():
        h_s[...] = h0_ref[0]
        tail_s[...] = jnp.zeros(tail_s.shape, F32)
        for j in range(N_TAIL):
            tail_s[j * LRU_SEG + LRU_SEG - 1:(j + 1) * LRU_SEG, :] = c0_ref[0, j:j + 1, :]

    tails = [tail_s[:, cs] for cs in blocks]
    carries = [h_s[:, cs] for cs in blocks]
    _lru_run(load_x, store_y, n_tiles, tails, carries, True, w)
    c = jnp.concatenate(carries, axis=1)
    tail = jnp.concatenate(tails, axis=1)
    h_s[...] = c
    tail_s[...] = tail

    @pl.when(t == pl.num_programs(1) - 1)
    def _():
        hl_ref[0] = c
        last = LRU_SEG - 1
        cl_ref[0] = jnp.concatenate(
            [tail[j * LRU_SEG + last:(j + 1) * LRU_SEG] for j in range(N_TAIL)], axis=0)


def _lru(x, h0, c0, weights, chained):
    nb, seq, _ = x.shape
    fixed = lambda a: pl.BlockSpec(a.shape, lambda i, j: (0,) * a.ndim)
    perm = np.zeros((LRU_TILE, LRU_TILE), np.float32)
    rho = np.arange(LRU_TILE)
    perm[rho, (rho % LRU_SEG) * LRU_STEPS + rho // LRU_SEG] = 1.0
    perm_mats = (jnp.asarray(perm, BF16), jnp.asarray(perm.T, BF16))
    if chained:
        state_specs = [pl.BlockSpec((1, 1, D_LRU), lambda i, j: (i, 0, 0)),
                       pl.BlockSpec((1, N_TAIL, D_LRU), lambda i, j: (i, 0, 0))]
        state_shapes = [jax.ShapeDtypeStruct((nb, 1, D_LRU), F32),
                        jax.ShapeDtypeStruct((nb, N_TAIL, D_LRU), F32)]
    else:
        assert seq == LRU_TILE
        state_specs = [pl.BlockSpec((1, LRU_SEG, D_LRU), lambda i, j: (i, 0, 0)),
                       pl.BlockSpec((N_TAIL, LRU_SEG, D_LRU), lambda i, j: (0, i, 0))]
        state_shapes = [jax.ShapeDtypeStruct((nb, LRU_SEG, D_LRU), F32),
                        jax.ShapeDtypeStruct((N_TAIL, nb * LRU_SEG, D_LRU), F32)]
    rows = LRU_TILE * (LRU_TILES_PER_STEP if chained else 1)
    tile = pl.BlockSpec((1, rows, D_MODEL), lambda i, j: (i, j, 0))
    return pl.pallas_call(
        functools.partial(_lru_body, chained=chained),
        grid=(nb, seq // rows),
        in_specs=[tile] + state_specs + [fixed(a) for a in perm_mats + weights],
        out_specs=[tile] + state_specs,
        out_shape=[jax.ShapeDtypeStruct(x.shape, F32)] + state_shapes,
        scratch_shapes=[pltpu.VMEM((1, D_LRU), F32), pltpu.VMEM((N_TAIL * LRU_SEG, D_LRU), F32)],
        compiler_params=_params(2),
        name="lru_chained" if chained else "lru_batched",
    )(x, h0, c0, *perm_mats, *weights)


def _rope_tables(pos):
    half = HEAD_DIM // 2
    inv = ROPE_THETA ** (-jnp.arange(half, dtype=F32) / half)
    ang = pos.astype(F32)[:, None] * inv[None, :]
    cos = jnp.tile(jnp.cos(ang), (1, LANES // half))
    sin = jnp.sin(ang)
    sin = jnp.tile(jnp.concatenate([-sin, sin], axis=1), (1, LANES // HEAD_DIM))
    return cos, sin


def _interleave_groups(w, axis, unit=HEAD_DIM):
    shape = w.shape
    per = H_A // KV_A
    w = w.reshape(shape[:axis] + (KV_A, per, unit) + shape[axis + 1:])
    w = jnp.swapaxes(w, axis, axis + 1)
    return w.reshape(shape)


def _block_diag(w):
    per = MXU_DIM // BLOCK
    w = w.reshape(N_BLOCKS // per, per, BLOCK, BLOCK)
    eye = jnp.eye(per, dtype=w.dtype)
    w = w[:, :, :, None, :] * eye[None, :, None, :, None]
    return w.reshape(N_BLOCKS // per, MXU_DIM, MXU_DIM)


def kernel(x_prompt, x_sample, cache_a_k, cache_a_v, cache_b_k, cache_b_v, state_c_h, state_c_conv,
           ln_pre, ln_post, w_in_ab, sinks_a, relpos_b, w_out_ab, w_in_c, conv_c_w, conv_c_b,
           gate_c_wa, gate_c_ba, gate_c_wx, gate_c_bx, lambda_c, w_out_c):
    bp, s_len, _ = x_prompt.shape
    bs, t_s, _ = x_sample.shape
    wa_rows = cache_a_k.shape[2]
    wb_rows = cache_b_k.shape[2]
    assert wa_rows + t_s <= WIN_A and wb_rows + t_s <= WIN_B and wb_rows == PAD_ROWS

    w_in = w_in_ab[0]
    w_in = jnp.concatenate([_interleave_groups(w_in[:, :QA_W], 1), w_in[:, QA_W:QA_W + 2 * KA_W],
                            _interleave_groups(w_in[:, QA_W + 2 * KA_W:2 * QA_W + 2 * KA_W], 1),
                            w_in[:, 2 * QA_W + 2 * KA_W:]], axis=1).astype(BF16)
    w_out = jnp.concatenate([_interleave_groups(w_out_ab[0, :QA_W], 0), w_out_ab[0, QA_W:]],
                            axis=0).astype(BF16)
    sinks = _interleave_groups(sinks_a[0], 0, unit=1)
    g_pre0 = ln_pre[0].reshape(1, D_MODEL)
    g_post0 = ln_post[0].reshape(1, D_MODEL)
    bias_p, bias_s, band = _build_bias(relpos_b[0], t_s, wb_rows + t_s)

    cos_p, sin_p = _rope_tables(jnp.arange(s_len, dtype=jnp.int32))
    assert wb_rows == PROJ_TM and wa_rows <= PROJ_TM and s_len % PROJ_TM == 0
    proj = _proj0(x_prompt.reshape(bp * s_len, D_MODEL), g_pre0, w_in, cos_p, sin_p, PROJ_TM,
                  seq=s_len, wa_rows=wa_rows)
    qa, ka, va, ga, qb, kb, vb, gb = [a.reshape(bp, s_len, a.shape[-1]) for a in proj[:8]]
    y0_p = _attn_prompt(sinks, x_prompt, qa, ga, qb, gb, ka, va, kb, vb, bias_p, band, w_out,
                        g_post0, ATT_TQ)
    tmaj = lambda c, heads: jnp.transpose(
        c.reshape(c.shape[0], heads, HEAD_DIM, c.shape[2]), (0, 3, 1, 2))[None]
    nak_p, nav_p = tmaj(proj[8], KV_A), tmaj(proj[9], KV_A)
    nbk_p, nbv_p = tmaj(proj[10], H_B), tmaj(proj[11], H_B)

    n_s = bs * t_s
    tm_s = min(PROJ_TM, n_s)
    cos_s, sin_s = _rope_tables(PAST_LEN + jnp.arange(t_s, dtype=jnp.int32))
    cos_s = jnp.tile(cos_s, (tm_s // t_s, 1))
    sin_s = jnp.tile(sin_s, (tm_s // t_s, 1))
    proj = _proj0(x_sample.reshape(n_s, D_MODEL), g_pre0, w_in, cos_s, sin_s, tm_s)
    qa, ka, va, ga, qb, kb, vb, gb = [a.reshape(bs, t_s, a.shape[-1]) for a in proj]
    fmaj = lambda c: jnp.transpose(c[0], (0, 2, 3, 1))
    cak, cav, cbk, cbv = fmaj(cache_a_k), fmaj(cache_a_v), fmaj(cache_b_k), fmaj(cache_b_v)
    flat = lambda c: c.reshape(bs, c.shape[1] * HEAD_DIM, c.shape[3])
    y0_s, nak_s, nav_s, nbk_s, nbv_s = _attn_sample(
        sinks, x_sample, qa, ga, qb, gb, ka, va, kb, vb, flat(cak), flat(cav), flat(cbk), flat(cbv),
        bias_s, w_out, g_post0)
    tmaj = lambda c, heads: jnp.transpose(
        c.reshape(c.shape[0], heads, HEAD_DIM, c.shape[2]), (0, 3, 1, 2))[None]
    nak_s, nav_s = tmaj(nak_s, KV_A), tmaj(nav_s, KV_A)
    nbk_s, nbv_s = tmaj(nbk_s, H_B), tmaj(nbv_s, H_B)

    row = lambda v: v.reshape(1, -1)
    lru_w = (row(ln_pre[1]), w_in_c[0].astype(BF16), conv_c_w[0], row(conv_c_b[0]),
             _block_diag(gate_c_wa[0]).astype(BF16), row(gate_c_ba[0]),
             _block_diag(gate_c_wx[0]).astype(BF16), row(gate_c_bx[0]),
             row(lambda_c[0]), w_out_c[0].astype(BF16), row(ln_post[1]))
    y1_p, hl_p, cl_p = _lru(y0_p, jnp.zeros((bp, 1, D_LRU), F32), jnp.zeros((bp, N_TAIL, D_LRU), F32),
                            lru_w, chained=True)
    assert t_s == LRU_STEPS and bs % LRU_SEG == 0
    y1_s, hl_s, cl_s = _lru(y0_s.reshape(bs // LRU_SEG, LRU_TILE, D_MODEL),
                            state_c_h[0].reshape(bs // LRU_SEG, LRU_SEG, D_LRU),
                            jnp.transpose(state_c_conv[0], (1, 0, 2)), lru_w, chained=False)
    y1_s = y1_s.reshape(bs, t_s, D_MODEL)
    cl_s = jnp.transpose(cl_s, (1, 0, 2))

    return (y1_p, y1_s, nak_p, nav_p, nbk_p, nbv_p,
            hl_p.reshape(1, bp, D_LRU), cl_p.reshape(1, bp, N_TAIL, D_LRU),
            nak_s, nav_s, nbk_s, nbv_s,
            hl_s.reshape(1, bs, D_LRU), cl_s.reshape(1, bs, N_TAIL, D_LRU))
```

```python
import functools

import jax
import jax.numpy as jnp
import numpy as np
from jax import lax
from jax.experimental import pallas as pl
from jax.experimental.pallas import tpu as pltpu

F32 = jnp.float32
BF16 = jnp.bfloat16

D_MODEL = 1024
CHUNK = 64
HEAD_DIM = 64
H_A = 8
KV_A = 2
N_PREV_A = 2
H_B = 8
N_PREV_B = 8
MAX_REL = 128
ROPE_THETA = 10000.0
D_LRU = D_MODEL
N_BLOCKS = 16
BLOCK = D_LRU // N_BLOCKS
CONV_W = 4
C_GATE = 8.0
EPS = 1e-6
PAST_LEN = 1024
NEG = -1e30
LOG2E = 1.4426950408889634
Q_SCALE = HEAD_DIM ** -0.5 * LOG2E

LANES = 128
MXU_DIM = 256
VMEM_LIMIT = 56 * 1024 * 1024

QA_W = H_A * HEAD_DIM
KA_W = KV_A * HEAD_DIM
B_W = H_B * HEAD_DIM
MIX_W = QA_W + B_W
GRP = 2 * CHUNK
WIN_A = (N_PREV_A + 2) * CHUNK
WIN_B = (N_PREV_B + 2) * CHUNK
PAD_ROWS = N_PREV_B * CHUNK
TOEP_W = WIN_B + LANES

PROJ_TM = 512
ATT_TQ = 256


def _params(n_axes):
    return pltpu.CompilerParams(
        dimension_semantics=("arbitrary",) * n_axes,
        vmem_limit_bytes=VMEM_LIMIT)


def _rms(x, g):
    ms = jnp.mean(x * x, axis=-1, keepdims=True)
    return x * lax.rsqrt(ms + EPS) * g


def _silu(x):
    return x * jax.nn.sigmoid(x)


def _proj0_body(x_ref, g_ref, w_ref, cos_ref, sin_ref,
                qa_ref, ka_ref, va_ref, ga_ref, qb_ref, kb_ref, vb_ref, gb_ref, *cache_refs,
                tiles_per_seq, wa_rows):
    h = _rms(x_ref[...], g_ref[...]).astype(BF16)
    cos = cos_ref[...]
    sin = sin_ref[...]
    tm = cos.shape[0]
    lane = lax.broadcasted_iota(jnp.int32, (tm, LANES), 1)
    lower = (lane & (HEAD_DIM - 1)) < (HEAD_DIM // 2)

    def mm(lo, hi):
        return jnp.dot(h, w_ref[:, lo:hi], preferred_element_type=F32)

    def rope(t):
        partner = jnp.where(lower, pltpu.roll(t, LANES - HEAD_DIM // 2, 1),
                            pltpu.roll(t, HEAD_DIM // 2, 1))
        return t * cos + partner * sin

    o = 0
    t = mm(o, o + QA_W)
    for k in range(QA_W // LANES):
        sl = slice(k * LANES, (k + 1) * LANES)
        qa_ref[:, sl] = (rope(t[:, sl]) * Q_SCALE).astype(BF16)
    o += QA_W
    ka = rope(mm(o, o + KA_W))
    ka_ref[...] = ka.astype(BF16)
    o += KA_W
    va = mm(o, o + KA_W)
    va_ref[...] = va.astype(BF16)
    o += KA_W
    ga_ref[...] = _silu(mm(o, o + QA_W)).astype(BF16)
    o += QA_W
    qb_ref[...] = (mm(o, o + B_W) * Q_SCALE).astype(BF16)
    o += B_W
    kb = mm(o, o + B_W)
    kb_ref[...] = kb.astype(BF16)
    o += B_W
    vb = mm(o, o + B_W)
    vb_ref[...] = vb.astype(BF16)
    o += B_W
    gb_ref[...] = _silu(mm(o, o + B_W)).astype(BF16)

    if cache_refs:
        @pl.when(pl.program_id(0) % tiles_per_seq == tiles_per_seq - 1)
        def _():
            kat_ref, vat_ref, kbt_ref, vbt_ref = cache_refs
            kat_ref[0] = ka[tm - wa_rows:].T
            vat_ref[0] = va[tm - wa_rows:].T
            kbt_ref[0] = kb.T
            vbt_ref[0] = vb.T


def _proj0(x2d, g, w, cos, sin, tm, seq=None, wa_rows=None):
    n = x2d.shape[0]
    n_pos_blocks = cos.shape[0] // tm
    widths = (QA_W, KA_W, KA_W, QA_W, B_W, B_W, B_W, B_W)
    row = lambda i: (i, 0)
    fixed = lambda i: (0, 0)
    pos = lambda i: (i % n_pos_blocks, 0)
    out_specs = [pl.BlockSpec((tm, wd), row) for wd in widths]
    out_shape = [jax.ShapeDtypeStruct((n, wd), BF16) for wd in widths]
    tiles_per_seq = None
    if seq is not None:
        tiles_per_seq = seq // tm
        for feat, keys in ((KA_W, wa_rows), (KA_W, wa_rows), (B_W, tm), (B_W, tm)):
            out_specs.append(pl.BlockSpec((1, feat, keys), lambda i: (i // tiles_per_seq, 0, 0)))
            out_shape.append(jax.ShapeDtypeStruct((n // seq, feat, keys), F32))
    return pl.pallas_call(
        functools.partial(_proj0_body, tiles_per_seq=tiles_per_seq, wa_rows=wa_rows),
        grid=(n // tm,),
        in_specs=[pl.BlockSpec((tm, D_MODEL), row),
                  pl.BlockSpec((1, D_MODEL), fixed),
                  pl.BlockSpec(w.shape, fixed),
                  pl.BlockSpec((tm, LANES), pos),
                  pl.BlockSpec((tm, LANES), pos)],
        out_specs=out_specs,
        out_shape=out_shape,
        compiler_params=_params(1),
        name="proj0",
    )(x2d, g, w, cos, sin)


def _bias_body(r0_ref, bp_ref, bs_ref, band_ref, *, t_sample, keys_sample):
    x = jnp.broadcast_to(r0_ref[0], (GRP, TOEP_W)) * LOG2E
    row = lax.broadcasted_iota(jnp.int32, (GRP, TOEP_W), 0)
    for b in range(GRP.bit_length() - 1):
        x = jnp.where(((row >> b) & 1) == 1, pltpu.roll(x, 1 << b, 1), x)
    x = x[:, :WIN_B]
    r = lax.broadcasted_iota(jnp.int32, (GRP, WIN_B), 0)
    c = lax.broadcasted_iota(jnp.int32, (GRP, WIN_B), 1)
    ci = r // CHUNK
    cj = c // CHUNK
    bp_ref[0] = jnp.where((cj >= ci) & (cj <= ci + N_PREV_B), x, NEG)
    cs = lax.broadcasted_iota(jnp.int32, (t_sample, WIN_B), 1)
    bs_ref[0] = jnp.where(cs < keys_sample, x[:t_sample], NEG)
    ra = lax.broadcasted_iota(jnp.int32, (GRP, WIN_A), 0) // CHUNK
    ca = lax.broadcasted_iota(jnp.int32, (GRP, WIN_A), 1) // CHUNK
    band_ref[...] = jnp.where((ca >= ra) & (ca <= ra + N_PREV_A), 0.0, NEG)


def _build_bias(relpos, t_sample, keys_sample):
    far = relpos[:, 2 * MAX_REL:]
    n_far = PAD_ROWS - MAX_REL + 1
    r0 = jnp.concatenate(
        [jnp.broadcast_to(far, (H_B, n_far)),
         relpos[:, 2 * MAX_REL - 1:0:-1],
         jnp.broadcast_to(far, (H_B, TOEP_W - n_far - (2 * MAX_REL - 1)))], axis=1)
    r0 = r0.reshape(H_B, 1, TOEP_W)
    return pl.pallas_call(
        functools.partial(_bias_body, t_sample=t_sample, keys_sample=keys_sample),
        grid=(H_B,),
        in_specs=[pl.BlockSpec((1, 1, TOEP_W), lambda h: (h, 0, 0))],
        out_specs=[pl.BlockSpec((1, GRP, WIN_B), lambda h: (h, 0, 0)),
                   pl.BlockSpec((1, t_sample, WIN_B), lambda h: (h, 0, 0)),
                   pl.BlockSpec((GRP, WIN_A), lambda h: (0, 0))],
        out_shape=[jax.ShapeDtypeStruct((H_B, GRP, WIN_B), F32),
                   jax.ShapeDtypeStruct((H_B, t_sample, WIN_B), F32),
                   jax.ShapeDtypeStruct((GRP, WIN_A), F32)],
        compiler_params=_params(1),
        name="relbias",
    )(r0)


def _nt_dot(a, b):
    return lax.dot_general(a, b, (((1,), (1,)), ((), ())), preferred_element_type=F32)


def _attend_group(qa, qb, ka, va, kb_ref, vb_ref, win_b, sink_ref, bias_ref, mask_a, col_ok_a,
                  col_ok_b):
    n = qa.shape[0]
    lane = lax.broadcasted_iota(jnp.int32, (n, LANES), 1)
    lo = lane < HEAD_DIM
    zero = jnp.zeros((n, LANES), BF16)
    halves = lambda tile: [jnp.where(lo, tile, zero), jnp.where(lo, zero, tile)]
    n_pairs = B_W // LANES

    def scores(unit):
        if unit == 0:
            lhs = jnp.concatenate([h for p in range(QA_W // LANES)
                                   for h in halves(qa[:, p * LANES:(p + 1) * LANES])], axis=0)
            return _nt_dot(lhs, ka)
        sl = slice((unit - 1) * LANES, unit * LANES)
        return _nt_dot(jnp.concatenate(halves(qb[:, sl]), axis=0), kb_ref[win_b, sl])

    def finish_a(s_all):
        probs, inv = [], []
        for h in range(H_A):
            s = s_all[h * n:(h + 1) * n]
            if mask_a is not None:
                s = s + mask_a
            if col_ok_a is not None:
                s = jnp.where(col_ok_a, s, NEG)
            sink = sink_ref[h] * LOG2E
            m = jnp.maximum(jnp.max(s, axis=-1, keepdims=True), sink)
            e = jnp.exp2(s - m)
            den = jnp.sum(e, axis=-1, keepdims=True) + jnp.exp2(sink - m)
            probs.append(e.astype(BF16))
            inv.append(1.0 / den)
        r_all = jnp.dot(jnp.concatenate(probs, axis=0), va, preferred_element_type=F32)
        out = []
        for p in range(QA_W // LANES):
            r0 = r_all[(2 * p) * n:(2 * p + 1) * n] * inv[2 * p]
            r1 = r_all[(2 * p + 1) * n:(2 * p + 2) * n] * inv[2 * p + 1]
            out.append(jnp.where(lo, r0, r1))
        return jnp.concatenate(out, axis=1)

    def finish_b(p, s2):
        sl = slice(p * LANES, (p + 1) * LANES)
        probs, inv = [], []
        for j in range(2):
            s = s2[j * n:(j + 1) * n] + bias_ref[2 * p + j]
            if col_ok_b is not None:
                s = jnp.where(col_ok_b, s, NEG)
            m = jnp.max(s, axis=-1, keepdims=True)
            e = jnp.exp2(s - m)
            probs.append(e.astype(BF16))
            inv.append(1.0 / jnp.sum(e, axis=-1, keepdims=True))
        r = jnp.dot(jnp.concatenate(probs, axis=0), vb_ref[win_b, sl], preferred_element_type=F32)
        return jnp.where(lo, r[:n] * inv[0], r[n:] * inv[1])

    s_cur = scores(0)
    o_a, o_b = None, []
    for unit in range(n_pairs + 1):
        s_next = scores(unit + 1) if unit < n_pairs else None
        if unit == 0:
            o_a = finish_a(s_cur)
        else:
            o_b.append(finish_b(unit - 1, s_cur))
        s_cur = s_next
    return o_a, jnp.concatenate(o_b, axis=1)


def _mix_out(o_a, o_b, ga, gb, x, w_ref, g_ref):
    mix = jnp.concatenate([o_a * ga.astype(F32), o_b * gb.astype(F32)], axis=1).astype(BF16)
    m = jnp.dot(mix, w_ref[...], preferred_element_type=F32)
    return x + _rms(m, g_ref[...])


def _attn_prompt_body(sink_ref, x_ref, qa_ref, ga_ref, qb_ref, gb_ref, ka_ref, va_ref, kb_ref,
                      vb_ref, bias_ref, band_ref, w_ref, g_ref, y_ref,
                      kap, vap, kbp, vbp, *, seq):
    t = pl.program_id(1)
    tq = x_ref.shape[1]

    @pl.when(t == 0)
    def _():
        for dst, src in ((kap, ka_ref), (vap, va_ref), (kbp, kb_ref), (vbp, vb_ref)):
            dst[:PAD_ROWS, :] = jnp.zeros((PAD_ROWS, dst.shape[1]), BF16)
            dst[PAD_ROWS:, :] = src[0]

    band = band_ref[...]

    def group(sub, masked):
        start = pl.multiple_of(t * tq + sub * GRP, GRP)
        rs = slice(sub * GRP, (sub + 1) * GRP)
        win_a = pl.ds(pl.multiple_of(start + (PAD_ROWS - N_PREV_A * CHUNK), GRP), WIN_A)
        win_b = pl.ds(start, WIN_B)
        ok_a = ok_b = None
        if masked:
            ok_a = lax.broadcasted_iota(jnp.int32, (1, WIN_A), 1) >= N_PREV_A * CHUNK - start
            ok_b = lax.broadcasted_iota(jnp.int32, (1, WIN_B), 1) >= N_PREV_B * CHUNK - start
        o_a, o_b = _attend_group(qa_ref[0, rs, :], qb_ref[0, rs, :], kap[win_a, :], vap[win_a, :],
                                 kbp, vbp, win_b, sink_ref, bias_ref, band, ok_a, ok_b)
        y_ref[0, rs, :] = _mix_out(o_a, o_b, ga_ref[0, rs, :], gb_ref[0, rs, :], x_ref[0, rs, :],
                                   w_ref, g_ref)

    n_masked = PAD_ROWS // tq

    @pl.when(t < n_masked)
    def _():
        for sub in range(tq // GRP):
            group(sub, True)

    @pl.when(t >= n_masked)
    def _():
        for sub in range(tq // GRP):
            group(sub, False)


def _attn_prompt(sinks, x, qa, ga, qb, gb, ka, va, kb, vb, bias, band, w_out, g_post, tq):
    b, seq, _ = x.shape
    tile = lambda w: pl.BlockSpec((1, tq, w), lambda i, j: (i, j, 0))
    whole = lambda w: pl.BlockSpec((1, seq, w), lambda i, j: (i, 0, 0))
    fixed = lambda shape: pl.BlockSpec(shape, lambda i, j: (0,) * len(shape))
    return pl.pallas_call(
        functools.partial(_attn_prompt_body, seq=seq),
        grid=(b, seq // tq),
        in_specs=[pl.BlockSpec(memory_space=pltpu.SMEM),
                  tile(D_MODEL), tile(QA_W), tile(QA_W), tile(B_W), tile(B_W),
                  whole(KA_W), whole(KA_W), whole(B_W), whole(B_W),
                  fixed(bias.shape), fixed(band.shape), fixed(w_out.shape), fixed((1, D_MODEL))],
        out_specs=tile(D_MODEL),
        out_shape=jax.ShapeDtypeStruct(x.shape, F32),
        scratch_shapes=[pltpu.VMEM((PAD_ROWS + seq, KA_W), BF16),
                        pltpu.VMEM((PAD_ROWS + seq, KA_W), BF16),
                        pltpu.VMEM((PAD_ROWS + seq, B_W), BF16),
                        pltpu.VMEM((PAD_ROWS + seq, B_W), BF16)],
        compiler_params=_params(2),
        name="attn_prompt",
    )(sinks, x, qa, ga, qb, gb, ka, va, kb, vb, bias, band, w_out, g_post)


SAMPLE_REQS = 4


def _softmax2(s_c, s_n, sink):
    m = jnp.maximum(jnp.max(s_c, axis=-1, keepdims=True), jnp.max(s_n, axis=-1, keepdims=True))
    if sink is not None:
        m = jnp.maximum(m, sink)
    e_c = jnp.exp2(s_c - m)
    e_n = jnp.exp2(s_n - m)
    den = jnp.sum(e_c, axis=-1, keepdims=True) + jnp.sum(e_n, axis=-1, keepdims=True)
    if sink is not None:
        den = den + jnp.exp2(sink - m)
    return e_c.astype(BF16), e_n.astype(BF16), 1.0 / den


def _roll_in(cache, new):
    t, w = new.shape[0], cache.shape[1]
    lane = lax.broadcasted_iota(jnp.int32, (cache.shape[0], LANES), 1)
    padded = jnp.concatenate([jnp.zeros((LANES - t, new.shape[1]), F32), new.astype(F32)], axis=0)
    rolled = pltpu.roll(cache, w - t, 1)
    last = jnp.where(lane >= LANES - t, padded.T, rolled[:, w - LANES:])
    return last if w == LANES else jnp.concatenate([rolled[:, :w - LANES], last], axis=1)


def _attn_sample_body(sink_ref, x_ref, qa_ref, ga_ref, qb_ref, gb_ref, ka_ref, va_ref, kb_ref,
                      vb_ref, cak_ref, cav_ref, cbk_ref, cbv_ref, bias_ref, w_ref, g_ref,
                      y_ref, nak_ref, nav_ref, nbk_ref, nbv_ref):
    nreq, t, _ = x_ref.shape
    for r in range(nreq):
        for dst, cache, new in ((nak_ref, cak_ref, ka_ref), (nav_ref, cav_ref, va_ref),
                                (nbk_ref, cbk_ref, kb_ref), (nbv_ref, cbv_ref, vb_ref)):
            dst[r] = _roll_in(cache[r], new[r])

    wb = cbk_ref.shape[2]
    lane = lax.broadcasted_iota(jnp.int32, (t, LANES), 1)
    lo = lane < HEAD_DIM
    zero = jnp.zeros((t, LANES), BF16)
    halves = lambda tile: (jnp.where(lo, tile, zero), jnp.where(lo, zero, tile))

    mix = []
    for r in range(nreq):
        qa = qa_ref[r]
        lhs = jnp.concatenate([h for p in range(QA_W // LANES)
                               for h in halves(qa[:, p * LANES:(p + 1) * LANES])], axis=0)
        s_c = jnp.dot(lhs, cak_ref[r].astype(BF16), preferred_element_type=F32)
        s_n = _nt_dot(lhs, ka_ref[r])
        cav = cav_ref[r].astype(BF16)
        va = va_ref[r]
        o_a = []
        for p in range(QA_W // LANES):
            pair = []
            for j in range(2):
                h = 2 * p + j
                rs = slice(h * t, (h + 1) * t)
                e_c, e_n, inv = _softmax2(s_c[rs], s_n[rs], sink_ref[h] * LOG2E)
                o = _nt_dot(e_c, cav) + jnp.dot(e_n, va, preferred_element_type=F32)
                pair.append(o * inv)
            o_a.append(jnp.where(lo, pair[0], pair[1]))
        qb = qb_ref[r]
        o_b = []
        for p in range(B_W // LANES):
            sl = slice(p * LANES, (p + 1) * LANES)
            lhs = jnp.concatenate(halves(qb[:, sl]), axis=0)
            s_c = jnp.dot(lhs, cbk_ref[r, sl, :].astype(BF16), preferred_element_type=F32)
            s_n = _nt_dot(lhs, kb_ref[r, :, sl])
            cbv = cbv_ref[r, sl, :].astype(BF16)
            vb = vb_ref[r, :, sl]
            pair = []
            for j in range(2):
                rs = slice(j * t, (j + 1) * t)
                bias = bias_ref[2 * p + j]
                e_c, e_n, inv = _softmax2(s_c[rs] + bias[:, :wb], s_n[rs] + bias[:, wb:wb + t], None)
                o = _nt_dot(e_c, cbv) + jnp.dot(e_n, vb, preferred_element_type=F32)
                pair.append(o * inv)
            o_b.append(jnp.where(lo, pair[0], pair[1]))
        mix.append(jnp.concatenate(
            [jnp.concatenate(o_a, axis=1) * ga_ref[r].astype(F32),
             jnp.concatenate(o_b, axis=1) * gb_ref[r].astype(F32)], axis=1).astype(BF16))
    m = jnp.dot(jnp.concatenate(mix, axis=0), w_ref[...], preferred_element_type=F32)
    y = x_ref[...].reshape(nreq * t, D_MODEL) + _rms(m, g_ref[...])
    y_ref[...] = y.reshape(nreq, t, D_MODEL)


def _attn_sample(sinks, x, qa, ga, qb, gb, ka, va, kb, vb, cak, cav, cbk, cbv, bias, w_out, g_post):
    b = x.shape[0]
    per = lambda a: pl.BlockSpec((SAMPLE_REQS,) + a.shape[1:], lambda i: (i, 0, 0))
    fixed = lambda shape: pl.BlockSpec(shape, lambda i: (0,) * len(shape))
    arrs = (x, qa, ga, qb, gb, ka, va, kb, vb, cak, cav, cbk, cbv)
    return pl.pallas_call(
        _attn_sample_body,
        grid=(b // SAMPLE_REQS,),
        in_specs=[pl.BlockSpec(memory_space=pltpu.SMEM)] + [per(a) for a in arrs]
                 + [fixed(bias.shape), fixed(w_out.shape), fixed((1, D_MODEL))],
        out_specs=[per(a) for a in (x, cak, cav, cbk, cbv)],
        out_shape=[jax.ShapeDtypeStruct(a.shape, F32) for a in (x, cak, cav, cbk, cbv)],
        compiler_params=_params(1),
        name="attn_sample",
    )(sinks, *arrs, bias, w_out, g_post)


LRU_SEG = 8
LRU_STEPS = 32
LRU_TILE = LRU_SEG * LRU_STEPS
N_TAIL = CONV_W - 1
LRU_TILES_PER_STEP = 4
LRU_CB = D_LRU
N_CB = D_LRU // LRU_CB


def _lru_block(xb, z, tail, carry_in, cb, chained, w, after_gates):
    cw_ref, cb_ref, wa_ref, ba_ref, wx_ref, bx_ref, lam_ref = w[4:11]
    cs = slice(cb * LRU_CB, (cb + 1) * LRU_CB)
    grp = lambda v, g, n=1: v[g * LRU_SEG:(g + n) * LRU_SEG]

    if chained:
        sub = lax.broadcasted_iota(jnp.int32, (LRU_SEG, LRU_CB), 0)
        before = [jnp.where(sub == 0, pltpu.roll(grp(tail, j), 1, 0),
                            pltpu.roll(grp(xb, LRU_STEPS - N_TAIL + j), 1, 0))
                  for j in range(N_TAIL)]
    else:
        before = tail
    xb_tail = grp(xb, LRU_STEPS - N_TAIL, N_TAIL)

    cw = cw_ref[:, cs]
    xc = cb_ref[:, cs] + xb * cw[CONV_W - 1:CONV_W]
    for k in range(1, CONV_W):
        shifted = jnp.concatenate(before[N_TAIL - k:] + [grp(xb, 0, LRU_STEPS - k)], axis=0)
        xc = xc + shifted * cw[CONV_W - 1 - k:CONV_W - k]

    xcb = xc.astype(BF16)
    per = LRU_CB // MXU_DIM

    def gate(w_ref):
        return jnp.concatenate(
            [jnp.dot(xcb[:, j * MXU_DIM:(j + 1) * MXU_DIM], w_ref[cb * per + j],
                     preferred_element_type=F32) for j in range(per)], axis=1)

    pre_r, pre_i = gate(wa_ref), gate(wx_ref)
    after_gates()
    r = jax.nn.sigmoid(pre_r + ba_ref[:, cs])
    gi = jax.nn.sigmoid(pre_i + bx_ref[:, cs])
    lam = lam_ref[:, cs]
    log_sig = jnp.minimum(lam, 0.0) - jnp.log1p(jnp.exp(-jnp.abs(lam)))
    log_a = r * (C_GATE * log_sig)
    a = jnp.exp(log_a)
    u = jnp.sqrt(jnp.tanh(-log_a) * (a * a + 1.0)) * (gi * xc)

    hc, ac = grp(u, 0), grp(a, 0)
    h_loc, a_loc = [hc], [ac]
    for g in range(1, LRU_STEPS):
        ag = grp(a, g)
        hc = ag * hc + grp(u, g)
        ac = ag * ac
        h_loc.append(hc)
        a_loc.append(ac)

    if chained:
        c = carry_in
        rows = []
        for s in range(LRU_SEG):
            rows.append(c)
            c = ac[s:s + 1] * c + hc[s:s + 1]
        carry = jnp.concatenate(rows, axis=0)
    else:
        c = None
        carry = carry_in
    h_groups = [hl + al * carry for hl, al in zip(h_loc, a_loc)]
    yl = (jnp.concatenate(h_groups, axis=0) * _silu(z)).astype(BF16)
    return yl, c, xb_tail, h_groups[-1]


def _lru_run(load_x, store_y, n_tiles, tails, carries, chained, w):
    p_ref, pt_ref, gpre_ref, win_ref = w[:4]
    wout_ref, gpost_ref = w[11:13]
    hp = {}

    def in_proj(unit):
        ti, cb = divmod(unit, N_CB)
        if ti not in hp:
            h = _rms(load_x(ti), gpre_ref[...]).astype(BF16)
            hp[ti] = jnp.dot(p_ref[...], h, preferred_element_type=F32).astype(BF16)
        lo = cb * LRU_CB
        xb = jnp.dot(hp[ti], win_ref[:, lo:lo + LRU_CB], preferred_element_type=F32)
        z = jnp.dot(hp[ti], win_ref[:, D_LRU + lo:D_LRU + lo + LRU_CB], preferred_element_type=F32)
        return xb, z

    units = n_tiles * N_CB
    ready = {0: in_proj(0)}
    h_last = [None] * N_CB
    m = None
    for unit in range(units):
        def emit_next(unit=unit):
            if unit + 1 < units:
                ready[unit + 1] = in_proj(unit + 1)

        ti, cb = divmod(unit, N_CB)
        xb, z = ready.pop(unit)
        yl, carries[cb], tails[cb], h_last[cb] = _lru_block(xb, z, tails[cb], carries[cb], cb,
                                                            chained, w, emit_next)
        yl = jnp.dot(pt_ref[...], yl, preferred_element_type=F32).astype(BF16)
        part = jnp.dot(yl, wout_ref[cb * LRU_CB:(cb + 1) * LRU_CB, :], preferred_element_type=F32)
        m = part if cb == 0 else m + part
        if cb == N_CB - 1:
            store_y(ti, load_x(ti) + _rms(m, gpost_ref[...]))
    return h_last


def _lru_body(x_ref, h0_ref, c0_ref, *rest, chained):
    w, (y_ref, hl_ref, cl_ref, h_s, tail_s) = rest[:-5], rest[-5:]
    t = pl.program_id(1)
    n_tiles = x_ref.shape[1] // LRU_TILE
    blocks = [slice(cb * LRU_CB, (cb + 1) * LRU_CB) for cb in range(N_CB)]
    rows = lambda ti: slice(ti * LRU_TILE, (ti + 1) * LRU_TILE)
    load_x = lambda ti: x_ref[0, rows(ti), :]

    def store_y(ti, y):
        y_ref[0, rows(ti), :] = y

    if not chained:
        tails = [[c0_ref[j, :, cs] for j in range(N_TAIL)] for cs in blocks]
        carries = [h0_ref[0, :, cs] for cs in blocks]
        h_last = _lru_run(load_x, store_y, n_tiles, tails, carries, False, w)
        hl_ref[0] = jnp.concatenate(h_last, axis=1)
        cl_ref[...] = jnp.concatenate(tails, axis=1).reshape(N_TAIL, LRU_SEG, D_LRU)
        return

    @pl.when(t == 0)
    def _():
        h_s[...] = h0_ref[0]
        tail_s[...] = jnp.zeros(tail_s.shape, F32)
        for j in range(N_TAIL):
            tail_s[j * LRU_SEG + LRU_SEG - 1:(j + 1) * LRU_SEG, :] = c0_ref[0, j:j + 1, :]

    tails = [tail_s[:, cs] for cs in blocks]
    carries = [h_s[:, cs] for cs in blocks]
    _lru_run(load_x, store_y, n_tiles, tails, carries, True, w)
    c = jnp.concatenate(carries, axis=1)
    tail = jnp.concatenate(tails, axis=1)
    h_s[...] = c
    tail_s[...] = tail

    @pl.when(t == pl.num_programs(1) - 1)
    def _():
        hl_ref[0] = c
        last = LRU_SEG - 1
        cl_ref[0] = jnp.concatenate(
            [tail[j * LRU_SEG + last:(j + 1) * LRU_SEG] for j in range(N_TAIL)], axis=0)


def _lru(x, h0, c0, weights, chained):
    nb, seq, _ = x.shape
    fixed = lambda a: pl.BlockSpec(a.shape, lambda i, j: (0,) * a.ndim)
    perm = np.zeros((LRU_TILE, LRU_TILE), np.float32)
    rho = np.arange(LRU_TILE)
    perm[rho, (rho % LRU_SEG) * LRU_STEPS + rho // LRU_SEG] = 1.0
    perm_mats = (jnp.asarray(perm, BF16), jnp.asarray(perm.T, BF16))
    if chained:
        state_specs = [pl.BlockSpec((1, 1, D_LRU), lambda i, j: (i, 0, 0)),
                       pl.BlockSpec((1, N_TAIL, D_LRU), lambda i, j: (i, 0, 0))]
        state_shapes = [jax.ShapeDtypeStruct((nb, 1, D_LRU), F32),
                        jax.ShapeDtypeStruct((nb, N_TAIL, D_LRU), F32)]
    else:
        assert seq == LRU_TILE
        state_specs = [pl.BlockSpec((1, LRU_SEG, D_LRU), lambda i, j: (i, 0, 0)),
                       pl.BlockSpec((N_TAIL, LRU_SEG, D_LRU), lambda i, j: (0, i, 0))]
        state_shapes = [jax.ShapeDtypeStruct((nb, LRU_SEG, D_LRU), F32),
                        jax.ShapeDtypeStruct((N_TAIL, nb * LRU_SEG, D_LRU), F32)]
    rows = LRU_TILE * (LRU_TILES_PER_STEP if chained else 1)
    tile = pl.BlockSpec((1, rows, D_MODEL), lambda i, j: (i, j, 0))
    return pl.pallas_call(
        functools.partial(_lru_body, chained=chained),
        grid=(nb, seq // rows),
        in_specs=[tile] + state_specs + [fixed(a) for a in perm_mats + weights],
        out_specs=[tile] + state_specs,
        out_shape=[jax.ShapeDtypeStruct(x.shape, F32)] + state_shapes,
        scratch_shapes=[pltpu.VMEM((1, D_LRU), F32), pltpu.VMEM((N_TAIL * LRU_SEG, D_LRU), F32)],
        compiler_params=_params(2),
        name="lru_chained" if chained else "lru_batched",
    )(x, h0, c0, *perm_mats, *weights)


def _rope_tables(pos):
    half = HEAD_DIM // 2
    inv = ROPE_THETA ** (-jnp.arange(half, dtype=F32) / half)
    ang = pos.astype(F32)[:, None] * inv[None, :]
    cos = jnp.tile(jnp.cos(ang), (1, LANES // half))
    sin = jnp.sin(ang)
    sin = jnp.tile(jnp.concatenate([-sin, sin], axis=1), (1, LANES // HEAD_DIM))
    return cos, sin


def _interleave_groups(w, axis, unit=HEAD_DIM):
    shape = w.shape
    per = H_A // KV_A
    w = w.reshape(shape[:axis] + (KV_A, per, unit) + shape[axis + 1:])
    w = jnp.swapaxes(w, axis, axis + 1)
    return w.reshape(shape)


def _block_diag(w):
    per = MXU_DIM // BLOCK
    w = w.reshape(N_BLOCKS // per, per, BLOCK, BLOCK)
    eye = jnp.eye(per, dtype=w.dtype)
    w = w[:, :, :, None, :] * eye[None, :, None, :, None]
    return w.reshape(N_BLOCKS // per, MXU_DIM, MXU_DIM)


def kernel(x_prompt, x_sample, cache_a_k, cache_a_v, cache_b_k, cache_b_v, state_c_h, state_c_conv,
           ln_pre, ln_post, w_in_ab, sinks_a, relpos_b, w_out_ab, w_in_c, conv_c_w, conv_c_b,
           gate_c_wa, gate_c_ba, gate_c_wx, gate_c_bx, lambda_c, w_out_c):
    bp, s_len, _ = x_prompt.shape
    bs, t_s, _ = x_sample.shape
    wa_rows = cache_a_k.shape[2]
    wb_rows = cache_b_k.shape[2]
    assert wa_rows + t_s <= WIN_A and wb_rows + t_s <= WIN_B and wb_rows == PAD_ROWS

    w_in = w_in_ab[0]
    w_in = jnp.concatenate([_interleave_groups(w_in[:, :QA_W], 1), w_in[:, QA_W:QA_W + 2 * KA_W],
                            _interleave_groups(w_in[:, QA_W + 2 * KA_W:2 * QA_W + 2 * KA_W], 1),
                            w_in[:, 2 * QA_W + 2 * KA_W:]], axis=1).astype(BF16)
    w_out = jnp.concatenate([_interleave_groups(w_out_ab[0, :QA_W], 0), w_out_ab[0, QA_W:]],
                            axis=0).astype(BF16)
    sinks = _interleave_groups(sinks_a[0], 0, unit=1)
    g_pre0 = ln_pre[0].reshape(1, D_MODEL)
    g_post0 = ln_post[0].reshape(1, D_MODEL)
    bias_p, bias_s, band = _build_bias(relpos_b[0], t_s, wb_rows + t_s)

    cos_p, sin_p = _rope_tables(jnp.arange(s_len, dtype=jnp.int32))
    assert wb_rows == PROJ_TM and wa_rows <= PROJ_TM and s_len % PROJ_TM == 0
    proj = _proj0(x_prompt.reshape(bp * s_len, D_MODEL), g_pre0, w_in, cos_p, sin_p, PROJ_TM,
                  seq=s_len, wa_rows=wa_rows)
    qa, ka, va, ga, qb, kb, vb, gb = [a.reshape(bp, s_len, a.shape[-1]) for a in proj[:8]]
    y0_p = _attn_prompt(sinks, x_prompt, qa, ga, qb, gb, ka, va, kb, vb, bias_p, band, w_out,
                        g_post0, ATT_TQ)
    tmaj = lambda c, heads: jnp.transpose(
        c.reshape(c.shape[0], heads, HEAD_DIM, c.shape[2]), (0, 3, 1, 2))[None]
    nak_p, nav_p = tmaj(proj[8], KV_A), tmaj(proj[9], KV_A)
    nbk_p, nbv_p = tmaj(proj[10], H_B), tmaj(proj[11], H_B)

    n_s = bs * t_s
    tm_s = min(PROJ_TM, n_s)
    cos_s, sin_s = _rope_tables(PAST_LEN + jnp.arange(t_s, dtype=jnp.int32))
    cos_s = jnp.tile(cos_s, (tm_s // t_s, 1))
    sin_s = jnp.tile(sin_s, (tm_s // t_s, 1))
    proj = _proj0(x_sample.reshape(n_s, D_MODEL), g_pre0, w_in, cos_s, sin_s, tm_s)
    qa, ka, va, ga, qb, kb, vb, gb = [a.reshape(bs, t_s, a.shape[-1]) for a in proj]
    fmaj = lambda c: jnp.transpose(c[0], (0, 2, 3, 1))
    cak, cav, cbk, cbv = fmaj(cache_a_k), fmaj(cache_a_v), fmaj(cache_b_k), fmaj(cache_b_v)
    flat = lambda c: c.reshape(bs, c.shape[1] * HEAD_DIM, c.shape[3])
    y0_s, nak_s, nav_s, nbk_s, nbv_s = _attn_sample(
        sinks, x_sample, qa, ga, qb, gb, ka, va, kb, vb, flat(cak), flat(cav), flat(cbk), flat(cbv),
        bias_s, w_out, g_post0)
    nak_s, nav_s = tmaj(nak_s, KV_A), tmaj(nav_s, KV_A)
    nbk_s, nbv_s = tmaj(nbk_s, H_B), tmaj(nbv_s, H_B)

    row = lambda v: v.reshape(1, -1)
    lru_w = (row(ln_pre[1]), w_in_c[0].astype(BF16), conv_c_w[0], row(conv_c_b[0]),
             _block_diag(gate_c_wa[0]).astype(BF16), row(gate_c_ba[0]),
             _block_diag(gate_c_wx[0]).astype(BF16), row(gate_c_bx[0]),
             row(lambda_c[0]), w_out_c[0].astype(BF16), row(ln_post[1]))
    y1_p, hl_p, cl_p = _lru(y0_p, jnp.zeros((bp, 1, D_LRU), F32), jnp.zeros((bp, N_TAIL, D_LRU), F32),
                            lru_w, chained=True)
    assert t_s == LRU_STEPS and bs % LRU_SEG == 0
    y1_s, hl_s, cl_s = _lru(y0_s.reshape(bs // LRU_SEG, LRU_TILE, D_MODEL),
                            state_c_h[0].reshape(bs // LRU_SEG, LRU_SEG, D_LRU),
                            jnp.transpose(state_c_conv[0], (1, 0, 2)), lru_w, chained=False)
    y1_s = y1_s.reshape(bs, t_s, D_MODEL)
    cl_s = jnp.transpose(cl_s, (1, 0, 2))

    return (y1_p, y1_s, nak_p, nav_p, nbk_p, nbv_p,
            hl_p.reshape(1, bp, D_LRU), cl_p.reshape(1, bp, N_TAIL, D_LRU),
            nak_s, nav_s, nbk_s, nbv_s,
            hl_s.reshape(1, bs, D_LRU), cl_s.reshape(1, bs, N_TAIL, D_LRU))
```

```python
import functools

import jax
import jax.numpy as jnp
import numpy as np
from jax import lax
from jax.experimental import pallas as pl
from jax.experimental.pallas import tpu as pltpu

F32 = jnp.float32
BF16 = jnp.bfloat16

D_MODEL = 1024
CHUNK = 64
HEAD_DIM = 64
H_A = 8
KV_A = 2
N_PREV_A = 2
H_B = 8
N_PREV_B = 8
MAX_REL = 128
ROPE_THETA = 10000.0
D_LRU = D_MODEL
N_BLOCKS = 16
BLOCK = D_LRU // N_BLOCKS
CONV_W = 4
C_GATE = 8.0
EPS = 1e-6
PAST_LEN = 1024
NEG = -1e30
LOG2E = 1.4426950408889634
Q_SCALE = HEAD_DIM ** -0.5 * LOG2E

LANES = 128
MXU_DIM = 256
VMEM_LIMIT = 56 * 1024 * 1024

QA_W = H_A * HEAD_DIM
KA_W = KV_A * HEAD_DIM
B_W = H_B * HEAD_DIM
MIX_W = QA_W + B_W
GRP = 2 * CHUNK
WIN_A = (N_PREV_A + 2) * CHUNK
WIN_B = (N_PREV_B + 2) * CHUNK
PAD_ROWS = N_PREV_B * CHUNK
TOEP_W = WIN_B + LANES

PROJ_TM = 512
ATT_TQ = 512


def _params(n_axes):
    return pltpu.CompilerParams(
        dimension_semantics=("arbitrary",) * n_axes,
        vmem_limit_bytes=VMEM_LIMIT)


def _rms(x, g):
    ms = jnp.mean(x * x, axis=-1, keepdims=True)
    return x * lax.rsqrt(ms + EPS) * g


def _silu(x):
    return x * jax.nn.sigmoid(x)


def _proj0_body(x_ref, g_ref, w_ref, cos_ref, sin_ref,
                qa_ref, ka_ref, va_ref, ga_ref, qb_ref, kb_ref, vb_ref, gb_ref, *cache_refs,
                tiles_per_seq, wa_rows):
    h = _rms(x_ref[...], g_ref[...]).astype(BF16)
    cos = cos_ref[...]
    sin = sin_ref[...]
    tm = cos.shape[0]
    lane = lax.broadcasted_iota(jnp.int32, (tm, LANES), 1)
    lower = (lane & (HEAD_DIM - 1)) < (HEAD_DIM // 2)

    def mm(lo, hi):
        return jnp.dot(h, w_ref[:, lo:hi], preferred_element_type=F32)

    def rope(t):
        partner = jnp.where(lower, pltpu.roll(t, LANES - HEAD_DIM // 2, 1),
                            pltpu.roll(t, HEAD_DIM // 2, 1))
        return t * cos + partner * sin

    o = 0
    t = mm(o, o + QA_W)
    for k in range(QA_W // LANES):
        sl = slice(k * LANES, (k + 1) * LANES)
        qa_ref[:, sl] = (rope(t[:, sl]) * Q_SCALE).astype(BF16)
    o += QA_W
    ka = rope(mm(o, o + KA_W))
    ka_ref[...] = ka.astype(BF16)
    o += KA_W
    va = mm(o, o + KA_W)
    va_ref[...] = va.astype(BF16)
    o += KA_W
    ga_ref[...] = _silu(mm(o, o + QA_W)).astype(BF16)
    o += QA_W
    qb_ref[...] = (mm(o, o + B_W) * Q_SCALE).astype(BF16)
    o += B_W
    kb = mm(o, o + B_W)
    kb_ref[...] = kb.astype(BF16)
    o += B_W
    vb = mm(o, o + B_W)
    vb_ref[...] = vb.astype(BF16)
    o += B_W
    gb_ref[...] = _silu(mm(o, o + B_W)).astype(BF16)

    if cache_refs:
        @pl.when(pl.program_id(0) % tiles_per_seq == tiles_per_seq - 1)
        def _():
            kat_ref, vat_ref, kbt_ref, vbt_ref = cache_refs
            kat_ref[0] = ka[tm - wa_rows:].T
            vat_ref[0] = va[tm - wa_rows:].T
            kbt_ref[0] = kb.T
            vbt_ref[0] = vb.T


def _proj0(x2d, g, w, cos, sin, tm, seq=None, wa_rows=None):
    n = x2d.shape[0]
    n_pos_blocks = cos.shape[0] // tm
    widths = (QA_W, KA_W, KA_W, QA_W, B_W, B_W, B_W, B_W)
    row = lambda i: (i, 0)
    fixed = lambda i: (0, 0)
    pos = lambda i: (i % n_pos_blocks, 0)
    out_specs = [pl.BlockSpec((tm, wd), row) for wd in widths]
    out_shape = [jax.ShapeDtypeStruct((n, wd), BF16) for wd in widths]
    tiles_per_seq = None
    if seq is not None:
        tiles_per_seq = seq // tm
        for feat, keys in ((KA_W, wa_rows), (KA_W, wa_rows), (B_W, tm), (B_W, tm)):
            out_specs.append(pl.BlockSpec((1, feat, keys), lambda i: (i // tiles_per_seq, 0, 0)))
            out_shape.append(jax.ShapeDtypeStruct((n // seq, feat, keys), F32))
    return pl.pallas_call(
        functools.partial(_proj0_body, tiles_per_seq=tiles_per_seq, wa_rows=wa_rows),
        grid=(n // tm,),
        in_specs=[pl.BlockSpec((tm, D_MODEL), row),
                  pl.BlockSpec((1, D_MODEL), fixed),
                  pl.BlockSpec(w.shape, fixed),
                  pl.BlockSpec((tm, LANES), pos),
                  pl.BlockSpec((tm, LANES), pos)],
        out_specs=out_specs,
        out_shape=out_shape,
        compiler_params=_params(1),
        name="proj0",
    )(x2d, g, w, cos, sin)


def _bias_body(r0_ref, bp_ref, bs_ref, band_ref, *, t_sample, keys_sample):
    x = jnp.broadcast_to(r0_ref[0], (GRP, TOEP_W)) * LOG2E
    row = lax.broadcasted_iota(jnp.int32, (GRP, TOEP_W), 0)
    for b in range(GRP.bit_length() - 1):
        x = jnp.where(((row >> b) & 1) == 1, pltpu.roll(x, 1 << b, 1), x)
    x = x[:, :WIN_B]
    r = lax.broadcasted_iota(jnp.int32, (GRP, WIN_B), 0)
    c = lax.broadcasted_iota(jnp.int32, (GRP, WIN_B), 1)
    ci = r // CHUNK
    cj = c // CHUNK
    bp_ref[0] = jnp.where((cj >= ci) & (cj <= ci + N_PREV_B), x, NEG)
    cs = lax.broadcasted_iota(jnp.int32, (t_sample, WIN_B), 1)
    bs_ref[0] = jnp.where(cs < keys_sample, x[:t_sample], NEG)
    ra = lax.broadcasted_iota(jnp.int32, (GRP, WIN_A), 0) // CHUNK
    ca = lax.broadcasted_iota(jnp.int32, (GRP, WIN_A), 1) // CHUNK
    band_ref[...] = jnp.where((ca >= ra) & (ca <= ra + N_PREV_A), 0.0, NEG)


def _build_bias(relpos, t_sample, keys_sample):
    far = relpos[:, 2 * MAX_REL:]
    n_far = PAD_ROWS - MAX_REL + 1
    r0 = jnp.concatenate(
        [jnp.broadcast_to(far, (H_B, n_far)),
         relpos[:, 2 * MAX_REL - 1:0:-1],
         jnp.broadcast_to(far, (H_B, TOEP_W - n_far - (2 * MAX_REL - 1)))], axis=1)
    r0 = r0.reshape(H_B, 1, TOEP_W)
    return pl.pallas_call(
        functools.partial(_bias_body, t_sample=t_sample, keys_sample=keys_sample),
        grid=(H_B,),
        in_specs=[pl.BlockSpec((1, 1, TOEP_W), lambda h: (h, 0, 0))],
        out_specs=[pl.BlockSpec((1, GRP, WIN_B), lambda h: (h, 0, 0)),
                   pl.BlockSpec((1, t_sample, WIN_B), lambda h: (h, 0, 0)),
                   pl.BlockSpec((GRP, WIN_A), lambda h: (0, 0))],
        out_shape=[jax.ShapeDtypeStruct((H_B, GRP, WIN_B), F32),
                   jax.ShapeDtypeStruct((H_B, t_sample, WIN_B), F32),
                   jax.ShapeDtypeStruct((GRP, WIN_A), F32)],
        compiler_params=_params(1),
        name="relbias",
    )(r0)


def _nt_dot(a, b):
    return lax.dot_general(a, b, (((1,), (1,)), ((), ())), preferred_element_type=F32)


def _attend_group(qa, qb, ka, va, kb_ref, vb_ref, win_b, sink_ref, bias_ref, mask_a, col_ok_a,
                  col_ok_b):
    n = qa.shape[0]
    lane = lax.broadcasted_iota(jnp.int32, (n, LANES), 1)
    lo = lane < HEAD_DIM
    zero = jnp.zeros((n, LANES), BF16)
    halves = lambda tile: [jnp.where(lo, tile, zero), jnp.where(lo, zero, tile)]
    n_pairs = B_W // LANES

    def scores(unit):
        if unit == 0:
            lhs = jnp.concatenate([h for p in range(QA_W // LANES)
                                   for h in halves(qa[:, p * LANES:(p + 1) * LANES])], axis=0)
            return _nt_dot(lhs, ka)
        sl = slice((unit - 1) * LANES, unit * LANES)
        return _nt_dot(jnp.concatenate(halves(qb[:, sl]), axis=0), kb_ref[win_b, sl])

    def finish_a(s_all):
        probs, inv = [], []
        for h in range(H_A):
            s = s_all[h * n:(h + 1) * n]
            if mask_a is not None:
                s = s + mask_a
            if col_ok_a is not None:
                s = jnp.where(col_ok_a, s, NEG)
            sink = sink_ref[h] * LOG2E
            m = jnp.maximum(jnp.max(s, axis=-1, keepdims=True), sink)
            e = jnp.exp2(s - m)
            den = jnp.sum(e, axis=-1, keepdims=True) + jnp.exp2(sink - m)
            probs.append(e.astype(BF16))
            inv.append(1.0 / den)
        r_all = jnp.dot(jnp.concatenate(probs, axis=0), va, preferred_element_type=F32)
        out = []
        for p in range(QA_W // LANES):
            r0 = r_all[(2 * p) * n:(2 * p + 1) * n] * inv[2 * p]
            r1 = r_all[(2 * p + 1) * n:(2 * p + 2) * n] * inv[2 * p + 1]
            out.append(jnp.where(lo, r0, r1))
        return jnp.concatenate(out, axis=1)

    def finish_b(p, s2):
        sl = slice(p * LANES, (p + 1) * LANES)
        probs, inv = [], []
        for j in range(2):
            s = s2[j * n:(j + 1) * n] + bias_ref[2 * p + j]
            if col_ok_b is not None:
                s = jnp.where(col_ok_b, s, NEG)
            m = jnp.max(s, axis=-1, keepdims=True)
            e = jnp.exp2(s - m)
            probs.append(e.astype(BF16))
            inv.append(1.0 / jnp.sum(e, axis=-1, keepdims=True))
        r = jnp.dot(jnp.concatenate(probs, axis=0), vb_ref[win_b, sl], preferred_element_type=F32)
        return jnp.where(lo, r[:n] * inv[0], r[n:] * inv[1])

    s_cur = scores(0)
    o_a, o_b = None, []
    for unit in range(n_pairs + 1):
        s_next = scores(unit + 1) if unit < n_pairs else None
        if unit == 0:
            o_a = finish_a(s_cur)
        else:
            o_b.append(finish_b(unit - 1, s_cur))
        s_cur = s_next
    return o_a, jnp.concatenate(o_b, axis=1)


def _mix_out(o_a, o_b, ga, gb, x, w_ref, g_ref):
    mix = jnp.concatenate([o_a * ga.astype(F32), o_b * gb.astype(F32)], axis=1).astype(BF16)
    m = jnp.dot(mix, w_ref[...], preferred_element_type=F32)
    return x + _rms(m, g_ref[...])


def _attn_prompt_body(sink_ref, x_ref, qa_ref, ga_ref, qb_ref, gb_ref, ka_ref, va_ref, kb_ref,
                      vb_ref, bias_ref, band_ref, w_ref, g_ref, y_ref,
                      kap, vap, kbp, vbp, *, seq):
    t = pl.program_id(1)
    tq = x_ref.shape[1]

    @pl.when(t == 0)
    def _():
        for dst, src in ((kap, ka_ref), (vap, va_ref), (kbp, kb_ref), (vbp, vb_ref)):
            dst[:PAD_ROWS, :] = jnp.zeros((PAD_ROWS, dst.shape[1]), BF16)
            dst[PAD_ROWS:, :] = src[0]

    band = band_ref[...]

    def group(sub, masked):
        start = pl.multiple_of(t * tq + sub * GRP, GRP)
        rs = slice(sub * GRP, (sub + 1) * GRP)
        win_a = pl.ds(pl.multiple_of(start + (PAD_ROWS - N_PREV_A * CHUNK), GRP), WIN_A)
        win_b = pl.ds(start, WIN_B)
        ok_a = ok_b = None
        if masked:
            ok_a = lax.broadcasted_iota(jnp.int32, (1, WIN_A), 1) >= N_PREV_A * CHUNK - start
            ok_b = lax.broadcasted_iota(jnp.int32, (1, WIN_B), 1) >= N_PREV_B * CHUNK - start
        o_a, o_b = _attend_group(qa_ref[0, rs, :], qb_ref[0, rs, :], kap[win_a, :], vap[win_a, :],
                                 kbp, vbp, win_b, sink_ref, bias_ref, band, ok_a, ok_b)
        y_ref[0, rs, :] = _mix_out(o_a, o_b, ga_ref[0, rs, :], gb_ref[0, rs, :], x_ref[0, rs, :],
                                   w_ref, g_ref)

    n_masked = PAD_ROWS // tq

    @pl.when(t < n_masked)
    def _():
        for sub in range(tq // GRP):
            group(sub, True)

    @pl.when(t >= n_masked)
    def _():
        for sub in range(tq // GRP):
            group(sub, False)


def _attn_prompt(sinks, x, qa, ga, qb, gb, ka, va, kb, vb, bias, band, w_out, g_post, tq):
    b, seq, _ = x.shape
    tile = lambda w: pl.BlockSpec((1, tq, w), lambda i, j: (i, j, 0))
    whole = lambda w: pl.BlockSpec((1, seq, w), lambda i, j: (i, 0, 0))
    fixed = lambda shape: pl.BlockSpec(shape, lambda i, j: (0,) * len(shape))
    return pl.pallas_call(
        functools.partial(_attn_prompt_body, seq=seq),
        grid=(b, seq // tq),
        in_specs=[pl.BlockSpec(memory_space=pltpu.SMEM),
                  tile(D_MODEL), tile(QA_W), tile(QA_W), tile(B_W), tile(B_W),
                  whole(KA_W), whole(KA_W), whole(B_W), whole(B_W),
                  fixed(bias.shape), fixed(band.shape), fixed(w_out.shape), fixed((1, D_MODEL))],
        out_specs=tile(D_MODEL),
        out_shape=jax.ShapeDtypeStruct(x.shape, F32),
        scratch_shapes=[pltpu.VMEM((PAD_ROWS + seq, KA_W), BF16),
                        pltpu.VMEM((PAD_ROWS + seq, KA_W), BF16),
                        pltpu.VMEM((PAD_ROWS + seq, B_W), BF16),
                        pltpu.VMEM((PAD_ROWS + seq, B_W), BF16)],
        compiler_params=_params(2),
        name="attn_prompt",
    )(sinks, x, qa, ga, qb, gb, ka, va, kb, vb, bias, band, w_out, g_post)


SAMPLE_REQS = 4


def _softmax2(s_c, s_n, sink):
    m = jnp.maximum(jnp.max(s_c, axis=-1, keepdims=True), jnp.max(s_n, axis=-1, keepdims=True))
    if sink is not None:
        m = jnp.maximum(m, sink)
    e_c = jnp.exp2(s_c - m)
    e_n = jnp.exp2(s_n - m)
    den = jnp.sum(e_c, axis=-1, keepdims=True) + jnp.sum(e_n, axis=-1, keepdims=True)
    if sink is not None:
        den = den + jnp.exp2(sink - m)
    return e_c.astype(BF16), e_n.astype(BF16), 1.0 / den


def _roll_in(cache, new):
    t, w = new.shape[0], cache.shape[1]
    lane = lax.broadcasted_iota(jnp.int32, (cache.shape[0], LANES), 1)
    padded = jnp.concatenate([jnp.zeros((LANES - t, new.shape[1]), F32), new.astype(F32)], axis=0)
    rolled = pltpu.roll(cache, w - t, 1)
    last = jnp.where(lane >= LANES - t, padded.T, rolled[:, w - LANES:])
    return last if w == LANES else jnp.concatenate([rolled[:, :w - LANES], last], axis=1)


def _attn_sample_body(sink_ref, x_ref, qa_ref, ga_ref, qb_ref, gb_ref, ka_ref, va_ref, kb_ref,
                      vb_ref, cak_ref, cav_ref, cbk_ref, cbv_ref, bias_ref, w_ref, g_ref,
                      y_ref, nak_ref, nav_ref, nbk_ref, nbv_ref):
    nreq, t, _ = x_ref.shape
    for r in range(nreq):
        for dst, cache, new in ((nak_ref, cak_ref, ka_ref), (nav_ref, cav_ref, va_ref),
                                (nbk_ref, cbk_ref, kb_ref), (nbv_ref, cbv_ref, vb_ref)):
            dst[r] = _roll_in(cache[r], new[r])

    wb = cbk_ref.shape[2]
    lane = lax.broadcasted_iota(jnp.int32, (t, LANES), 1)
    lo = lane < HEAD_DIM
    zero = jnp.zeros((t, LANES), BF16)
    halves = lambda tile: [jnp.where(lo, tile, zero), jnp.where(lo, zero, tile)]
    n_pairs = B_W // LANES

    def scores(r, unit):
        if unit == 0:
            qa = qa_ref[r]
            lhs = jnp.concatenate([h for p in range(QA_W // LANES)
                                   for h in halves(qa[:, p * LANES:(p + 1) * LANES])], axis=0)
            return (jnp.dot(lhs, cak_ref[r].astype(BF16), preferred_element_type=F32),
                    _nt_dot(lhs, ka_ref[r]))
        sl = slice((unit - 1) * LANES, unit * LANES)
        lhs = jnp.concatenate(halves(qb_ref[r][:, sl]), axis=0)
        return (jnp.dot(lhs, cbk_ref[r, sl, :].astype(BF16), preferred_element_type=F32),
                _nt_dot(lhs, kb_ref[r, :, sl]))

    def finish(r, unit, s):
        s_c, s_n = s
        if unit == 0:
            cache_v, new_v = cav_ref[r].astype(BF16), va_ref[r]
            heads = [(h, sink_ref[h] * LOG2E, None) for h in range(H_A)]
        else:
            sl = slice((unit - 1) * LANES, unit * LANES)
            cache_v, new_v = cbv_ref[r, sl, :].astype(BF16), vb_ref[r, :, sl]
            heads = [(j, None, bias_ref[2 * (unit - 1) + j]) for j in range(2)]
        probs_c, probs_n, inv = [], [], []
        for j, sink, bias in heads:
            rs = slice(j * t, (j + 1) * t)
            sc, sn = s_c[rs], s_n[rs]
            if bias is not None:
                sc, sn = sc + bias[:, :wb], sn + bias[:, wb:wb + t]
            e_c, e_n, iv = _softmax2(sc, sn, sink)
            probs_c.append(e_c)
            probs_n.append(e_n)
            inv.append(iv)
        o = (_nt_dot(jnp.concatenate(probs_c, axis=0), cache_v)
             + jnp.dot(jnp.concatenate(probs_n, axis=0), new_v, preferred_element_type=F32))
        tiles = []
        for p in range(len(heads) // 2):
            o0 = o[(2 * p) * t:(2 * p + 1) * t] * inv[2 * p]
            o1 = o[(2 * p + 1) * t:(2 * p + 2) * t] * inv[2 * p + 1]
            tiles.append(jnp.where(lo, o0, o1))
        return tiles

    order = [(r, u) for r in range(nreq) for u in range(n_pairs + 1)]
    pending = {order[0]: scores(*order[0])}
    mix, tiles = [], []
    for i, (r, u) in enumerate(order):
        if i + 1 < len(order):
            pending[order[i + 1]] = scores(*order[i + 1])
        tiles += finish(r, u, pending.pop((r, u)))
        if u == n_pairs:
            gate = jnp.concatenate([ga_ref[r], gb_ref[r]], axis=1).astype(F32)
            mix.append((jnp.concatenate(tiles, axis=1) * gate).astype(BF16))
            tiles = []
    m = jnp.dot(jnp.concatenate(mix, axis=0), w_ref[...], preferred_element_type=F32)
    y = x_ref[...].reshape(nreq * t, D_MODEL) + _rms(m, g_ref[...])
    y_ref[...] = y.reshape(nreq, t, D_MODEL)


def _attn_sample(sinks, x, qa, ga, qb, gb, ka, va, kb, vb, cak, cav, cbk, cbv, bias, w_out, g_post):
    b = x.shape[0]
    per = lambda a: pl.BlockSpec((SAMPLE_REQS,) + a.shape[1:], lambda i: (i, 0, 0))
    fixed = lambda shape: pl.BlockSpec(shape, lambda i: (0,) * len(shape))
    arrs = (x, qa, ga, qb, gb, ka, va, kb, vb, cak, cav, cbk, cbv)
    return pl.pallas_call(
        _attn_sample_body,
        grid=(b // SAMPLE_REQS,),
        in_specs=[pl.BlockSpec(memory_space=pltpu.SMEM)] + [per(a) for a in arrs]
                 + [fixed(bias.shape), fixed(w_out.shape), fixed((1, D_MODEL))],
        out_specs=[per(a) for a in (x, cak, cav, cbk, cbv)],
        out_shape=[jax.ShapeDtypeStruct(a.shape, F32) for a in (x, cak, cav, cbk, cbv)],
        compiler_params=_params(1),
        name="attn_sample",
    )(sinks, *arrs, bias, w_out, g_post)


LRU_SEG = 8
LRU_STEPS = 32
LRU_TILE = LRU_SEG * LRU_STEPS
N_TAIL = CONV_W - 1
LRU_TILES_PER_STEP = 4
LRU_CB = D_LRU
N_CB = D_LRU // LRU_CB


def _lru_block(xb, z, tail, carry_in, cb, chained, w, after_gates):
    cw_ref, cb_ref, wa_ref, ba_ref, wx_ref, bx_ref, lam_ref = w[4:11]
    cs = slice(cb * LRU_CB, (cb + 1) * LRU_CB)
    grp = lambda v, g, n=1: v[g * LRU_SEG:(g + n) * LRU_SEG]

    if chained:
        sub = lax.broadcasted_iota(jnp.int32, (LRU_SEG, LRU_CB), 0)
        before = [jnp.where(sub == 0, pltpu.roll(grp(tail, j), 1, 0),
                            pltpu.roll(grp(xb, LRU_STEPS - N_TAIL + j), 1, 0))
                  for j in range(N_TAIL)]
    else:
        before = tail
    xb_tail = grp(xb, LRU_STEPS - N_TAIL, N_TAIL)

    cw = cw_ref[:, cs]
    xc = cb_ref[:, cs] + xb * cw[CONV_W - 1:CONV_W]
    for k in range(1, CONV_W):
        shifted = jnp.concatenate(before[N_TAIL - k:] + [grp(xb, 0, LRU_STEPS - k)], axis=0)
        xc = xc + shifted * cw[CONV_W - 1 - k:CONV_W - k]

    xcb = xc.astype(BF16)
    per = LRU_CB // MXU_DIM

    both = [jnp.dot(xcb[:, j * MXU_DIM:(j + 1) * MXU_DIM],
                    jnp.concatenate([wa_ref[cb * per + j], wx_ref[cb * per + j]], axis=1),
                    preferred_element_type=F32) for j in range(per)]
    pre_r = jnp.concatenate([b[:, :MXU_DIM] for b in both], axis=1)
    pre_i = jnp.concatenate([b[:, MXU_DIM:] for b in both], axis=1)
    after_gates()
    r = jax.nn.sigmoid(pre_r + ba_ref[:, cs])
    gi = jax.nn.sigmoid(pre_i + bx_ref[:, cs])
    lam = lam_ref[:, cs]
    log_sig = jnp.minimum(lam, 0.0) - jnp.log1p(jnp.exp(-jnp.abs(lam)))
    log_a = r * (C_GATE * log_sig)
    a = jnp.exp(log_a)
    u = jnp.sqrt(jnp.tanh(-log_a) * (a * a + 1.0)) * (gi * xc)

    hc, ac = grp(u, 0), grp(a, 0)
    h_loc, a_loc = [hc], [ac]
    for g in range(1, LRU_STEPS):
        ag = grp(a, g)
        hc = ag * hc + grp(u, g)
        ac = ag * ac
        h_loc.append(hc)
        a_loc.append(ac)

    if chained:
        c = carry_in
        rows = []
        for s in range(LRU_SEG):
            rows.append(c)
            c = ac[s:s + 1] * c + hc[s:s + 1]
        carry = jnp.concatenate(rows, axis=0)
    else:
        c = None
        carry = carry_in
    h_groups = [hl + al * carry for hl, al in zip(h_loc, a_loc)]
    yl = (jnp.concatenate(h_groups, axis=0) * _silu(z)).astype(BF16)
    return yl, c, xb_tail, h_groups[-1]


def _lru_run(load_x, store_y, n_tiles, tails, carries, chained, w):
    p_ref, pt_ref, gpre_ref, win_ref = w[:4]
    wout_ref, gpost_ref = w[11:13]
    hp = {}

    def in_proj(unit):
        ti, cb = divmod(unit, N_CB)
        if ti not in hp:
            h = _rms(load_x(ti), gpre_ref[...]).astype(BF16)
            hp[ti] = jnp.dot(p_ref[...], h, preferred_element_type=F32).astype(BF16)
        lo = cb * LRU_CB
        xb = jnp.dot(hp[ti], win_ref[:, lo:lo + LRU_CB], preferred_element_type=F32)
        z = jnp.dot(hp[ti], win_ref[:, D_LRU + lo:D_LRU + lo + LRU_CB], preferred_element_type=F32)
        return xb, z

    units = n_tiles * N_CB
    ready = {0: in_proj(0)}
    h_last = [None] * N_CB
    m = None
    for unit in range(units):
        def emit_next(unit=unit):
            if unit + 1 < units:
                ready[unit + 1] = in_proj(unit + 1)

        ti, cb = divmod(unit, N_CB)
        xb, z = ready.pop(unit)
        yl, carries[cb], tails[cb], h_last[cb] = _lru_block(xb, z, tails[cb], carries[cb], cb,
                                                            chained, w, emit_next)
        yl = jnp.dot(pt_ref[...], yl, preferred_element_type=F32).astype(BF16)
        part = jnp.dot(yl, wout_ref[cb * LRU_CB:(cb + 1) * LRU_CB, :], preferred_element_type=F32)
        m = part if cb == 0 else m + part
        if cb == N_CB - 1:
            store_y(ti, load_x(ti) + _rms(m, gpost_ref[...]))
    return h_last


def _lru_body(x_ref, h0_ref, c0_ref, *rest, chained):
    w, (y_ref, hl_ref, cl_ref, h_s, tail_s) = rest[:-5], rest[-5:]
    t = pl.program_id(1)
    n_tiles = x_ref.shape[1] // LRU_TILE
    blocks = [slice(cb * LRU_CB, (cb + 1) * LRU_CB) for cb in range(N_CB)]
    rows = lambda ti: slice(ti * LRU_TILE, (ti + 1) * LRU_TILE)
    load_x = lambda ti: x_ref[0, rows(ti), :]

    def store_y(ti, y):
        y_ref[0, rows(ti), :] = y

    if not chained:
        tails = [[c0_ref[j, :, cs] for j in range(N_TAIL)] for cs in blocks]
        carries = [h0_ref[0, :, cs] for cs in blocks]
        h_last = _lru_run(load_x, store_y, n_tiles, tails, carries, False, w)
        hl_ref[0] = jnp.concatenate(h_last, axis=1)
        cl_ref[...] = jnp.concatenate(tails, axis=1).reshape(N_TAIL, LRU_SEG, D_LRU)
        return

    @pl.when(t == 0)
    def _():
        h_s[...] = h0_ref[0]
        tail_s[...] = jnp.zeros(tail_s.shape, F32)
        for j in range(N_TAIL):
            tail_s[j * LRU_SEG + LRU_SEG - 1:(j + 1) * LRU_SEG, :] = c0_ref[0, j:j + 1, :]

    tails = [tail_s[:, cs] for cs in blocks]
    carries = [h_s[:, cs] for cs in blocks]
    _lru_run(load_x, store_y, n_tiles, tails, carries, True, w)
    c = jnp.concatenate(carries, axis=1)
    tail = jnp.concatenate(tails, axis=1)
    h_s[...] = c
    tail_s[...] = tail

    @pl.when(t == pl.num_programs(1) - 1)
    def _():
        hl_ref[0] = c
        last = LRU_SEG - 1
        cl_ref[0] = jnp.concatenate(
            [tail[j * LRU_SEG + last:(j + 1) * LRU_SEG] for j in range(N_TAIL)], axis=0)


def _lru(x, h0, c0, weights, chained):
    nb, seq, _ = x.shape
    fixed = lambda a: pl.BlockSpec(a.shape, lambda i, j: (0,) * a.ndim)
    perm = np.zeros((LRU_TILE, LRU_TILE), np.float32)
    rho = np.arange(LRU_TILE)
    perm[rho, (rho % LRU_SEG) * LRU_STEPS + rho // LRU_SEG] = 1.0
    perm_mats = (jnp.asarray(perm, BF16), jnp.asarray(perm.T, BF16))
    if chained:
        state_specs = [pl.BlockSpec((1, 1, D_LRU), lambda i, j: (i, 0, 0)),
                       pl.BlockSpec((1, N_TAIL, D_LRU), lambda i, j: (i, 0, 0))]
        state_shapes = [jax.ShapeDtypeStruct((nb, 1, D_LRU), F32),
                        jax.ShapeDtypeStruct((nb, N_TAIL, D_LRU), F32)]
    else:
        assert seq == LRU_TILE
        state_specs = [pl.BlockSpec((1, LRU_SEG, D_LRU), lambda i, j: (i, 0, 0)),
                       pl.BlockSpec((N_TAIL, LRU_SEG, D_LRU), lambda i, j: (0, i, 0))]
        state_shapes = [jax.ShapeDtypeStruct((nb, LRU_SEG, D_LRU), F32),
                        jax.ShapeDtypeStruct((N_TAIL, nb * LRU_SEG, D_LRU), F32)]
    rows = LRU_TILE * (LRU_TILES_PER_STEP if chained else 1)
    tile = pl.BlockSpec((1, rows, D_MODEL), lambda i, j: (i, j, 0))
    return pl.pallas_call(
        functools.partial(_lru_body, chained=chained),
        grid=(nb, seq // rows),
        in_specs=[tile] + state_specs + [fixed(a) for a in perm_mats + weights],
        out_specs=[tile] + state_specs,
        out_shape=[jax.ShapeDtypeStruct(x.shape, F32)] + state_shapes,
        scratch_shapes=[pltpu.VMEM((1, D_LRU), F32), pltpu.VMEM((N_TAIL * LRU_SEG, D_LRU), F32)],
        compiler_params=_params(2),
        name="lru_chained" if chained else "lru_batched",
    )(x, h0, c0, *perm_mats, *weights)


def _rope_tables(pos):
    half = HEAD_DIM // 2
    inv = ROPE_THETA ** (-jnp.arange(half, dtype=F32) / half)
    ang = pos.astype(F32)[:, None] * inv[None, :]
    cos = jnp.tile(jnp.cos(ang), (1, LANES // half))
    sin = jnp.sin(ang)
    sin = jnp.tile(jnp.concatenate([-sin, sin], axis=1), (1, LANES // HEAD_DIM))
    return cos, sin


def _interleave_groups(w, axis, unit=HEAD_DIM):
    shape = w.shape
    per = H_A // KV_A
    w = w.reshape(shape[:axis] + (KV_A, per, unit) + shape[axis + 1:])
    w = jnp.swapaxes(w, axis, axis + 1)
    return w.reshape(shape)


def _block_diag(w):
    per = MXU_DIM // BLOCK
    w = w.reshape(N_BLOCKS // per, per, BLOCK, BLOCK)
    eye = jnp.eye(per, dtype=w.dtype)
    w = w[:, :, :, None, :] * eye[None, :, None, :, None]
    return w.reshape(N_BLOCKS // per, MXU_DIM, MXU_DIM)


def kernel(x_prompt, x_sample, cache_a_k, cache_a_v, cache_b_k, cache_b_v, state_c_h, state_c_conv,
           ln_pre, ln_post, w_in_ab, sinks_a, relpos_b, w_out_ab, w_in_c, conv_c_w, conv_c_b,
           gate_c_wa, gate_c_ba, gate_c_wx, gate_c_bx, lambda_c, w_out_c):
    bp, s_len, _ = x_prompt.shape
    bs, t_s, _ = x_sample.shape
    wa_rows = cache_a_k.shape[2]
    wb_rows = cache_b_k.shape[2]
    assert wa_rows + t_s <= WIN_A and wb_rows + t_s <= WIN_B and wb_rows == PAD_ROWS

    w_in = w_in_ab[0]
    w_in = jnp.concatenate([_interleave_groups(w_in[:, :QA_W], 1), w_in[:, QA_W:QA_W + 2 * KA_W],
                            _interleave_groups(w_in[:, QA_W + 2 * KA_W:2 * QA_W + 2 * KA_W], 1),
                            w_in[:, 2 * QA_W + 2 * KA_W:]], axis=1).astype(BF16)
    w_out = jnp.concatenate([_interleave_groups(w_out_ab[0, :QA_W], 0), w_out_ab[0, QA_W:]],
                            axis=0).astype(BF16)
    sinks = _interleave_groups(sinks_a[0], 0, unit=1)
    g_pre0 = ln_pre[0].reshape(1, D_MODEL)
    g_post0 = ln_post[0].reshape(1, D_MODEL)
    bias_p, bias_s, band = _build_bias(relpos_b[0], t_s, wb_rows + t_s)

    cos_p, sin_p = _rope_tables(jnp.arange(s_len, dtype=jnp.int32))
    assert wb_rows == PROJ_TM and wa_rows <= PROJ_TM and s_len % PROJ_TM == 0
    proj = _proj0(x_prompt.reshape(bp * s_len, D_MODEL), g_pre0, w_in, cos_p, sin_p, PROJ_TM,
                  seq=s_len, wa_rows=wa_rows)
    qa, ka, va, ga, qb, kb, vb, gb = [a.reshape(bp, s_len, a.shape[-1]) for a in proj[:8]]
    y0_p = _attn_prompt(sinks, x_prompt, qa, ga, qb, gb, ka, va, kb, vb, bias_p, band, w_out,
                        g_post0, ATT_TQ)
    tmaj = lambda c, heads: jnp.transpose(
        c.reshape(c.shape[0], heads, HEAD_DIM, c.shape[2]), (0, 3, 1, 2))[None]
    nak_p, nav_p = tmaj(proj[8], KV_A), tmaj(proj[9], KV_A)
    nbk_p, nbv_p = tmaj(proj[10], H_B), tmaj(proj[11], H_B)

    n_s = bs * t_s
    tm_s = min(PROJ_TM, n_s)
    cos_s, sin_s = _rope_tables(PAST_LEN + jnp.arange(t_s, dtype=jnp.int32))
    cos_s = jnp.tile(cos_s, (tm_s // t_s, 1))
    sin_s = jnp.tile(sin_s, (tm_s // t_s, 1))
    proj = _proj0(x_sample.reshape(n_s, D_MODEL), g_pre0, w_in, cos_s, sin_s, tm_s)
    qa, ka, va, ga, qb, kb, vb, gb = [a.reshape(bs, t_s, a.shape[-1]) for a in proj]
    fmaj = lambda c: jnp.transpose(c[0], (0, 2, 3, 1))
    cak, cav, cbk, cbv = fmaj(cache_a_k), fmaj(cache_a_v), fmaj(cache_b_k), fmaj(cache_b_v)
    flat = lambda c: c.reshape(bs, c.shape[1] * HEAD_DIM, c.shape[3])
    y0_s, nak_s, nav_s, nbk_s, nbv_s = _attn_sample(
        sinks, x_sample, qa, ga, qb, gb, ka, va, kb, vb, flat(cak), flat(cav), flat(cbk), flat(cbv),
        bias_s, w_out, g_post0)
    nak_s, nav_s = tmaj(nak_s, KV_A), tmaj(nav_s, KV_A)
    nbk_s, nbv_s = tmaj(nbk_s, H_B), tmaj(nbv_s, H_B)

    row = lambda v: v.reshape(1, -1)
    lru_w = (row(ln_pre[1]), w_in_c[0].astype(BF16), conv_c_w[0], row(conv_c_b[0]),
             _block_diag(gate_c_wa[0]).astype(BF16), row(gate_c_ba[0]),
             _block_diag(gate_c_wx[0]).astype(BF16), row(gate_c_bx[0]),
             row(lambda_c[0]), w_out_c[0].astype(BF16), row(ln_post[1]))
    y1_p, hl_p, cl_p = _lru(y0_p, jnp.zeros((bp, 1, D_LRU), F32), jnp.zeros((bp, N_TAIL, D_LRU), F32),
                            lru_w, chained=True)
    assert t_s == LRU_STEPS and bs % LRU_SEG == 0
    y1_s, hl_s, cl_s = _lru(y0_s.reshape(bs // LRU_SEG, LRU_TILE, D_MODEL),
                            state_c_h[0].reshape(bs // LRU_SEG, LRU_SEG, D_LRU),
                            jnp.transpose(state_c_conv[0], (1, 0, 2)), lru_w, chained=False)
    y1_s = y1_s.reshape(bs, t_s, D_MODEL)
    cl_s = jnp.transpose(cl_s, (1, 0, 2))

    return (y1_p, y1_s, nak_p, nav_p, nbk_p, nbv_p,
            hl_p.reshape(1, bp, D_LRU), cl_p.reshape(1, bp, N_TAIL, D_LRU),
            nak_s, nav_s, nbk_s, nbv_s,
            hl_s.reshape(1, bs, D_LRU), cl_s.reshape(1, bs, N_TAIL, D_LRU))
```

```python
import functools

import jax
import jax.numpy as jnp
import numpy as np
from jax import lax
from jax.experimental import pallas as pl
from jax.experimental.pallas import tpu as pltpu

F32 = jnp.float32
BF16 = jnp.bfloat16

D_MODEL = 1024
CHUNK = 64
HEAD_DIM = 64
H_A = 8
KV_A = 2
N_PREV_A = 2
H_B = 8
N_PREV_B = 8
MAX_REL = 128
ROPE_THETA = 10000.0
D_LRU = D_MODEL
N_BLOCKS = 16
BLOCK = D_LRU // N_BLOCKS
CONV_W = 4
C_GATE = 8.0
EPS = 1e-6
PAST_LEN = 1024
NEG = -1e30
LOG2E = 1.4426950408889634
Q_SCALE = HEAD_DIM ** -0.5 * LOG2E

LANES = 128
MXU_DIM = 256
VMEM_LIMIT = 56 * 1024 * 1024

QA_W = H_A * HEAD_DIM
KA_W = KV_A * HEAD_DIM
B_W = H_B * HEAD_DIM
MIX_W = QA_W + B_W
GRP = 2 * CHUNK
WIN_A = (N_PREV_A + 2) * CHUNK
WIN_B = (N_PREV_B + 2) * CHUNK
PAD_ROWS = N_PREV_B * CHUNK
TOEP_W = WIN_B + LANES

PROJ_TM = 512
ATT_TQ = 512


def _params(n_axes):
    return pltpu.CompilerParams(
        dimension_semantics=("arbitrary",) * n_axes,
        vmem_limit_bytes=VMEM_LIMIT)


def _rms(x, g):
    ms = jnp.mean(x * x, axis=-1, keepdims=True)
    return x * lax.rsqrt(ms + EPS) * g


def _silu(x):
    return x * jax.nn.sigmoid(x)


def _proj0_body(x_ref, g_ref, w_ref, cos_ref, sin_ref,
                qa_ref, ka_ref, va_ref, ga_ref, qb_ref, kb_ref, vb_ref, gb_ref, *cache_refs,
                tiles_per_seq, wa_rows):
    h = _rms(x_ref[...], g_ref[...]).astype(BF16)
    cos = cos_ref[...]
    sin = sin_ref[...]
    tm = cos.shape[0]
    lane = lax.broadcasted_iota(jnp.int32, (tm, LANES), 1)
    lower = (lane & (HEAD_DIM - 1)) < (HEAD_DIM // 2)

    def mm(lo, hi):
        return jnp.dot(h, w_ref[:, lo:hi], preferred_element_type=F32)

    def rope(t):
        partner = jnp.where(lower, pltpu.roll(t, LANES - HEAD_DIM // 2, 1),
                            pltpu.roll(t, HEAD_DIM // 2, 1))
        return t * cos + partner * sin

    o = 0
    t = mm(o, o + QA_W)
    for k in range(QA_W // LANES):
        sl = slice(k * LANES, (k + 1) * LANES)
        qa_ref[:, sl] = (rope(t[:, sl]) * Q_SCALE).astype(BF16)
    o += QA_W
    ka = rope(mm(o, o + KA_W))
    ka_ref[...] = ka.astype(BF16)
    o += KA_W
    va = mm(o, o + KA_W)
    va_ref[...] = va.astype(BF16)
    o += KA_W
    ga_ref[...] = _silu(mm(o, o + QA_W)).astype(BF16)
    o += QA_W
    qb_ref[...] = (mm(o, o + B_W) * Q_SCALE).astype(BF16)
    o += B_W
    kb = mm(o, o + B_W)
    kb_ref[...] = kb.astype(BF16)
    o += B_W
    vb = mm(o, o + B_W)
    vb_ref[...] = vb.astype(BF16)
    o += B_W
    gb_ref[...] = _silu(mm(o, o + B_W)).astype(BF16)

    if cache_refs:
        @pl.when(pl.program_id(0) % tiles_per_seq == tiles_per_seq - 1)
        def _():
            kat_ref, vat_ref, kbt_ref, vbt_ref = cache_refs
            kat_ref[0] = ka[tm - wa_rows:].T
            vat_ref[0] = va[tm - wa_rows:].T
            kbt_ref[0] = kb.T
            vbt_ref[0] = vb.T


def _proj0(x2d, g, w, cos, sin, tm, seq=None, wa_rows=None):
    n = x2d.shape[0]
    n_pos_blocks = cos.shape[0] // tm
    widths = (QA_W, KA_W, KA_W, QA_W, B_W, B_W, B_W, B_W)
    row = lambda i: (i, 0)
    fixed = lambda i: (0, 0)
    pos = lambda i: (i % n_pos_blocks, 0)
    out_specs = [pl.BlockSpec((tm, wd), row) for wd in widths]
    out_shape = [jax.ShapeDtypeStruct((n, wd), BF16) for wd in widths]
    tiles_per_seq = None
    if seq is not None:
        tiles_per_seq = seq // tm
        for feat, keys in ((KA_W, wa_rows), (KA_W, wa_rows), (B_W, tm), (B_W, tm)):
            out_specs.append(pl.BlockSpec((1, feat, keys), lambda i: (i // tiles_per_seq, 0, 0)))
            out_shape.append(jax.ShapeDtypeStruct((n // seq, feat, keys), F32))
    return pl.pallas_call(
        functools.partial(_proj0_body, tiles_per_seq=tiles_per_seq, wa_rows=wa_rows),
        grid=(n // tm,),
        in_specs=[pl.BlockSpec((tm, D_MODEL), row),
                  pl.BlockSpec((1, D_MODEL), fixed),
                  pl.BlockSpec(w.shape, fixed),
                  pl.BlockSpec((tm, LANES), pos),
                  pl.BlockSpec((tm, LANES), pos)],
        out_specs=out_specs,
        out_shape=out_shape,
        compiler_params=_params(1),
        name="proj0",
    )(x2d, g, w, cos, sin)


def _bias_body(r0_ref, bp_ref, bs_ref, band_ref, *, t_sample, keys_sample):
    x = jnp.broadcast_to(r0_ref[0], (GRP, TOEP_W)) * LOG2E
    row = lax.broadcasted_iota(jnp.int32, (GRP, TOEP_W), 0)
    for b in range(GRP.bit_length() - 1):
        x = jnp.where(((row >> b) & 1) == 1, pltpu.roll(x, 1 << b, 1), x)
    x = x[:, :WIN_B]
    r = lax.broadcasted_iota(jnp.int32, (GRP, WIN_B), 0)
    c = lax.broadcasted_iota(jnp.int32, (GRP, WIN_B), 1)
    ci = r // CHUNK
    cj = c // CHUNK
    bp_ref[0] = jnp.where((cj >= ci) & (cj <= ci + N_PREV_B), x, NEG)
    cs = lax.broadcasted_iota(jnp.int32, (t_sample, WIN_B), 1)
    bs_ref[0] = jnp.where(cs < keys_sample, x[:t_sample], NEG)
    ra = lax.broadcasted_iota(jnp.int32, (GRP, WIN_A), 0) // CHUNK
    ca = lax.broadcasted_iota(jnp.int32, (GRP, WIN_A), 1) // CHUNK
    band_ref[...] = jnp.where((ca >= ra) & (ca <= ra + N_PREV_A), 0.0, NEG)


def _build_bias(relpos, t_sample, keys_sample):
    far = relpos[:, 2 * MAX_REL:]
    n_far = PAD_ROWS - MAX_REL + 1
    r0 = jnp.concatenate(
        [jnp.broadcast_to(far, (H_B, n_far)),
         relpos[:, 2 * MAX_REL - 1:0:-1],
         jnp.broadcast_to(far, (H_B, TOEP_W - n_far - (2 * MAX_REL - 1)))], axis=1)
    r0 = r0.reshape(H_B, 1, TOEP_W)
    return pl.pallas_call(
        functools.partial(_bias_body, t_sample=t_sample, keys_sample=keys_sample),
        grid=(H_B,),
        in_specs=[pl.BlockSpec((1, 1, TOEP_W), lambda h: (h, 0, 0))],
        out_specs=[pl.BlockSpec((1, GRP, WIN_B), lambda h: (h, 0, 0)),
                   pl.BlockSpec((1, t_sample, WIN_B), lambda h: (h, 0, 0)),
                   pl.BlockSpec((GRP, WIN_A), lambda h: (0, 0))],
        out_shape=[jax.ShapeDtypeStruct((H_B, GRP, WIN_B), F32),
                   jax.ShapeDtypeStruct((H_B, t_sample, WIN_B), F32),
                   jax.ShapeDtypeStruct((GRP, WIN_A), F32)],
        compiler_params=_params(1),
        name="relbias",
    )(r0)


def _nt_dot(a, b):
    return lax.dot_general(a, b, (((1,), (1,)), ((), ())), preferred_element_type=F32)


def _attend_group(qa, qb, ka, va, kb_ref, vb_ref, win_b, sink_ref, bias_ref, mask_a, col_ok_a,
                  col_ok_b):
    n = qa.shape[0]
    lane = lax.broadcasted_iota(jnp.int32, (n, LANES), 1)
    lo = lane < HEAD_DIM
    zero = jnp.zeros((n, LANES), BF16)
    halves = lambda tile: [jnp.where(lo, tile, zero), jnp.where(lo, zero, tile)]
    n_pairs = B_W // LANES

    def scores(unit):
        if unit == 0:
            lhs = jnp.concatenate([h for p in range(QA_W // LANES)
                                   for h in halves(qa[:, p * LANES:(p + 1) * LANES])], axis=0)
            return _nt_dot(lhs, ka)
        sl = slice((unit - 1) * LANES, unit * LANES)
        return _nt_dot(jnp.concatenate(halves(qb[:, sl]), axis=0), kb_ref[win_b, sl])

    def finish_a(s_all):
        probs, inv = [], []
        for h in range(H_A):
            s = s_all[h * n:(h + 1) * n]
            if mask_a is not None:
                s = s + mask_a
            if col_ok_a is not None:
                s = jnp.where(col_ok_a, s, NEG)
            sink = sink_ref[h] * LOG2E
            m = jnp.maximum(jnp.max(s, axis=-1, keepdims=True), sink)
            e = jnp.exp2(s - m)
            den = jnp.sum(e, axis=-1, keepdims=True) + jnp.exp2(sink - m)
            probs.append(e.astype(BF16))
            inv.append(1.0 / den)
        r_all = jnp.dot(jnp.concatenate(probs, axis=0), va, preferred_element_type=F32)
        out = []
        for p in range(QA_W // LANES):
            r0 = r_all[(2 * p) * n:(2 * p + 1) * n] * inv[2 * p]
            r1 = r_all[(2 * p + 1) * n:(2 * p + 2) * n] * inv[2 * p + 1]
            out.append(jnp.where(lo, r0, r1))
        return jnp.concatenate(out, axis=1)

    def finish_b(p, s2):
        sl = slice(p * LANES, (p + 1) * LANES)
        probs, inv = [], []
        for j in range(2):
            s = s2[j * n:(j + 1) * n] + bias_ref[2 * p + j]
            if col_ok_b is not None:
                s = jnp.where(col_ok_b, s, NEG)
            m = jnp.max(s, axis=-1, keepdims=True)
            e = jnp.exp2(s - m)
            probs.append(e.astype(BF16))
            inv.append(1.0 / jnp.sum(e, axis=-1, keepdims=True))
        r = jnp.dot(jnp.concatenate(probs, axis=0), vb_ref[win_b, sl], preferred_element_type=F32)
        return jnp.where(lo, r[:n] * inv[0], r[n:] * inv[1])

    s_cur = scores(0)
    o_a, o_b = None, []
    for unit in range(n_pairs + 1):
        s_next = scores(unit + 1) if unit < n_pairs else None
        if unit == 0:
            o_a = finish_a(s_cur)
        else:
            o_b.append(finish_b(unit - 1, s_cur))
        s_cur = s_next
    return o_a, jnp.concatenate(o_b, axis=1)


def _mix_out(o_a, o_b, ga, gb, x, w_ref, g_ref):
    mix = jnp.concatenate([o_a * ga.astype(F32), o_b * gb.astype(F32)], axis=1).astype(BF16)
    m = jnp.dot(mix, w_ref[...], preferred_element_type=F32)
    return x + _rms(m, g_ref[...])


def _attn_prompt_body(sink_ref, x_ref, qa_ref, ga_ref, qb_ref, gb_ref, ka_ref, va_ref, kb_ref,
                      vb_ref, bias_ref, band_ref, w_ref, g_ref, y_ref,
                      kap, vap, kbp, vbp, *, seq):
    t = pl.program_id(1)
    tq = x_ref.shape[1]

    @pl.when(t == 0)
    def _():
        for dst, src in ((kap, ka_ref), (vap, va_ref), (kbp, kb_ref), (vbp, vb_ref)):
            dst[:PAD_ROWS, :] = jnp.zeros((PAD_ROWS, dst.shape[1]), BF16)
            dst[PAD_ROWS:, :] = src[0]

    band = band_ref[...]

    def group(sub, masked):
        start = pl.multiple_of(t * tq + sub * GRP, GRP)
        rs = slice(sub * GRP, (sub + 1) * GRP)
        win_a = pl.ds(pl.multiple_of(start + (PAD_ROWS - N_PREV_A * CHUNK), GRP), WIN_A)
        win_b = pl.ds(start, WIN_B)
        ok_a = ok_b = None
        if masked:
            ok_a = lax.broadcasted_iota(jnp.int32, (1, WIN_A), 1) >= N_PREV_A * CHUNK - start
            ok_b = lax.broadcasted_iota(jnp.int32, (1, WIN_B), 1) >= N_PREV_B * CHUNK - start
        o_a, o_b = _attend_group(qa_ref[0, rs, :], qb_ref[0, rs, :], kap[win_a, :], vap[win_a, :],
                                 kbp, vbp, win_b, sink_ref, bias_ref, band, ok_a, ok_b)
        y_ref[0, rs, :] = _mix_out(o_a, o_b, ga_ref[0, rs, :], gb_ref[0, rs, :], x_ref[0, rs, :],
                                   w_ref, g_ref)

    n_masked = PAD_ROWS // tq

    @pl.when(t < n_masked)
    def _():
        for sub in range(tq // GRP):
            group(sub, True)

    @pl.when(t >= n_masked)
    def _():
        for sub in range(tq // GRP):
            group(sub, False)


def _attn_prompt(sinks, x, qa, ga, qb, gb, ka, va, kb, vb, bias, band, w_out, g_post, tq):
    b, seq, _ = x.shape
    tile = lambda w: pl.BlockSpec((1, tq, w), lambda i, j: (i, j, 0))
    whole = lambda w: pl.BlockSpec((1, seq, w), lambda i, j: (i, 0, 0))
    fixed = lambda shape: pl.BlockSpec(shape, lambda i, j: (0,) * len(shape))
    return pl.pallas_call(
        functools.partial(_attn_prompt_body, seq=seq),
        grid=(b, seq // tq),
        in_specs=[pl.BlockSpec(memory_space=pltpu.SMEM),
                  tile(D_MODEL), tile(QA_W), tile(QA_W), tile(B_W), tile(B_W),
                  whole(KA_W), whole(KA_W), whole(B_W), whole(B_W),
                  fixed(bias.shape), fixed(band.shape), fixed(w_out.shape), fixed((1, D_MODEL))],
        out_specs=tile(D_MODEL),
        out_shape=jax.ShapeDtypeStruct(x.shape, F32),
        scratch_shapes=[pltpu.VMEM((PAD_ROWS + seq, KA_W), BF16),
                        pltpu.VMEM((PAD_ROWS + seq, KA_W), BF16),
                        pltpu.VMEM((PAD_ROWS + seq, B_W), BF16),
                        pltpu.VMEM((PAD_ROWS + seq, B_W), BF16)],
        compiler_params=_params(2),
        name="attn_prompt",
    )(sinks, x, qa, ga, qb, gb, ka, va, kb, vb, bias, band, w_out, g_post)


SAMPLE_REQS = 4


def _softmax2(s_c, s_n, sink):
    m = jnp.maximum(jnp.max(s_c, axis=-1, keepdims=True), jnp.max(s_n, axis=-1, keepdims=True))
    if sink is not None:
        m = jnp.maximum(m, sink)
    e_c = jnp.exp2(s_c - m)
    e_n = jnp.exp2(s_n - m)
    den = jnp.sum(e_c, axis=-1, keepdims=True) + jnp.sum(e_n, axis=-1, keepdims=True)
    if sink is not None:
        den = den + jnp.exp2(sink - m)
    return e_c.astype(BF16), e_n.astype(BF16), 1.0 / den


def _roll_in(cache, new):
    t, w = new.shape[0], cache.shape[1]
    lane = lax.broadcasted_iota(jnp.int32, (cache.shape[0], LANES), 1)
    padded = jnp.concatenate([jnp.zeros((LANES - t, new.shape[1]), F32), new.astype(F32)], axis=0)
    rolled = pltpu.roll(cache, w - t, 1)
    last = jnp.where(lane >= LANES - t, padded.T, rolled[:, w - LANES:])
    return last if w == LANES else jnp.concatenate([rolled[:, :w - LANES], last], axis=1)


def _attn_sample_body(sink_ref, x_ref, qa_ref, ga_ref, qb_ref, gb_ref, ka_ref, va_ref, kb_ref,
                      vb_ref, cak_ref, cav_ref, cbk_ref, cbv_ref, bias_ref, w_ref, g_ref,
                      y_ref, nak_ref, nav_ref, nbk_ref, nbv_ref):
    nreq, t, _ = x_ref.shape
    for r in range(nreq):
        for dst, cache, new in ((nak_ref, cak_ref, ka_ref), (nav_ref, cav_ref, va_ref),
                                (nbk_ref, cbk_ref, kb_ref), (nbv_ref, cbv_ref, vb_ref)):
            dst[r] = _roll_in(cache[r], new[r])

    wb = cbk_ref.shape[2]
    lane = lax.broadcasted_iota(jnp.int32, (t, LANES), 1)
    lo = lane < HEAD_DIM
    zero = jnp.zeros((t, LANES), BF16)
    halves = lambda tile: [jnp.where(lo, tile, zero), jnp.where(lo, zero, tile)]
    n_pairs = B_W // LANES

    def scores(r, unit):
        if unit == 0:
            qa = qa_ref[r]
            lhs = jnp.concatenate([h for p in range(QA_W // LANES)
                                   for h in halves(qa[:, p * LANES:(p + 1) * LANES])], axis=0)
            return (jnp.dot(lhs, cak_ref[r].astype(BF16), preferred_element_type=F32),
                    _nt_dot(lhs, ka_ref[r]))
        sl = slice((unit - 1) * LANES, unit * LANES)
        lhs = jnp.concatenate(halves(qb_ref[r][:, sl]), axis=0)
        return (jnp.dot(lhs, cbk_ref[r, sl, :].astype(BF16), preferred_element_type=F32),
                _nt_dot(lhs, kb_ref[r, :, sl]))

    def finish(r, unit, s):
        s_c, s_n = s
        if unit == 0:
            cache_v, new_v = cav_ref[r].astype(BF16), va_ref[r]
            heads = [(h, sink_ref[h] * LOG2E, None) for h in range(H_A)]
        else:
            sl = slice((unit - 1) * LANES, unit * LANES)
            cache_v, new_v = cbv_ref[r, sl, :].astype(BF16), vb_ref[r, :, sl]
            heads = [(j, None, bias_ref[2 * (unit - 1) + j]) for j in range(2)]
        probs_c, probs_n, inv = [], [], []
        for j, sink, bias in heads:
            rs = slice(j * t, (j + 1) * t)
            sc, sn = s_c[rs], s_n[rs]
            if bias is not None:
                sc, sn = sc + bias[:, :wb], sn + bias[:, wb:wb + t]
            e_c, e_n, iv = _softmax2(sc, sn, sink)
            probs_c.append(e_c)
            probs_n.append(e_n)
            inv.append(iv)
        o = (_nt_dot(jnp.concatenate(probs_c, axis=0), cache_v)
             + jnp.dot(jnp.concatenate(probs_n, axis=0), new_v, preferred_element_type=F32))
        tiles = []
        for p in range(len(heads) // 2):
            o0 = o[(2 * p) * t:(2 * p + 1) * t] * inv[2 * p]
            o1 = o[(2 * p + 1) * t:(2 * p + 2) * t] * inv[2 * p + 1]
            tiles.append(jnp.where(lo, o0, o1))
        return tiles

    order = [(r, u) for r in range(nreq) for u in range(n_pairs + 1)]
    pending = {order[0]: scores(*order[0])}
    mix, tiles = [], []
    for i, (r, u) in enumerate(order):
        if i + 1 < len(order):
            pending[order[i + 1]] = scores(*order[i + 1])
        tiles += finish(r, u, pending.pop((r, u)))
        if u == n_pairs:
            gate = jnp.concatenate([ga_ref[r], gb_ref[r]], axis=1).astype(F32)
            mix.append((jnp.concatenate(tiles, axis=1) * gate).astype(BF16))
            tiles = []
    m = jnp.dot(jnp.concatenate(mix, axis=0), w_ref[...], preferred_element_type=F32)
    y = x_ref[...].reshape(nreq * t, D_MODEL) + _rms(m, g_ref[...])
    y_ref[...] = y.reshape(nreq, t, D_MODEL)


def _attn_sample(sinks, x, qa, ga, qb, gb, ka, va, kb, vb, cak, cav, cbk, cbv, bias, w_out, g_post):
    b = x.shape[0]
    per = lambda a: pl.BlockSpec((SAMPLE_REQS,) + a.shape[1:], lambda i: (i, 0, 0))
    fixed = lambda shape: pl.BlockSpec(shape, lambda i: (0,) * len(shape))
    arrs = (x, qa, ga, qb, gb, ka, va, kb, vb, cak, cav, cbk, cbv)
    return pl.pallas_call(
        _attn_sample_body,
        grid=(b // SAMPLE_REQS,),
        in_specs=[pl.BlockSpec(memory_space=pltpu.SMEM)] + [per(a) for a in arrs]
                 + [fixed(bias.shape), fixed(w_out.shape), fixed((1, D_MODEL))],
        out_specs=[per(a) for a in (x, cak, cav, cbk, cbv)],
        out_shape=[jax.ShapeDtypeStruct(a.shape, F32) for a in (x, cak, cav, cbk, cbv)],
        compiler_params=_params(1),
        name="attn_sample",
    )(sinks, *arrs, bias, w_out, g_post)


LRU_SEG = 8
LRU_STEPS = 32
LRU_TILE = LRU_SEG * LRU_STEPS
N_TAIL = CONV_W - 1
LRU_TILES_PER_STEP = 4
LRU_CB = 512
N_CB = D_LRU // LRU_CB


def _lru_block(xb, z, tail, carry_in, cb, chained, w, after_gates):
    cw_ref, cb_ref, wa_ref, ba_ref, wx_ref, bx_ref, lam_ref = w[4:11]
    cs = slice(cb * LRU_CB, (cb + 1) * LRU_CB)
    grp = lambda v, g, n=1: v[g * LRU_SEG:(g + n) * LRU_SEG]

    if chained:
        sub = lax.broadcasted_iota(jnp.int32, (LRU_SEG, LRU_CB), 0)
        before = [jnp.where(sub == 0, pltpu.roll(grp(tail, j), 1, 0),
                            pltpu.roll(grp(xb, LRU_STEPS - N_TAIL + j), 1, 0))
                  for j in range(N_TAIL)]
    else:
        before = tail
    xb_tail = grp(xb, LRU_STEPS - N_TAIL, N_TAIL)

    cw = cw_ref[:, cs]
    xc = cb_ref[:, cs] + xb * cw[CONV_W - 1:CONV_W]
    for k in range(1, CONV_W):
        shifted = jnp.concatenate(before[N_TAIL - k:] + [grp(xb, 0, LRU_STEPS - k)], axis=0)
        xc = xc + shifted * cw[CONV_W - 1 - k:CONV_W - k]

    xcb = xc.astype(BF16)
    per = LRU_CB // MXU_DIM

    both = [jnp.dot(xcb[:, j * MXU_DIM:(j + 1) * MXU_DIM],
                    jnp.concatenate([wa_ref[cb * per + j], wx_ref[cb * per + j]], axis=1),
                    preferred_element_type=F32) for j in range(per)]
    pre_r = jnp.concatenate([b[:, :MXU_DIM] for b in both], axis=1)
    pre_i = jnp.concatenate([b[:, MXU_DIM:] for b in both], axis=1)
    after_gates()
    r = jax.nn.sigmoid(pre_r + ba_ref[:, cs])
    gi = jax.nn.sigmoid(pre_i + bx_ref[:, cs])
    lam = lam_ref[:, cs]
    log_sig = jnp.minimum(lam, 0.0) - jnp.log1p(jnp.exp(-jnp.abs(lam)))
    log_a = r * (C_GATE * log_sig)
    a = jnp.exp(log_a)
    u = jnp.sqrt(jnp.tanh(-log_a) * (a * a + 1.0)) * (gi * xc)

    hc, ac = grp(u, 0), grp(a, 0)
    h_loc, a_loc = [hc], [ac]
    for g in range(1, LRU_STEPS):
        ag = grp(a, g)
        hc = ag * hc + grp(u, g)
        ac = ag * ac
        h_loc.append(hc)
        a_loc.append(ac)

    if chained:
        c = carry_in
        rows = []
        for s in range(LRU_SEG):
            rows.append(c)
            c = ac[s:s + 1] * c + hc[s:s + 1]
        carry = jnp.concatenate(rows, axis=0)
    else:
        c = None
        carry = carry_in
    h_groups = [hl + al * carry for hl, al in zip(h_loc, a_loc)]
    yl = (jnp.concatenate(h_groups, axis=0) * _silu(z)).astype(BF16)
    return yl, c, xb_tail, h_groups[-1]


def _lru_run(load_x, store_y, n_tiles, tails, carries, chained, w):
    p_ref, pt_ref, gpre_ref, win_ref = w[:4]
    wout_ref, gpost_ref = w[11:13]
    hp = {}

    def in_proj(unit):
        ti, cb = divmod(unit, N_CB)
        if ti not in hp:
            h = _rms(load_x(ti), gpre_ref[...]).astype(BF16)
            hp[ti] = jnp.dot(p_ref[...], h, preferred_element_type=F32).astype(BF16)
        lo = cb * LRU_CB
        xb = jnp.dot(hp[ti], win_ref[:, lo:lo + LRU_CB], preferred_element_type=F32)
        z = jnp.dot(hp[ti], win_ref[:, D_LRU + lo:D_LRU + lo + LRU_CB], preferred_element_type=F32)
        return xb, z

    units = n_tiles * N_CB
    ready = {0: in_proj(0)}
    h_last = [None] * N_CB
    m = None
    for unit in range(units):
        def emit_next(unit=unit):
            if unit + 1 < units:
                ready[unit + 1] = in_proj(unit + 1)

        ti, cb = divmod(unit, N_CB)
        xb, z = ready.pop(unit)
        yl, carries[cb], tails[cb], h_last[cb] = _lru_block(xb, z, tails[cb], carries[cb], cb,
                                                            chained, w, emit_next)
        yl = jnp.dot(pt_ref[...], yl, preferred_element_type=F32).astype(BF16)
        part = jnp.dot(yl, wout_ref[cb * LRU_CB:(cb + 1) * LRU_CB, :], preferred_element_type=F32)
        m = part if cb == 0 else m + part
        if cb == N_CB - 1:
            store_y(ti, load_x(ti) + _rms(m, gpost_ref[...]))
    return h_last


def _lru_body(x_ref, h0_ref, c0_ref, *rest, chained):
    w, (y_ref, hl_ref, cl_ref, h_s, tail_s) = rest[:-5], rest[-5:]
    t = pl.program_id(1)
    n_tiles = x_ref.shape[1] // LRU_TILE
    blocks = [slice(cb * LRU_CB, (cb + 1) * LRU_CB) for cb in range(N_CB)]
    rows = lambda ti: slice(ti * LRU_TILE, (ti + 1) * LRU_TILE)
    load_x = lambda ti: x_ref[0, rows(ti), :]

    def store_y(ti, y):
        y_ref[0, rows(ti), :] = y

    if not chained:
        tails = [[c0_ref[j, :, cs] for j in range(N_TAIL)] for cs in blocks]
        carries = [h0_ref[0, :, cs] for cs in blocks]
        h_last = _lru_run(load_x, store_y, n_tiles, tails, carries, False, w)
        hl_ref[0] = jnp.concatenate(h_last, axis=1)
        cl_ref[...] = jnp.concatenate(tails, axis=1).reshape(N_TAIL, LRU_SEG, D_LRU)
        return

    @pl.when(t == 0)
    def _():
        h_s[...] = h0_ref[0]
        tail_s[...] = jnp.zeros(tail_s.shape, F32)
        for j in range(N_TAIL):
            tail_s[j * LRU_SEG + LRU_SEG - 1:(j + 1) * LRU_SEG, :] = c0_ref[0, j:j + 1, :]

    tails = [tail_s[:, cs] for cs in blocks]
    carries = [h_s[:, cs] for cs in blocks]
    _lru_run(load_x, store_y, n_tiles, tails, carries, True, w)
    c = jnp.concatenate(carries, axis=1)
    tail = jnp.concatenate(tails, axis=1)
    h_s[...] = c
    tail_s[...] = tail

    @pl.when(t == pl.num_programs(1) - 1)
    def _():
        hl_ref[0] = c
        last = LRU_SEG - 1
        cl_ref[0] = jnp.concatenate(
            [tail[j * LRU_SEG + last:(j + 1) * LRU_SEG] for j in range(N_TAIL)], axis=0)


def _lru(x, h0, c0, weights, chained):
    nb, seq, _ = x.shape
    fixed = lambda a: pl.BlockSpec(a.shape, lambda i, j: (0,) * a.ndim)
    perm = np.zeros((LRU_TILE, LRU_TILE), np.float32)
    rho = np.arange(LRU_TILE)
    perm[rho, (rho % LRU_SEG) * LRU_STEPS + rho // LRU_SEG] = 1.0
    perm_mats = (jnp.asarray(perm, BF16), jnp.asarray(perm.T, BF16))
    if chained:
        state_specs = [pl.BlockSpec((1, 1, D_LRU), lambda i, j: (i, 0, 0)),
                       pl.BlockSpec((1, N_TAIL, D_LRU), lambda i, j: (i, 0, 0))]
        state_shapes = [jax.ShapeDtypeStruct((nb, 1, D_LRU), F32),
                        jax.ShapeDtypeStruct((nb, N_TAIL, D_LRU), F32)]
    else:
        assert seq == LRU_TILE
        state_specs = [pl.BlockSpec((1, LRU_SEG, D_LRU), lambda i, j: (i, 0, 0)),
                       pl.BlockSpec((N_TAIL, LRU_SEG, D_LRU), lambda i, j: (0, i, 0))]
        state_shapes = [jax.ShapeDtypeStruct((nb, LRU_SEG, D_LRU), F32),
                        jax.ShapeDtypeStruct((N_TAIL, nb * LRU_SEG, D_LRU), F32)]
    rows = LRU_TILE * (LRU_TILES_PER_STEP if chained else 1)
    tile = pl.BlockSpec((1, rows, D_MODEL), lambda i, j: (i, j, 0))
    return pl.pallas_call(
        functools.partial(_lru_body, chained=chained),
        grid=(nb, seq // rows),
        in_specs=[tile] + state_specs + [fixed(a) for a in perm_mats + weights],
        out_specs=[tile] + state_specs,
        out_shape=[jax.ShapeDtypeStruct(x.shape, F32)] + state_shapes,
        scratch_shapes=[pltpu.VMEM((1, D_LRU), F32), pltpu.VMEM((N_TAIL * LRU_SEG, D_LRU), F32)],
        compiler_params=_params(2),
        name="lru_chained" if chained else "lru_batched",
    )(x, h0, c0, *perm_mats, *weights)


def _rope_tables(pos):
    half = HEAD_DIM // 2
    inv = ROPE_THETA ** (-jnp.arange(half, dtype=F32) / half)
    ang = pos.astype(F32)[:, None] * inv[None, :]
    cos = jnp.tile(jnp.cos(ang), (1, LANES // half))
    sin = jnp.sin(ang)
    sin = jnp.tile(jnp.concatenate([-sin, sin], axis=1), (1, LANES // HEAD_DIM))
    return cos, sin


def _interleave_groups(w, axis, unit=HEAD_DIM):
    shape = w.shape
    per = H_A // KV_A
    w = w.reshape(shape[:axis] + (KV_A, per, unit) + shape[axis + 1:])
    w = jnp.swapaxes(w, axis, axis + 1)
    return w.reshape(shape)


def _block_diag(w):
    per = MXU_DIM // BLOCK
    w = w.reshape(N_BLOCKS // per, per, BLOCK, BLOCK)
    eye = jnp.eye(per, dtype=w.dtype)
    w = w[:, :, :, None, :] * eye[None, :, None, :, None]
    return w.reshape(N_BLOCKS // per, MXU_DIM, MXU_DIM)


def kernel(x_prompt, x_sample, cache_a_k, cache_a_v, cache_b_k, cache_b_v, state_c_h, state_c_conv,
           ln_pre, ln_post, w_in_ab, sinks_a, relpos_b, w_out_ab, w_in_c, conv_c_w, conv_c_b,
           gate_c_wa, gate_c_ba, gate_c_wx, gate_c_bx, lambda_c, w_out_c):
    bp, s_len, _ = x_prompt.shape
    bs, t_s, _ = x_sample.shape
    wa_rows = cache_a_k.shape[2]
    wb_rows = cache_b_k.shape[2]
    assert wa_rows + t_s <= WIN_A and wb_rows + t_s <= WIN_B and wb_rows == PAD_ROWS

    w_in = w_in_ab[0]
    w_in = jnp.concatenate([_interleave_groups(w_in[:, :QA_W], 1), w_in[:, QA_W:QA_W + 2 * KA_W],
                            _interleave_groups(w_in[:, QA_W + 2 * KA_W:2 * QA_W + 2 * KA_W], 1),
                            w_in[:, 2 * QA_W + 2 * KA_W:]], axis=1).astype(BF16)
    w_out = jnp.concatenate([_interleave_groups(w_out_ab[0, :QA_W], 0), w_out_ab[0, QA_W:]],
                            axis=0).astype(BF16)
    sinks = _interleave_groups(sinks_a[0], 0, unit=1)
    g_pre0 = ln_pre[0].reshape(1, D_MODEL)
    g_post0 = ln_post[0].reshape(1, D_MODEL)
    bias_p, bias_s, band = _build_bias(relpos_b[0], t_s, wb_rows + t_s)

    cos_p, sin_p = _rope_tables(jnp.arange(s_len, dtype=jnp.int32))
    assert wb_rows == PROJ_TM and wa_rows <= PROJ_TM and s_len % PROJ_TM == 0
    proj = _proj0(x_prompt.reshape(bp * s_len, D_MODEL), g_pre0, w_in, cos_p, sin_p, PROJ_TM,
                  seq=s_len, wa_rows=wa_rows)
    qa, ka, va, ga, qb, kb, vb, gb = [a.reshape(bp, s_len, a.shape[-1]) for a in proj[:8]]
    y0_p = _attn_prompt(sinks, x_prompt, qa, ga, qb, gb, ka, va, kb, vb, bias_p, band, w_out,
                        g_post0, ATT_TQ)
    tmaj = lambda c, heads: jnp.transpose(
        c.reshape(c.shape[0], heads, HEAD_DIM, c.shape[2]), (0, 3, 1, 2))[None]
    nak_p, nav_p = tmaj(proj[8], KV_A), tmaj(proj[9], KV_A)
    nbk_p, nbv_p = tmaj(proj[10], H_B), tmaj(proj[11], H_B)

    n_s = bs * t_s
    tm_s = min(PROJ_TM, n_s)
    cos_s, sin_s = _rope_tables(PAST_LEN + jnp.arange(t_s, dtype=jnp.int32))
    cos_s = jnp.tile(cos_s, (tm_s // t_s, 1))
    sin_s = jnp.tile(sin_s, (tm_s // t_s, 1))
    proj = _proj0(x_sample.reshape(n_s, D_MODEL), g_pre0, w_in, cos_s, sin_s, tm_s)
    qa, ka, va, ga, qb, kb, vb, gb = [a.reshape(bs, t_s, a.shape[-1]) for a in proj]
    fmaj = lambda c: jnp.transpose(c[0], (0, 2, 3, 1))
    cak, cav, cbk, cbv = fmaj(cache_a_k), fmaj(cache_a_v), fmaj(cache_b_k), fmaj(cache_b_v)
    flat = lambda c: c.reshape(bs, c.shape[1] * HEAD_DIM, c.shape[3])
    y0_s, nak_s, nav_s, nbk_s, nbv_s = _attn_sample(
        sinks, x_sample, qa, ga, qb, gb, ka, va, kb, vb, flat(cak), flat(cav), flat(cbk), flat(cbv),
        bias_s, w_out, g_post0)
    nak_s, nav_s = tmaj(nak_s, KV_A), tmaj(nav_s, KV_A)
    nbk_s, nbv_s = tmaj(nbk_s, H_B), tmaj(nbv_s, H_B)

    row = lambda v: v.reshape(1, -1)
    lru_w = (row(ln_pre[1]), w_in_c[0].astype(BF16), conv_c_w[0], row(conv_c_b[0]),
             _block_diag(gate_c_wa[0]).astype(BF16), row(gate_c_ba[0]),
             _block_diag(gate_c_wx[0]).astype(BF16), row(gate_c_bx[0]),
             row(lambda_c[0]), w_out_c[0].astype(BF16), row(ln_post[1]))
    y1_p, hl_p, cl_p = _lru(y0_p, jnp.zeros((bp, 1, D_LRU), F32), jnp.zeros((bp, N_TAIL, D_LRU), F32),
                            lru_w, chained=True)
    assert t_s == LRU_STEPS and bs % LRU_SEG == 0
    y1_s, hl_s, cl_s = _lru(y0_s.reshape(bs // LRU_SEG, LRU_TILE, D_MODEL),
                            state_c_h[0].reshape(bs // LRU_SEG, LRU_SEG, D_LRU),
                            jnp.transpose(state_c_conv[0], (1, 0, 2)), lru_w, chained=False)
    y1_s = y1_s.reshape(bs, t_s, D_MODEL)
    cl_s = jnp.transpose(cl_s, (1, 0, 2))

    return (y1_p, y1_s, nak_p, nav_p, nbk_p, nbv_p,
            hl_p.reshape(1, bp, D_LRU), cl_p.reshape(1, bp, N_TAIL, D_LRU),
            nak_s, nav_s, nbk_s, nbv_s,
            hl_s.reshape(1, bs, D_LRU), cl_s.reshape(1, bs, N_TAIL, D_LRU))
```

```python
import functools

import jax
import jax.numpy as jnp
from jax import lax
from jax.experimental import pallas as pl
from jax.experimental.pallas import tpu as pltpu

F32 = jnp.float32
BF16 = jnp.bfloat16

D_MODEL = 1024
CHUNK = 64
HEAD_DIM = 64
H_A = 8
KV_A = 2
N_PREV_A = 2
H_B = 8
N_PREV_B = 8
MAX_REL = 128
ROPE_THETA = 10000.0
D_LRU = D_MODEL
N_BLOCKS = 16
BLOCK = D_LRU // N_BLOCKS
CONV_W = 4
C_GATE = 8.0
EPS = 1e-6
PAST_LEN = 1024
NEG = -1e30
LOG2E = 1.4426950408889634
Q_SCALE = HEAD_DIM ** -0.5 * LOG2E

LANES = 128
MXU_DIM = 256
VMEM_LIMIT = 56 * 1024 * 1024

QA_W = H_A * HEAD_DIM
KA_W = KV_A * HEAD_DIM
B_W = H_B * HEAD_DIM
MIX_W = QA_W + B_W
GRP = 2 * CHUNK
WIN_A = (N_PREV_A + 2) * CHUNK
WIN_B = (N_PREV_B + 2) * CHUNK
PAD_ROWS = N_PREV_B * CHUNK
TOEP_W = WIN_B + LANES

PROJ_TM = 512
ATT_TQ = 512


def _params(n_axes):
    return pltpu.CompilerParams(
        dimension_semantics=("arbitrary",) * n_axes,
        vmem_limit_bytes=VMEM_LIMIT)


def _rms(x, g):
    ms = jnp.mean(x * x, axis=-1, keepdims=True)
    return x * lax.rsqrt(ms + EPS) * g


def _silu(x):
    return x * jax.nn.sigmoid(x)


def _proj0_body(x_ref, g_ref, w_ref, cos_ref, sin_ref,
                qa_ref, ka_ref, va_ref, ga_ref, qb_ref, kb_ref, vb_ref, gb_ref, *cache_refs,
                tiles_per_seq, wa_rows):
    h = _rms(x_ref[...], g_ref[...]).astype(BF16)
    cos = cos_ref[...]
    sin = sin_ref[...]
    tm = cos.shape[0]
    lane = lax.broadcasted_iota(jnp.int32, (tm, LANES), 1)
    lower = (lane & (HEAD_DIM - 1)) < (HEAD_DIM // 2)

    def mm(lo, hi):
        return jnp.dot(h, w_ref[:, lo:hi], preferred_element_type=F32)

    def rope(t):
        partner = jnp.where(lower, pltpu.roll(t, LANES - HEAD_DIM // 2, 1),
                            pltpu.roll(t, HEAD_DIM // 2, 1))
        return t * cos + partner * sin

    o = 0
    t = mm(o, o + QA_W)
    for k in range(QA_W // LANES):
        sl = slice(k * LANES, (k + 1) * LANES)
        qa_ref[:, sl] = (rope(t[:, sl]) * Q_SCALE).astype(BF16)
    o += QA_W
    ka = rope(mm(o, o + KA_W))
    ka_ref[...] = ka.astype(BF16)
    o += KA_W
    va = mm(o, o + KA_W)
    va_ref[...] = va.astype(BF16)
    o += KA_W
    ga_ref[...] = _silu(mm(o, o + QA_W)).astype(BF16)
    o += QA_W
    qb_ref[...] = (mm(o, o + B_W) * Q_SCALE).astype(BF16)
    o += B_W
    kb = mm(o, o + B_W)
    kb_ref[...] = kb.astype(BF16)
    o += B_W
    vb = mm(o, o + B_W)
    vb_ref[...] = vb.astype(BF16)
    o += B_W
    gb_ref[...] = _silu(mm(o, o + B_W)).astype(BF16)

    if cache_refs:
        @pl.when(pl.program_id(0) % tiles_per_seq == tiles_per_seq - 1)
        def _():
            kat_ref, vat_ref, kbt_ref, vbt_ref = cache_refs
            kat_ref[0] = ka[tm - wa_rows:].T
            vat_ref[0] = va[tm - wa_rows:].T
            kbt_ref[0] = kb.T
            vbt_ref[0] = vb.T


def _proj0(x2d, g, w, cos, sin, tm, seq=None, wa_rows=None):
    n = x2d.shape[0]
    n_pos_blocks = cos.shape[0] // tm
    widths = (QA_W, KA_W, KA_W, QA_W, B_W, B_W, B_W, B_W)
    row = lambda i: (i, 0)
    fixed = lambda i: (0, 0)
    pos = lambda i: (i % n_pos_blocks, 0)
    out_specs = [pl.BlockSpec((tm, wd), row) for wd in widths]
    out_shape = [jax.ShapeDtypeStruct((n, wd), BF16) for wd in widths]
    tiles_per_seq = None
    if seq is not None:
        tiles_per_seq = seq // tm
        for feat, keys in ((KA_W, wa_rows), (KA_W, wa_rows), (B_W, tm), (B_W, tm)):
            out_specs.append(pl.BlockSpec((1, feat, keys), lambda i: (i // tiles_per_seq, 0, 0)))
            out_shape.append(jax.ShapeDtypeStruct((n // seq, feat, keys), F32))
    return pl.pallas_call(
        functools.partial(_proj0_body, tiles_per_seq=tiles_per_seq, wa_rows=wa_rows),
        grid=(n // tm,),
        in_specs=[pl.BlockSpec((tm, D_MODEL), row),
                  pl.BlockSpec((1, D_MODEL), fixed),
                  pl.BlockSpec(w.shape, fixed),
                  pl.BlockSpec((tm, LANES), pos),
                  pl.BlockSpec((tm, LANES), pos)],
        out_specs=out_specs,
        out_shape=out_shape,
        compiler_params=_params(1),
        name="proj0",
    )(x2d, g, w, cos, sin)


def _bias_body(r0_ref, bp_ref, bs_ref, band_ref, *, t_sample, keys_sample):
    x = jnp.broadcast_to(r0_ref[0], (GRP, TOEP_W)) * LOG2E
    row = lax.broadcasted_iota(jnp.int32, (GRP, TOEP_W), 0)
    for b in range(GRP.bit_length() - 1):
        x = jnp.where(((row >> b) & 1) == 1, pltpu.roll(x, 1 << b, 1), x)
    x = x[:, :WIN_B]
    r = lax.broadcasted_iota(jnp.int32, (GRP, WIN_B), 0)
    c = lax.broadcasted_iota(jnp.int32, (GRP, WIN_B), 1)
    ci = r // CHUNK
    cj = c // CHUNK
    bp_ref[0] = jnp.where((cj >= ci) & (cj <= ci + N_PREV_B), x, NEG)
    cs = lax.broadcasted_iota(jnp.int32, (t_sample, WIN_B), 1)
    bs_ref[0] = jnp.where(cs < keys_sample, x[:t_sample], NEG)
    ra = lax.broadcasted_iota(jnp.int32, (GRP, WIN_A), 0) // CHUNK
    ca = lax.broadcasted_iota(jnp.int32, (GRP, WIN_A), 1) // CHUNK
    band_ref[...] = jnp.where((ca >= ra) & (ca <= ra + N_PREV_A), 0.0, NEG)


def _build_bias(relpos, t_sample, keys_sample):
    far = relpos[:, 2 * MAX_REL:]
    n_far = PAD_ROWS - MAX_REL + 1
    r0 = jnp.concatenate(
        [jnp.broadcast_to(far, (H_B, n_far)),
         relpos[:, 2 * MAX_REL - 1:0:-1],
         jnp.broadcast_to(far, (H_B, TOEP_W - n_far - (2 * MAX_REL - 1)))], axis=1)
    r0 = r0.reshape(H_B, 1, TOEP_W)
    return pl.pallas_call(
        functools.partial(_bias_body, t_sample=t_sample, keys_sample=keys_sample),
        grid=(H_B,),
        in_specs=[pl.BlockSpec((1, 1, TOEP_W), lambda h: (h, 0, 0))],
        out_specs=[pl.BlockSpec((1, GRP, WIN_B), lambda h: (h, 0, 0)),
                   pl.BlockSpec((1, t_sample, WIN_B), lambda h: (h, 0, 0)),
                   pl.BlockSpec((GRP, WIN_A), lambda h: (0, 0))],
        out_shape=[jax.ShapeDtypeStruct((H_B, GRP, WIN_B), F32),
                   jax.ShapeDtypeStruct((H_B, t_sample, WIN_B), F32),
                   jax.ShapeDtypeStruct((GRP, WIN_A), F32)],
        compiler_params=_params(1),
        name="relbias",
    )(r0)


def _nt_dot(a, b):
    return lax.dot_general(a, b, (((1,), (1,)), ((), ())), preferred_element_type=F32)


def _attend_group(qa, qb, ka, va, kb_ref, vb_ref, win_b, sink_ref, bias_ref, mask_a, col_ok_a,
                  col_ok_b):
    n = qa.shape[0]
    lane = lax.broadcasted_iota(jnp.int32, (n, LANES), 1)
    lo = lane < HEAD_DIM
    zero = jnp.zeros((n, LANES), BF16)
    halves = lambda tile: [jnp.where(lo, tile, zero), jnp.where(lo, zero, tile)]
    n_pairs = B_W // LANES

    def scores(unit):
        if unit == 0:
            lhs = jnp.concatenate([h for p in range(QA_W // LANES)
                                   for h in halves(qa[:, p * LANES:(p + 1) * LANES])], axis=0)
            return _nt_dot(lhs, ka)
        sl = slice((unit - 1) * LANES, unit * LANES)
        return _nt_dot(jnp.concatenate(halves(qb[:, sl]), axis=0), kb_ref[win_b, sl])

    def finish_a(s_all):
        probs, inv = [], []
        for h in range(H_A):
            s = s_all[h * n:(h + 1) * n]
            if mask_a is not None:
                s = s + mask_a
            if col_ok_a is not None:
                s = jnp.where(col_ok_a, s, NEG)
            sink = sink_ref[h] * LOG2E
            m = jnp.maximum(jnp.max(s, axis=-1, keepdims=True), sink)
            e = jnp.exp2(s - m)
            den = jnp.sum(e, axis=-1, keepdims=True) + jnp.exp2(sink - m)
            probs.append(e.astype(BF16))
            inv.append(1.0 / den)
        r_all = jnp.dot(jnp.concatenate(probs, axis=0), va, preferred_element_type=F32)
        out = []
        for p in range(QA_W // LANES):
            r0 = r_all[(2 * p) * n:(2 * p + 1) * n] * inv[2 * p]
            r1 = r_all[(2 * p + 1) * n:(2 * p + 2) * n] * inv[2 * p + 1]
            out.append(jnp.where(lo, r0, r1))
        return jnp.concatenate(out, axis=1)

    def finish_b(p, s2):
        sl = slice(p * LANES, (p + 1) * LANES)
        probs, inv = [], []
        for j in range(2):
            s = s2[j * n:(j + 1) * n] + bias_ref[2 * p + j]
            if col_ok_b is not None:
                s = jnp.where(col_ok_b, s, NEG)
            m = jnp.max(s, axis=-1, keepdims=True)
            e = jnp.exp2(s - m)
            probs.append(e.astype(BF16))
            inv.append(1.0 / jnp.sum(e, axis=-1, keepdims=True))
        r = jnp.dot(jnp.concatenate(probs, axis=0), vb_ref[win_b, sl], preferred_element_type=F32)
        return jnp.where(lo, r[:n] * inv[0], r[n:] * inv[1])

    s_cur = scores(0)
    o_a, o_b = None, []
    for unit in range(n_pairs + 1):
        s_next = scores(unit + 1) if unit < n_pairs else None
        if unit == 0:
            o_a = finish_a(s_cur)
        else:
            o_b.append(finish_b(unit - 1, s_cur))
        s_cur = s_next
    return o_a, jnp.concatenate(o_b, axis=1)


def _mix_out(o_a, o_b, ga, gb, x, w_ref, g_ref):
    mix = jnp.concatenate([o_a * ga.astype(F32), o_b * gb.astype(F32)], axis=1).astype(BF16)
    m = jnp.dot(mix, w_ref[...], preferred_element_type=F32)
    return x + _rms(m, g_ref[...])


def _attn_prompt_body(sink_ref, x_ref, qa_ref, ga_ref, qb_ref, gb_ref, ka_ref, va_ref, kb_ref,
                      vb_ref, bias_ref, band_ref, w_ref, g_ref, y_ref,
                      kap, vap, kbp, vbp, *, seq):
    t = pl.program_id(1)
    tq = x_ref.shape[1]

    @pl.when(t == 0)
    def _():
        for dst, src in ((kap, ka_ref), (vap, va_ref), (kbp, kb_ref), (vbp, vb_ref)):
            dst[:PAD_ROWS, :] = jnp.zeros((PAD_ROWS, dst.shape[1]), BF16)
            dst[PAD_ROWS:, :] = src[0]

    band = band_ref[...]

    def group(sub, masked):
        start = pl.multiple_of(t * tq + sub * GRP, GRP)
        rs = slice(sub * GRP, (sub + 1) * GRP)
        win_a = pl.ds(pl.multiple_of(start + (PAD_ROWS - N_PREV_A * CHUNK), GRP), WIN_A)
        win_b = pl.ds(start, WIN_B)
        ok_a = ok_b = None
        if masked:
            ok_a = lax.broadcasted_iota(jnp.int32, (1, WIN_A), 1) >= N_PREV_A * CHUNK - start
            ok_b = lax.broadcasted_iota(jnp.int32, (1, WIN_B), 1) >= N_PREV_B * CHUNK - start
        o_a, o_b = _attend_group(qa_ref[0, rs, :], qb_ref[0, rs, :], kap[win_a, :], vap[win_a, :],
                                 kbp, vbp, win_b, sink_ref, bias_ref, band, ok_a, ok_b)
        y_ref[0, rs, :] = _mix_out(o_a, o_b, ga_ref[0, rs, :], gb_ref[0, rs, :], x_ref[0, rs, :],
                                   w_ref, g_ref)

    n_masked = PAD_ROWS // tq

    @pl.when(t < n_masked)
    def _():
        for sub in range(tq // GRP):
            group(sub, True)

    @pl.when(t >= n_masked)
    def _():
        for sub in range(tq // GRP):
            group(sub, False)


def _attn_prompt(sinks, x, qa, ga, qb, gb, ka, va, kb, vb, bias, band, w_out, g_post, tq):
    b, seq, _ = x.shape
    tile = lambda w: pl.BlockSpec((1, tq, w), lambda i, j: (i, j, 0))
    whole = lambda w: pl.BlockSpec((1, seq, w), lambda i, j: (i, 0, 0))
    fixed = lambda shape: pl.BlockSpec(shape, lambda i, j: (0,) * len(shape))
    return pl.pallas_call(
        functools.partial(_attn_prompt_body, seq=seq),
        grid=(b, seq // tq),
        in_specs=[pl.BlockSpec(memory_space=pltpu.SMEM),
                  tile(D_MODEL), tile(QA_W), tile(QA_W), tile(B_W), tile(B_W),
                  whole(KA_W), whole(KA_W), whole(B_W), whole(B_W),
                  fixed(bias.shape), fixed(band.shape), fixed(w_out.shape), fixed((1, D_MODEL))],
        out_specs=tile(D_MODEL),
        out_shape=jax.ShapeDtypeStruct(x.shape, F32),
        scratch_shapes=[pltpu.VMEM((PAD_ROWS + seq, KA_W), BF16),
                        pltpu.VMEM((PAD_ROWS + seq, KA_W), BF16),
                        pltpu.VMEM((PAD_ROWS + seq, B_W), BF16),
                        pltpu.VMEM((PAD_ROWS + seq, B_W), BF16)],
        compiler_params=_params(2),
        name="attn_prompt",
    )(sinks, x, qa, ga, qb, gb, ka, va, kb, vb, bias, band, w_out, g_post)


SAMPLE_REQS = 4


def _softmax2(s_c, s_n, sink):
    m = jnp.maximum(jnp.max(s_c, axis=-1, keepdims=True), jnp.max(s_n, axis=-1, keepdims=True))
    if sink is not None:
        m = jnp.maximum(m, sink)
    e_c = jnp.exp2(s_c - m)
    e_n = jnp.exp2(s_n - m)
    den = jnp.sum(e_c, axis=-1, keepdims=True) + jnp.sum(e_n, axis=-1, keepdims=True)
    if sink is not None:
        den = den + jnp.exp2(sink - m)
    return e_c.astype(BF16), e_n.astype(BF16), 1.0 / den


def _roll_in(cache, new):
    t, w = new.shape[0], cache.shape[1]
    lane = lax.broadcasted_iota(jnp.int32, (cache.shape[0], LANES), 1)
    padded = jnp.concatenate([jnp.zeros((LANES - t, new.shape[1]), F32), new.astype(F32)], axis=0)
    rolled = pltpu.roll(cache, w - t, 1)
    last = jnp.where(lane >= LANES - t, padded.T, rolled[:, w - LANES:])
    return last if w == LANES else jnp.concatenate([rolled[:, :w - LANES], last], axis=1)


def _attn_sample_body(sink_ref, x_ref, qa_ref, ga_ref, qb_ref, gb_ref, ka_ref, va_ref, kb_ref,
                      vb_ref, cak_ref, cav_ref, cbk_ref, cbv_ref, bias_ref, w_ref, g_ref,
                      y_ref, nak_ref, nav_ref, nbk_ref, nbv_ref):
    nreq, t, _ = x_ref.shape
    for r in range(nreq):
        for dst, cache, new in ((nak_ref, cak_ref, ka_ref), (nav_ref, cav_ref, va_ref),
                                (nbk_ref, cbk_ref, kb_ref), (nbv_ref, cbv_ref, vb_ref)):
            dst[r] = _roll_in(cache[r], new[r])

    wb = cbk_ref.shape[2]
    lane = lax.broadcasted_iota(jnp.int32, (t, LANES), 1)
    lo = lane < HEAD_DIM
    zero = jnp.zeros((t, LANES), BF16)
    halves = lambda tile: [jnp.where(lo, tile, zero), jnp.where(lo, zero, tile)]
    n_pairs = B_W // LANES

    def scores(r, unit):
        if unit == 0:
            qa = qa_ref[r]
            lhs = jnp.concatenate([h for p in range(QA_W // LANES)
                                   for h in halves(qa[:, p * LANES:(p + 1) * LANES])], axis=0)
            return (jnp.dot(lhs, cak_ref[r].astype(BF16), preferred_element_type=F32),
                    _nt_dot(lhs, ka_ref[r]))
        sl = slice((unit - 1) * LANES, unit * LANES)
        lhs = jnp.concatenate(halves(qb_ref[r][:, sl]), axis=0)
        return (jnp.dot(lhs, cbk_ref[r, sl, :].astype(BF16), preferred_element_type=F32),
                _nt_dot(lhs, kb_ref[r, :, sl]))

    def finish(r, unit, s):
        s_c, s_n = s
        if unit == 0:
            cache_v, new_v = cav_ref[r].astype(BF16), va_ref[r]
            heads = [(h, sink_ref[h] * LOG2E, None) for h in range(H_A)]
        else:
            sl = slice((unit - 1) * LANES, unit * LANES)
            cache_v, new_v = cbv_ref[r, sl, :].astype(BF16), vb_ref[r, :, sl]
            heads = [(j, None, bias_ref[2 * (unit - 1) + j]) for j in range(2)]
        probs_c, probs_n, inv = [], [], []
        for j, sink, bias in heads:
            rs = slice(j * t, (j + 1) * t)
            sc, sn = s_c[rs], s_n[rs]
            if bias is not None:
                sc, sn = sc + bias[:, :wb], sn + bias[:, wb:wb + t]
            e_c, e_n, iv = _softmax2(sc, sn, sink)
            probs_c.append(e_c)
            probs_n.append(e_n)
            inv.append(iv)
        o = (_nt_dot(jnp.concatenate(probs_c, axis=0), cache_v)
             + jnp.dot(jnp.concatenate(probs_n, axis=0), new_v, preferred_element_type=F32))
        tiles = []
        for p in range(len(heads) // 2):
            o0 = o[(2 * p) * t:(2 * p + 1) * t] * inv[2 * p]
            o1 = o[(2 * p + 1) * t:(2 * p + 2) * t] * inv[2 * p + 1]
            tiles.append(jnp.where(lo, o0, o1))
        return tiles

    order = [(r, u) for r in range(nreq) for u in range(n_pairs + 1)]
    pending = {order[0]: scores(*order[0])}
    mix, tiles = [], []
    for i, (r, u) in enumerate(order):
        if i + 1 < len(order):
            pending[order[i + 1]] = scores(*order[i + 1])
        tiles += finish(r, u, pending.pop((r, u)))
        if u == n_pairs:
            gate = jnp.concatenate([ga_ref[r], gb_ref[r]], axis=1).astype(F32)
            mix.append((jnp.concatenate(tiles, axis=1) * gate).astype(BF16))
            tiles = []
    m = jnp.dot(jnp.concatenate(mix, axis=0), w_ref[...], preferred_element_type=F32)
    y = x_ref[...].reshape(nreq * t, D_MODEL) + _rms(m, g_ref[...])
    y_ref[...] = y.reshape(nreq, t, D_MODEL)


def _attn_sample(sinks, x, qa, ga, qb, gb, ka, va, kb, vb, cak, cav, cbk, cbv, bias, w_out, g_post):
    b = x.shape[0]
    per = lambda a: pl.BlockSpec((SAMPLE_REQS,) + a.shape[1:], lambda i: (i, 0, 0))
    fixed = lambda shape: pl.BlockSpec(shape, lambda i: (0,) * len(shape))
    arrs = (x, qa, ga, qb, gb, ka, va, kb, vb, cak, cav, cbk, cbv)
    return pl.pallas_call(
        _attn_sample_body,
        grid=(b // SAMPLE_REQS,),
        in_specs=[pl.BlockSpec(memory_space=pltpu.SMEM)] + [per(a) for a in arrs]
                 + [fixed(bias.shape), fixed(w_out.shape), fixed((1, D_MODEL))],
        out_specs=[per(a) for a in (x, cak, cav, cbk, cbv)],
        out_shape=[jax.ShapeDtypeStruct(a.shape, F32) for a in (x, cak, cav, cbk, cbv)],
        compiler_params=_params(1),
        name="attn_sample",
    )(sinks, *arrs, bias, w_out, g_post)


LRU_SEG = 8
LRU_STEPS = 32
LRU_TILE = LRU_SEG * LRU_STEPS
N_TAIL = CONV_W - 1
LRU_TILES_PER_STEP = 4
LRU_CB = D_LRU
N_CB = D_LRU // LRU_CB


def _lru_block(xb, z, tail, carry_in, cb, chained, w, after_gates):
    cw_ref, cb_ref, wa_ref, ba_ref, wx_ref, bx_ref, lam_ref = w[2:9]
    cs = slice(cb * LRU_CB, (cb + 1) * LRU_CB)
    grp = lambda v, g, n=1: v[g * LRU_SEG:(g + n) * LRU_SEG]

    if chained:
        sub = lax.broadcasted_iota(jnp.int32, (LRU_SEG, LRU_CB), 0)
        before = [jnp.where(sub == 0, pltpu.roll(grp(tail, j), 1, 0),
                            pltpu.roll(grp(xb, LRU_STEPS - N_TAIL + j), 1, 0))
                  for j in range(N_TAIL)]
    else:
        before = tail
    xb_tail = grp(xb, LRU_STEPS - N_TAIL, N_TAIL)

    cw = cw_ref[:, cs]
    xc = cb_ref[:, cs] + xb * cw[CONV_W - 1:CONV_W]
    for k in range(1, CONV_W):
        shifted = jnp.concatenate(before[N_TAIL - k:] + [grp(xb, 0, LRU_STEPS - k)], axis=0)
        xc = xc + shifted * cw[CONV_W - 1 - k:CONV_W - k]

    xcb = xc.astype(BF16)
    per = LRU_CB // MXU_DIM

    both = [jnp.dot(xcb[:, j * MXU_DIM:(j + 1) * MXU_DIM],
                    jnp.concatenate([wa_ref[cb * per + j], wx_ref[cb * per + j]], axis=1),
                    preferred_element_type=F32) for j in range(per)]
    pre_r = jnp.concatenate([b[:, :MXU_DIM] for b in both], axis=1)
    pre_i = jnp.concatenate([b[:, MXU_DIM:] for b in both], axis=1)
    after_gates()
    r = jax.nn.sigmoid(pre_r + ba_ref[:, cs])
    gi = jax.nn.sigmoid(pre_i + bx_ref[:, cs])
    lam = lam_ref[:, cs]
    log_sig = jnp.minimum(lam, 0.0) - jnp.log1p(jnp.exp(-jnp.abs(lam)))
    log_a = r * (C_GATE * log_sig)
    a = jnp.exp(log_a)
    u = jnp.sqrt(jnp.tanh(-log_a) * (a * a + 1.0)) * (gi * xc)

    hc, ac = grp(u, 0), grp(a, 0)
    h_loc, a_loc = [hc], [ac]
    for g in range(1, LRU_STEPS):
        ag = grp(a, g)
        hc = ag * hc + grp(u, g)
        ac = ag * ac
        h_loc.append(hc)
        a_loc.append(ac)

    if chained:
        c = carry_in
        rows = []
        for s in range(LRU_SEG):
            rows.append(c)
            c = ac[s:s + 1] * c + hc[s:s + 1]
        carry = jnp.concatenate(rows, axis=0)
    else:
        c = None
        carry = carry_in
    h_groups = [hl + al * carry for hl, al in zip(h_loc, a_loc)]
    yl = jnp.concatenate(h_groups, axis=0) * _silu(z)
    yl = pltpu.einshape("gsd->sgd", yl.reshape(LRU_STEPS, LRU_SEG, LRU_CB))
    return yl.reshape(LRU_TILE, LRU_CB).astype(BF16), c, xb_tail, h_groups[-1]


def _lru_run(load_x, store_y, n_tiles, tails, carries, chained, w):
    gpre_ref, win_ref = w[:2]
    wout_ref, gpost_ref = w[9:11]
    hp = {}

    def in_proj(unit):
        ti, cb = divmod(unit, N_CB)
        if ti not in hp:
            h = _rms(load_x(ti), gpre_ref[...])
            h = pltpu.einshape("sgd->gsd", h.reshape(LRU_SEG, LRU_STEPS, D_MODEL))
            hp[ti] = h.reshape(LRU_TILE, D_MODEL).astype(BF16)
        lo = cb * LRU_CB
        xb = jnp.dot(hp[ti], win_ref[:, lo:lo + LRU_CB], preferred_element_type=F32)
        z = jnp.dot(hp[ti], win_ref[:, D_LRU + lo:D_LRU + lo + LRU_CB], preferred_element_type=F32)
        return xb, z

    units = n_tiles * N_CB
    ready = {0: in_proj(0)}
    h_last = [None] * N_CB
    m = None
    for unit in range(units):
        def emit_next(unit=unit):
            if unit + 1 < units:
                ready[unit + 1] = in_proj(unit + 1)

        ti, cb = divmod(unit, N_CB)
        xb, z = ready.pop(unit)
        yl, carries[cb], tails[cb], h_last[cb] = _lru_block(xb, z, tails[cb], carries[cb], cb,
                                                            chained, w, emit_next)
        part = jnp.dot(yl, wout_ref[cb * LRU_CB:(cb + 1) * LRU_CB, :], preferred_element_type=F32)
        m = part if cb == 0 else m + part
        if cb == N_CB - 1:
            store_y(ti, load_x(ti) + _rms(m, gpost_ref[...]))
    return h_last


def _lru_body(x_ref, h0_ref, c0_ref, *rest, chained):
    w, (y_ref, hl_ref, cl_ref, h_s, tail_s) = rest[:-5], rest[-5:]
    t = pl.program_id(1)
    n_tiles = x_ref.shape[1] // LRU_TILE
    blocks = [slice(cb * LRU_CB, (cb + 1) * LRU_CB) for cb in range(N_CB)]
    rows = lambda ti: slice(ti * LRU_TILE, (ti + 1) * LRU_TILE)
    load_x = lambda ti: x_ref[0, rows(ti), :]

    def store_y(ti, y):
        y_ref[0, rows(ti), :] = y

    if not chained:
        tails = [[c0_ref[j, :, cs] for j in range(N_TAIL)] for cs in blocks]
        carries = [h0_ref[0, :, cs] for cs in blocks]
        h_last = _lru_run(load_x, store_y, n_tiles, tails, carries, False, w)
        hl_ref[0] = jnp.concatenate(h_last, axis=1)
        cl_ref[...] = jnp.concatenate(tails, axis=1).reshape(N_TAIL, LRU_SEG, D_LRU)
        return

    @pl.when(t == 0)
    def _():
        h_s[...] = h0_ref[0]
        tail_s[...] = jnp.zeros(tail_s.shape, F32)
        for j in range(N_TAIL):
            tail_s[j * LRU_SEG + LRU_SEG - 1:(j + 1) * LRU_SEG, :] = c0_ref[0, j:j + 1, :]

    tails = [tail_s[:, cs] for cs in blocks]
    carries = [h_s[:, cs] for cs in blocks]
    _lru_run(load_x, store_y, n_tiles, tails, carries, True, w)
    c = jnp.concatenate(carries, axis=1)
    tail = jnp.concatenate(tails, axis=1)
    h_s[...] = c
    tail_s[...] = tail

    @pl.when(t == pl.num_programs(1) - 1)
    def _():
        hl_ref[0] = c
        last = LRU_SEG - 1
        cl_ref[0] = jnp.concatenate(
            [tail[j * LRU_SEG + last:(j + 1) * LRU_SEG] for j in range(N_TAIL)], axis=0)


def _lru(x, h0, c0, weights, chained):
    nb, seq, _ = x.shape
    fixed = lambda a: pl.BlockSpec(a.shape, lambda i, j: (0,) * a.ndim)
    if chained:
        state_specs = [pl.BlockSpec((1, 1, D_LRU), lambda i, j: (i, 0, 0)),
                       pl.BlockSpec((1, N_TAIL, D_LRU), lambda i, j: (i, 0, 0))]
        state_shapes = [jax.ShapeDtypeStruct((nb, 1, D_LRU), F32),
                        jax.ShapeDtypeStruct((nb, N_TAIL, D_LRU), F32)]
    else:
        assert seq == LRU_TILE
        state_specs = [pl.BlockSpec((1, LRU_SEG, D_LRU), lambda i, j: (i, 0, 0)),
                       pl.BlockSpec((N_TAIL, LRU_SEG, D_LRU), lambda i, j: (0, i, 0))]
        state_shapes = [jax.ShapeDtypeStruct((nb, LRU_SEG, D_LRU), F32),
                        jax.ShapeDtypeStruct((N_TAIL, nb * LRU_SEG, D_LRU), F32)]
    rows = LRU_TILE * (LRU_TILES_PER_STEP if chained else 1)
    tile = pl.BlockSpec((1, rows, D_MODEL), lambda i, j: (i, j, 0))
    return pl.pallas_call(
        functools.partial(_lru_body, chained=chained),
        grid=(nb, seq // rows),
        in_specs=[tile] + state_specs + [fixed(a) for a in weights],
        out_specs=[tile] + state_specs,
        out_shape=[jax.ShapeDtypeStruct(x.shape, F32)] + state_shapes,
        scratch_shapes=[pltpu.VMEM((1, D_LRU), F32), pltpu.VMEM((N_TAIL * LRU_SEG, D_LRU), F32)],
        compiler_params=_params(2),
        name="lru_chained" if chained else "lru_batched",
    )(x, h0, c0, *weights)


def _rope_tables(pos):
    half = HEAD_DIM // 2
    inv = ROPE_THETA ** (-jnp.arange(half, dtype=F32) / half)
    ang = pos.astype(F32)[:, None] * inv[None, :]
    cos = jnp.tile(jnp.cos(ang), (1, LANES // half))
    sin = jnp.sin(ang)
    sin = jnp.tile(jnp.concatenate([-sin, sin], axis=1), (1, LANES // HEAD_DIM))
    return cos, sin


def _interleave_groups(w, axis, unit=HEAD_DIM):
    shape = w.shape
    per = H_A // KV_A
    w = w.reshape(shape[:axis] + (KV_A, per, unit) + shape[axis + 1:])
    w = jnp.swapaxes(w, axis, axis + 1)
    return w.reshape(shape)


def _block_diag(w):
    per = MXU_DIM // BLOCK
    w = w.reshape(N_BLOCKS // per, per, BLOCK, BLOCK)
    eye = jnp.eye(per, dtype=w.dtype)
    w = w[:, :, :, None, :] * eye[None, :, None, :, None]
    return w.reshape(N_BLOCKS // per, MXU_DIM, MXU_DIM)


def kernel(x_prompt, x_sample, cache_a_k, cache_a_v, cache_b_k, cache_b_v, state_c_h, state_c_conv,
           ln_pre, ln_post, w_in_ab, sinks_a, relpos_b, w_out_ab, w_in_c, conv_c_w, conv_c_b,
           gate_c_wa, gate_c_ba, gate_c_wx, gate_c_bx, lambda_c, w_out_c):
    bp, s_len, _ = x_prompt.shape
    bs, t_s, _ = x_sample.shape
    wa_rows = cache_a_k.shape[2]
    wb_rows = cache_b_k.shape[2]
    assert wa_rows + t_s <= WIN_A and wb_rows + t_s <= WIN_B and wb_rows == PAD_ROWS

    w_in = w_in_ab[0]
    w_in = jnp.concatenate([_interleave_groups(w_in[:, :QA_W], 1), w_in[:, QA_W:QA_W + 2 * KA_W],
                            _interleave_groups(w_in[:, QA_W + 2 * KA_W:2 * QA_W + 2 * KA_W], 1),
                            w_in[:, 2 * QA_W + 2 * KA_W:]], axis=1).astype(BF16)
    w_out = jnp.concatenate([_interleave_groups(w_out_ab[0, :QA_W], 0), w_out_ab[0, QA_W:]],
                            axis=0).astype(BF16)
    sinks = _interleave_groups(sinks_a[0], 0, unit=1)
    g_pre0 = ln_pre[0].reshape(1, D_MODEL)
    g_post0 = ln_post[0].reshape(1, D_MODEL)
    bias_p, bias_s, band = _build_bias(relpos_b[0], t_s, wb_rows + t_s)

    cos_p, sin_p = _rope_tables(jnp.arange(s_len, dtype=jnp.int32))
    assert wb_rows == PROJ_TM and wa_rows <= PROJ_TM and s_len % PROJ_TM == 0
    proj = _proj0(x_prompt.reshape(bp * s_len, D_MODEL), g_pre0, w_in, cos_p, sin_p, PROJ_TM,
                  seq=s_len, wa_rows=wa_rows)
    qa, ka, va, ga, qb, kb, vb, gb = [a.reshape(bp, s_len, a.shape[-1]) for a in proj[:8]]
    y0_p = _attn_prompt(sinks, x_prompt, qa, ga, qb, gb, ka, va, kb, vb, bias_p, band, w_out,
                        g_post0, ATT_TQ)
    tmaj = lambda c, heads: jnp.transpose(
        c.reshape(c.shape[0], heads, HEAD_DIM, c.shape[2]), (0, 3, 1, 2))[None]
    nak_p, nav_p = tmaj(proj[8], KV_A), tmaj(proj[9], KV_A)
    nbk_p, nbv_p = tmaj(proj[10], H_B), tmaj(proj[11], H_B)

    n_s = bs * t_s
    tm_s = min(PROJ_TM, n_s)
    cos_s, sin_s = _rope_tables(PAST_LEN + jnp.arange(t_s, dtype=jnp.int32))
    cos_s = jnp.tile(cos_s, (tm_s // t_s, 1))
    sin_s = jnp.tile(sin_s, (tm_s // t_s, 1))
    proj = _proj0(x_sample.reshape(n_s, D_MODEL), g_pre0, w_in, cos_s, sin_s, tm_s)
    qa, ka, va, ga, qb, kb, vb, gb = [a.reshape(bs, t_s, a.shape[-1]) for a in proj]
    fmaj = lambda c: jnp.transpose(c[0], (0, 2, 3, 1))
    cak, cav, cbk, cbv = fmaj(cache_a_k), fmaj(cache_a_v), fmaj(cache_b_k), fmaj(cache_b_v)
    flat = lambda c: c.reshape(bs, c.shape[1] * HEAD_DIM, c.shape[3])
    y0_s, nak_s, nav_s, nbk_s, nbv_s = _attn_sample(
        sinks, x_sample, qa, ga, qb, gb, ka, va, kb, vb, flat(cak), flat(cav), flat(cbk), flat(cbv),
        bias_s, w_out, g_post0)
    nak_s, nav_s = tmaj(nak_s, KV_A), tmaj(nav_s, KV_A)
    nbk_s, nbv_s = tmaj(nbk_s, H_B), tmaj(nbv_s, H_B)

    row = lambda v: v.reshape(1, -1)
    lru_w = (row(ln_pre[1]), w_in_c[0].astype(BF16), conv_c_w[0], row(conv_c_b[0]),
             _block_diag(gate_c_wa[0]).astype(BF16), row(gate_c_ba[0]),
             _block_diag(gate_c_wx[0]).astype(BF16), row(gate_c_bx[0]),
             row(lambda_c[0]), w_out_c[0].astype(BF16), row(ln_post[1]))
    y1_p, hl_p, cl_p = _lru(y0_p, jnp.zeros((bp, 1, D_LRU), F32), jnp.zeros((bp, N_TAIL, D_LRU), F32),
                            lru_w, chained=True)
    assert t_s == LRU_STEPS and bs % LRU_SEG == 0
    y1_s, hl_s, cl_s = _lru(y0_s.reshape(bs // LRU_SEG, LRU_TILE, D_MODEL),
                            state_c_h[0].reshape(bs // LRU_SEG, LRU_SEG, D_LRU),
                            jnp.transpose(state_c_conv[0], (1, 0, 2)), lru_w, chained=False)
    y1_s = y1_s.reshape(bs, t_s, D_MODEL)
    cl_s = jnp.transpose(cl_s, (1, 0, 2))

    return (y1_p, y1_s, nak_p, nav_p, nbk_p, nbv_p,
            hl_p.reshape(1, bp, D_LRU), cl_p.reshape(1, bp, N_TAIL, D_LRU),
            nak_s, nav_s, nbk_s, nbv_s,
            hl_s.reshape(1, bs, D_LRU), cl_s.reshape(1, bs, N_TAIL, D_LRU))
```

```python
import functools

import jax
import jax.numpy as jnp
from jax import lax
from jax.experimental import pallas as pl
from jax.experimental.pallas import tpu as pltpu

F32 = jnp.float32
BF16 = jnp.bfloat16

D_MODEL = 1024
CHUNK = 64
HEAD_DIM = 64
H_A = 8
KV_A = 2
N_PREV_A = 2
H_B = 8
N_PREV_B = 8
MAX_REL = 128
ROPE_THETA = 10000.0
D_LRU = D_MODEL
N_BLOCKS = 16
BLOCK = D_LRU // N_BLOCKS
CONV_W = 4
C_GATE = 8.0
EPS = 1e-6
PAST_LEN = 1024
NEG = -1e30
LOG2E = 1.4426950408889634
Q_SCALE = HEAD_DIM ** -0.5 * LOG2E

LANES = 128
MXU_DIM = 256
VMEM_LIMIT = 56 * 1024 * 1024

QA_W = H_A * HEAD_DIM
KA_W = KV_A * HEAD_DIM
B_W = H_B * HEAD_DIM
MIX_W = QA_W + B_W
GRP = 2 * CHUNK
WIN_A = (N_PREV_A + 2) * CHUNK
WIN_B = (N_PREV_B + 2) * CHUNK
PAD_ROWS = N_PREV_B * CHUNK
TOEP_W = WIN_B + LANES

PROJ_TM = 512
ATT_TQ = 512


def _params(n_axes):
    return pltpu.CompilerParams(
        dimension_semantics=("arbitrary",) * n_axes,
        vmem_limit_bytes=VMEM_LIMIT)


def _rms(x, g):
    ms = jnp.mean(x * x, axis=-1, keepdims=True)
    return x * lax.rsqrt(ms + EPS) * g


def _silu(x):
    return x * jax.nn.sigmoid(x)


def _proj0_body(x_ref, g_ref, w_ref, cos_ref, sin_ref,
                qa_ref, ka_ref, va_ref, ga_ref, qb_ref, kb_ref, vb_ref, gb_ref, *cache_refs,
                tiles_per_seq, wa_rows):
    h = _rms(x_ref[...], g_ref[...]).astype(BF16)
    cos = cos_ref[...]
    sin = sin_ref[...]
    tm = cos.shape[0]
    lane = lax.broadcasted_iota(jnp.int32, (tm, LANES), 1)
    lower = (lane & (HEAD_DIM - 1)) < (HEAD_DIM // 2)

    def mm(lo, hi):
        return jnp.dot(h, w_ref[:, lo:hi], preferred_element_type=F32)

    def rope(t):
        partner = jnp.where(lower, pltpu.roll(t, LANES - HEAD_DIM // 2, 1),
                            pltpu.roll(t, HEAD_DIM // 2, 1))
        return t * cos + partner * sin

    o = 0
    t = mm(o, o + QA_W)
    for k in range(QA_W // LANES):
        sl = slice(k * LANES, (k + 1) * LANES)
        qa_ref[:, sl] = (rope(t[:, sl]) * Q_SCALE).astype(BF16)
    o += QA_W
    ka = rope(mm(o, o + KA_W))
    ka_ref[...] = ka.astype(BF16)
    o += KA_W
    va = mm(o, o + KA_W)
    va_ref[...] = va.astype(BF16)
    o += KA_W
    ga_ref[...] = _silu(mm(o, o + QA_W)).astype(BF16)
    o += QA_W
    qb_ref[...] = (mm(o, o + B_W) * Q_SCALE).astype(BF16)
    o += B_W
    kb = mm(o, o + B_W)
    kb_ref[...] = kb.astype(BF16)
    o += B_W
    vb = mm(o, o + B_W)
    vb_ref[...] = vb.astype(BF16)
    o += B_W
    gb_ref[...] = _silu(mm(o, o + B_W)).astype(BF16)

    if cache_refs:
        @pl.when(pl.program_id(0) % tiles_per_seq == tiles_per_seq - 1)
        def _():
            kat_ref, vat_ref, kbt_ref, vbt_ref = cache_refs
            kat_ref[0] = ka[tm - wa_rows:].T
            vat_ref[0] = va[tm - wa_rows:].T
            kbt_ref[0] = kb.T
            vbt_ref[0] = vb.T


def _proj0(x2d, g, w, cos, sin, tm, seq=None, wa_rows=None):
    n = x2d.shape[0]
    n_pos_blocks = cos.shape[0] // tm
    widths = (QA_W, KA_W, KA_W, QA_W, B_W, B_W, B_W, B_W)
    row = lambda i: (i, 0)
    fixed = lambda i: (0, 0)
    pos = lambda i: (i % n_pos_blocks, 0)
    out_specs = [pl.BlockSpec((tm, wd), row) for wd in widths]
    out_shape = [jax.ShapeDtypeStruct((n, wd), BF16) for wd in widths]
    tiles_per_seq = None
    if seq is not None:
        tiles_per_seq = seq // tm
        for feat, keys in ((KA_W, wa_rows), (KA_W, wa_rows), (B_W, tm), (B_W, tm)):
            out_specs.append(pl.BlockSpec((1, feat, keys), lambda i: (i // tiles_per_seq, 0, 0)))
            out_shape.append(jax.ShapeDtypeStruct((n // seq, feat, keys), F32))
    return pl.pallas_call(
        functools.partial(_proj0_body, tiles_per_seq=tiles_per_seq, wa_rows=wa_rows),
        grid=(n // tm,),
        in_specs=[pl.BlockSpec((tm, D_MODEL), row),
                  pl.BlockSpec((1, D_MODEL), fixed),
                  pl.BlockSpec(w.shape, fixed),
                  pl.BlockSpec((tm, LANES), pos),
                  pl.BlockSpec((tm, LANES), pos)],
        out_specs=out_specs,
        out_shape=out_shape,
        compiler_params=_params(1),
        name="proj0",
    )(x2d, g, w, cos, sin)


def _bias_body(r0_ref, bp_ref, bs_ref, band_ref, *, t_sample, keys_sample):
    x = jnp.broadcast_to(r0_ref[0], (GRP, TOEP_W)) * LOG2E
    row = lax.broadcasted_iota(jnp.int32, (GRP, TOEP_W), 0)
    for b in range(GRP.bit_length() - 1):
        x = jnp.where(((row >> b) & 1) == 1, pltpu.roll(x, 1 << b, 1), x)
    x = x[:, :WIN_B]
    r = lax.broadcasted_iota(jnp.int32, (GRP, WIN_B), 0)
    c = lax.broadcasted_iota(jnp.int32, (GRP, WIN_B), 1)
    ci = r // CHUNK
    cj = c // CHUNK
    bp_ref[0] = jnp.where((cj >= ci) & (cj <= ci + N_PREV_B), x, NEG)
    cs = lax.broadcasted_iota(jnp.int32, (t_sample, WIN_B), 1)
    bs_ref[0] = jnp.where(cs < keys_sample, x[:t_sample], NEG)
    ra = lax.broadcasted_iota(jnp.int32, (GRP, WIN_A), 0) // CHUNK
    ca = lax.broadcasted_iota(jnp.int32, (GRP, WIN_A), 1) // CHUNK
    band_ref[...] = jnp.where((ca >= ra) & (ca <= ra + N_PREV_A), 0.0, NEG)


def _build_bias(relpos, t_sample, keys_sample):
    far = relpos[:, 2 * MAX_REL:]
    n_far = PAD_ROWS - MAX_REL + 1
    r0 = jnp.concatenate(
        [jnp.broadcast_to(far, (H_B, n_far)),
         relpos[:, 2 * MAX_REL - 1:0:-1],
         jnp.broadcast_to(far, (H_B, TOEP_W - n_far - (2 * MAX_REL - 1)))], axis=1)
    r0 = r0.reshape(H_B, 1, TOEP_W)
    return pl.pallas_call(
        functools.partial(_bias_body, t_sample=t_sample, keys_sample=keys_sample),
        grid=(H_B,),
        in_specs=[pl.BlockSpec((1, 1, TOEP_W), lambda h: (h, 0, 0))],
        out_specs=[pl.BlockSpec((1, GRP, WIN_B), lambda h: (h, 0, 0)),
                   pl.BlockSpec((1, t_sample, WIN_B), lambda h: (h, 0, 0)),
                   pl.BlockSpec((GRP, WIN_A), lambda h: (0, 0))],
        out_shape=[jax.ShapeDtypeStruct((H_B, GRP, WIN_B), F32),
                   jax.ShapeDtypeStruct((H_B, t_sample, WIN_B), F32),
                   jax.ShapeDtypeStruct((GRP, WIN_A), F32)],
        compiler_params=_params(1),
        name="relbias",
    )(r0)


def _nt_dot(a, b):
    return lax.dot_general(a, b, (((1,), (1,)), ((), ())), preferred_element_type=F32)


def _attend_group(qa, qb, ka, va, kb_ref, vb_ref, win_b, sink_ref, bias_ref, mask_a, col_ok_a,
                  col_ok_b):
    n = qa.shape[0]
    lane = lax.broadcasted_iota(jnp.int32, (n, LANES), 1)
    lo = lane < HEAD_DIM
    zero = jnp.zeros((n, LANES), BF16)
    halves = lambda tile: [jnp.where(lo, tile, zero), jnp.where(lo, zero, tile)]
    n_pairs = B_W // LANES

    def scores(unit):
        if unit == 0:
            lhs = jnp.concatenate([h for p in range(QA_W // LANES)
                                   for h in halves(qa[:, p * LANES:(p + 1) * LANES])], axis=0)
            return _nt_dot(lhs, ka)
        sl = slice((unit - 1) * LANES, unit * LANES)
        return _nt_dot(jnp.concatenate(halves(qb[:, sl]), axis=0), kb_ref[win_b, sl])

    def finish_a(s_all):
        probs, inv = [], []
        for h in range(H_A):
            s = s_all[h * n:(h + 1) * n]
            if mask_a is not None:
                s = s + mask_a
            if col_ok_a is not None:
                s = jnp.where(col_ok_a, s, NEG)
            sink = sink_ref[h] * LOG2E
            m = jnp.maximum(jnp.max(s, axis=-1, keepdims=True), sink)
            e = jnp.exp2(s - m)
            den = jnp.sum(e, axis=-1, keepdims=True) + jnp.exp2(sink - m)
            probs.append(e.astype(BF16))
            inv.append(1.0 / den)
        r_all = jnp.dot(jnp.concatenate(probs, axis=0), va, preferred_element_type=F32)
        out = []
        for p in range(QA_W // LANES):
            r0 = r_all[(2 * p) * n:(2 * p + 1) * n] * inv[2 * p]
            r1 = r_all[(2 * p + 1) * n:(2 * p + 2) * n] * inv[2 * p + 1]
            out.append(jnp.where(lo, r0, r1))
        return jnp.concatenate(out, axis=1)

    def finish_b(p, s2):
        sl = slice(p * LANES, (p + 1) * LANES)
        probs, inv = [], []
        for j in range(2):
            s = s2[j * n:(j + 1) * n] + bias_ref[2 * p + j]
            if col_ok_b is not None:
                s = jnp.where(col_ok_b, s, NEG)
            m = jnp.max(s, axis=-1, keepdims=True)
            e = jnp.exp2(s - m)
            probs.append(e.astype(BF16))
            inv.append(1.0 / jnp.sum(e, axis=-1, keepdims=True))
        r = jnp.dot(jnp.concatenate(probs, axis=0), vb_ref[win_b, sl], preferred_element_type=F32)
        return jnp.where(lo, r[:n] * inv[0], r[n:] * inv[1])

    s_cur = scores(0)
    o_a, o_b = None, []
    for unit in range(n_pairs + 1):
        s_next = scores(unit + 1) if unit < n_pairs else None
        if unit == 0:
            o_a = finish_a(s_cur)
        else:
            o_b.append(finish_b(unit - 1, s_cur))
        s_cur = s_next
    return o_a, jnp.concatenate(o_b, axis=1)


def _mix_out(o_a, o_b, ga, gb, x, w_ref, g_ref):
    mix = jnp.concatenate([o_a * ga.astype(F32), o_b * gb.astype(F32)], axis=1).astype(BF16)
    m = jnp.dot(mix, w_ref[...], preferred_element_type=F32)
    return x + _rms(m, g_ref[...])


def _attn_prompt_body(sink_ref, x_ref, qa_ref, ga_ref, qb_ref, gb_ref, ka_ref, va_ref, kb_ref,
                      vb_ref, bias_ref, band_ref, w_ref, g_ref, y_ref,
                      kap, vap, kbp, vbp, *, seq):
    t = pl.program_id(1)
    tq = x_ref.shape[1]

    @pl.when(t == 0)
    def _():
        for dst, src in ((kap, ka_ref), (vap, va_ref), (kbp, kb_ref), (vbp, vb_ref)):
            dst[:PAD_ROWS, :] = jnp.zeros((PAD_ROWS, dst.shape[1]), BF16)
            dst[PAD_ROWS:, :] = src[0]

    band = band_ref[...]

    def group(sub, masked):
        start = pl.multiple_of(t * tq + sub * GRP, GRP)
        rs = slice(sub * GRP, (sub + 1) * GRP)
        win_a = pl.ds(pl.multiple_of(start + (PAD_ROWS - N_PREV_A * CHUNK), GRP), WIN_A)
        win_b = pl.ds(start, WIN_B)
        ok_a = ok_b = None
        if masked:
            ok_a = lax.broadcasted_iota(jnp.int32, (1, WIN_A), 1) >= N_PREV_A * CHUNK - start
            ok_b = lax.broadcasted_iota(jnp.int32, (1, WIN_B), 1) >= N_PREV_B * CHUNK - start
        o_a, o_b = _attend_group(qa_ref[0, rs, :], qb_ref[0, rs, :], kap[win_a, :], vap[win_a, :],
                                 kbp, vbp, win_b, sink_ref, bias_ref, band, ok_a, ok_b)
        y_ref[0, rs, :] = _mix_out(o_a, o_b, ga_ref[0, rs, :], gb_ref[0, rs, :], x_ref[0, rs, :],
                                   w_ref, g_ref)

    n_masked = PAD_ROWS // tq

    @pl.when(t < n_masked)
    def _():
        for sub in range(tq // GRP):
            group(sub, True)

    @pl.when(t >= n_masked)
    def _():
        for sub in range(tq // GRP):
            group(sub, False)


def _attn_prompt(sinks, x, qa, ga, qb, gb, ka, va, kb, vb, bias, band, w_out, g_post, tq):
    b, seq, _ = x.shape
    tile = lambda w: pl.BlockSpec((1, tq, w), lambda i, j: (i, j, 0))
    whole = lambda w: pl.BlockSpec((1, seq, w), lambda i, j: (i, 0, 0))
    fixed = lambda shape: pl.BlockSpec(shape, lambda i, j: (0,) * len(shape))
    return pl.pallas_call(
        functools.partial(_attn_prompt_body, seq=seq),
        grid=(b, seq // tq),
        in_specs=[pl.BlockSpec(memory_space=pltpu.SMEM),
                  tile(D_MODEL), tile(QA_W), tile(QA_W), tile(B_W), tile(B_W),
                  whole(KA_W), whole(KA_W), whole(B_W), whole(B_W),
                  fixed(bias.shape), fixed(band.shape), fixed(w_out.shape), fixed((1, D_MODEL))],
        out_specs=tile(D_MODEL),
        out_shape=jax.ShapeDtypeStruct(x.shape, F32),
        scratch_shapes=[pltpu.VMEM((PAD_ROWS + seq, KA_W), BF16),
                        pltpu.VMEM((PAD_ROWS + seq, KA_W), BF16),
                        pltpu.VMEM((PAD_ROWS + seq, B_W), BF16),
                        pltpu.VMEM((PAD_ROWS + seq, B_W), BF16)],
        compiler_params=_params(2),
        name="attn_prompt",
    )(sinks, x, qa, ga, qb, gb, ka, va, kb, vb, bias, band, w_out, g_post)


SAMPLE_REQS = 4


def _softmax2(s_c, s_n, sink):
    m = jnp.maximum(jnp.max(s_c, axis=-1, keepdims=True), jnp.max(s_n, axis=-1, keepdims=True))
    if sink is not None:
        m = jnp.maximum(m, sink)
    e_c = jnp.exp2(s_c - m)
    e_n = jnp.exp2(s_n - m)
    den = jnp.sum(e_c, axis=-1, keepdims=True) + jnp.sum(e_n, axis=-1, keepdims=True)
    if sink is not None:
        den = den + jnp.exp2(sink - m)
    return e_c.astype(BF16), e_n.astype(BF16), 1.0 / den


def _roll_in(cache, new):
    t, w = new.shape[0], cache.shape[1]
    lane = lax.broadcasted_iota(jnp.int32, (cache.shape[0], LANES), 1)
    padded = jnp.concatenate([jnp.zeros((LANES - t, new.shape[1]), F32), new.astype(F32)], axis=0)
    rolled = pltpu.roll(cache, w - t, 1)
    last = jnp.where(lane >= LANES - t, padded.T, rolled[:, w - LANES:])
    return last if w == LANES else jnp.concatenate([rolled[:, :w - LANES], last], axis=1)


def _attn_sample_body(sink_ref, x_ref, qa_ref, ga_ref, qb_ref, gb_ref, ka_ref, va_ref, kb_ref,
                      vb_ref, cak_ref, cav_ref, cbk_ref, cbv_ref, bias_ref, w_ref, g_ref,
                      y_ref, nak_ref, nav_ref, nbk_ref, nbv_ref):
    nreq, t, _ = x_ref.shape
    for r in range(nreq):
        for dst, cache, new in ((nak_ref, cak_ref, ka_ref), (nav_ref, cav_ref, va_ref),
                                (nbk_ref, cbk_ref, kb_ref), (nbv_ref, cbv_ref, vb_ref)):
            dst[r] = _roll_in(cache[r], new[r])

    wb = cbk_ref.shape[2]
    lane = lax.broadcasted_iota(jnp.int32, (t, LANES), 1)
    lo = lane < HEAD_DIM
    zero = jnp.zeros((t, LANES), BF16)
    halves = lambda tile: [jnp.where(lo, tile, zero), jnp.where(lo, zero, tile)]
    n_pairs = B_W // LANES

    def scores(r, unit):
        if unit == 0:
            qa = qa_ref[r]
            lhs = jnp.concatenate([h for p in range(QA_W // LANES)
                                   for h in halves(qa[:, p * LANES:(p + 1) * LANES])], axis=0)
            return (jnp.dot(lhs, cak_ref[r].astype(BF16), preferred_element_type=F32),
                    _nt_dot(lhs, ka_ref[r]))
        sl = slice((unit - 1) * LANES, unit * LANES)
        lhs = jnp.concatenate(halves(qb_ref[r][:, sl]), axis=0)
        return (jnp.dot(lhs, cbk_ref[r, sl, :].astype(BF16), preferred_element_type=F32),
                _nt_dot(lhs, kb_ref[r, :, sl]))

    def finish(r, unit, s):
        s_c, s_n = s
        if unit == 0:
            cache_v, new_v = cav_ref[r].astype(BF16), va_ref[r]
            heads = [(h, sink_ref[h] * LOG2E, None) for h in range(H_A)]
        else:
            sl = slice((unit - 1) * LANES, unit * LANES)
            cache_v, new_v = cbv_ref[r, sl, :].astype(BF16), vb_ref[r, :, sl]
            heads = [(j, None, bias_ref[2 * (unit - 1) + j]) for j in range(2)]
        probs_c, probs_n, inv = [], [], []
        for j, sink, bias in heads:
            rs = slice(j * t, (j + 1) * t)
            sc, sn = s_c[rs], s_n[rs]
            if bias is not None:
                sc, sn = sc + bias[:, :wb], sn + bias[:, wb:wb + t]
            e_c, e_n, iv = _softmax2(sc, sn, sink)
            probs_c.append(e_c)
            probs_n.append(e_n)
            inv.append(iv)
        o = (_nt_dot(jnp.concatenate(probs_c, axis=0), cache_v)
             + jnp.dot(jnp.concatenate(probs_n, axis=0), new_v, preferred_element_type=F32))
        tiles = []
        for p in range(len(heads) // 2):
            o0 = o[(2 * p) * t:(2 * p + 1) * t] * inv[2 * p]
            o1 = o[(2 * p + 1) * t:(2 * p + 2) * t] * inv[2 * p + 1]
            tiles.append(jnp.where(lo, o0, o1))
        return tiles

    order = [(r, u) for r in range(nreq) for u in range(n_pairs + 1)]
    pending = {order[0]: scores(*order[0])}
    mix, tiles = [], []
    for i, (r, u) in enumerate(order):
        if i + 1 < len(order):
            pending[order[i + 1]] = scores(*order[i + 1])
        tiles += finish(r, u, pending.pop((r, u)))
        if u == n_pairs:
            gate = jnp.concatenate([ga_ref[r], gb_ref[r]], axis=1).astype(F32)
            mix.append((jnp.concatenate(tiles, axis=1) * gate).astype(BF16))
            tiles = []
    m = jnp.dot(jnp.concatenate(mix, axis=0), w_ref[...], preferred_element_type=F32)
    y = x_ref[...].reshape(nreq * t, D_MODEL) + _rms(m, g_ref[...])
    y_ref[...] = y.reshape(nreq, t, D_MODEL)


def _attn_sample(sinks, x, qa, ga, qb, gb, ka, va, kb, vb, cak, cav, cbk, cbv, bias, w_out, g_post):
    b = x.shape[0]
    per = lambda a: pl.BlockSpec((SAMPLE_REQS,) + a.shape[1:], lambda i: (i, 0, 0))
    fixed = lambda shape: pl.BlockSpec(shape, lambda i: (0,) * len(shape))
    arrs = (x, qa, ga, qb, gb, ka, va, kb, vb, cak, cav, cbk, cbv)
    return pl.pallas_call(
        _attn_sample_body,
        grid=(b // SAMPLE_REQS,),
        in_specs=[pl.BlockSpec(memory_space=pltpu.SMEM)] + [per(a) for a in arrs]
                 + [fixed(bias.shape), fixed(w_out.shape), fixed((1, D_MODEL))],
        out_specs=[per(a) for a in (x, cak, cav, cbk, cbv)],
        out_shape=[jax.ShapeDtypeStruct(a.shape, F32) for a in (x, cak, cav, cbk, cbv)],
        compiler_params=_params(1),
        name="attn_sample",
    )(sinks, *arrs, bias, w_out, g_post)


LRU_SEG = 8
LRU_STEPS = 32
LRU_TILE = LRU_SEG * LRU_STEPS
N_TAIL = CONV_W - 1
LRU_TILES_PER_STEP = 4
LRU_CB = D_LRU
N_CB = D_LRU // LRU_CB


def _lru_block(xb, z, tail, carry_in, cb, chained, w, after_gates):
    cw_ref, cb_ref, wa_ref, ba_ref, wx_ref, bx_ref, lam_ref = w[2:9]
    cs = slice(cb * LRU_CB, (cb + 1) * LRU_CB)
    grp = lambda v, g, n=1: v[g * LRU_SEG:(g + n) * LRU_SEG]

    if chained:
        sub = lax.broadcasted_iota(jnp.int32, (LRU_SEG, LRU_CB), 0)
        before = [jnp.where(sub == 0, pltpu.roll(grp(tail, j), 1, 0),
                            pltpu.roll(grp(xb, LRU_STEPS - N_TAIL + j), 1, 0))
                  for j in range(N_TAIL)]
    else:
        before = tail
    xb_tail = grp(xb, LRU_STEPS - N_TAIL, N_TAIL)

    cw = cw_ref[:, cs]
    xc = cb_ref[:, cs] + xb * cw[CONV_W - 1:CONV_W]
    for k in range(1, CONV_W):
        shifted = jnp.concatenate(before[N_TAIL - k:] + [grp(xb, 0, LRU_STEPS - k)], axis=0)
        xc = xc + shifted * cw[CONV_W - 1 - k:CONV_W - k]

    xcb = xc.astype(BF16)
    per = LRU_CB // MXU_DIM

    both = [jnp.dot(xcb[:, j * MXU_DIM:(j + 1) * MXU_DIM],
                    jnp.concatenate([wa_ref[cb * per + j], wx_ref[cb * per + j]], axis=1),
                    preferred_element_type=F32) for j in range(per)]
    pre_r = jnp.concatenate([b[:, :MXU_DIM] for b in both], axis=1)
    pre_i = jnp.concatenate([b[:, MXU_DIM:] for b in both], axis=1)
    after_gates()
    r = jax.nn.sigmoid(pre_r + ba_ref[:, cs])
    gi = jax.nn.sigmoid(pre_i + bx_ref[:, cs])
    lam = lam_ref[:, cs]
    log_sig = jnp.minimum(lam, 0.0) - jnp.log1p(jnp.exp(-jnp.abs(lam)))
    log_a = r * (C_GATE * log_sig)
    a = jnp.exp(log_a)
    u = jnp.sqrt(jnp.tanh(-log_a) * (a * a + 1.0)) * (gi * xc)

    hc, ac = grp(u, 0), grp(a, 0)
    h_loc, a_loc = [hc], [ac]
    for g in range(1, LRU_STEPS):
        ag = grp(a, g)
        hc = ag * hc + grp(u, g)
        ac = ag * ac
        h_loc.append(hc)
        a_loc.append(ac)

    if chained:
        c = carry_in
        rows = []
        for s in range(LRU_SEG):
            rows.append(c)
            c = ac[s:s + 1] * c + hc[s:s + 1]
        carry = jnp.concatenate(rows, axis=0)
    else:
        c = None
        carry = carry_in
    h_groups = [hl + al * carry for hl, al in zip(h_loc, a_loc)]
    yl = jnp.concatenate(h_groups, axis=0) * _silu(z)
    yl = jnp.swapaxes(yl.reshape(LRU_STEPS, LRU_SEG, LRU_CB), 0, 1)
    return yl.reshape(LRU_TILE, LRU_CB).astype(BF16), c, xb_tail, h_groups[-1]


def _lru_run(load_x, store_y, n_tiles, tails, carries, chained, w):
    gpre_ref, win_ref = w[:2]
    wout_ref, gpost_ref = w[9:11]
    hp = {}

    def in_proj(unit):
        ti, cb = divmod(unit, N_CB)
        if ti not in hp:
            h = _rms(load_x(ti), gpre_ref[...])
            h = jnp.swapaxes(h.reshape(LRU_SEG, LRU_STEPS, D_MODEL), 0, 1)
            hp[ti] = h.reshape(LRU_TILE, D_MODEL).astype(BF16)
        lo = cb * LRU_CB
        xb = jnp.dot(hp[ti], win_ref[:, lo:lo + LRU_CB], preferred_element_type=F32)
        z = jnp.dot(hp[ti], win_ref[:, D_LRU + lo:D_LRU + lo + LRU_CB], preferred_element_type=F32)
        return xb, z

    units = n_tiles * N_CB
    ready = {0: in_proj(0)}
    h_last = [None] * N_CB
    m = None
    for unit in range(units):
        def emit_next(unit=unit):
            if unit + 1 < units:
                ready[unit + 1] = in_proj(unit + 1)

        ti, cb = divmod(unit, N_CB)
        xb, z = ready.pop(unit)
        yl, carries[cb], tails[cb], h_last[cb] = _lru_block(xb, z, tails[cb], carries[cb], cb,
                                                            chained, w, emit_next)
        part = jnp.dot(yl, wout_ref[cb * LRU_CB:(cb + 1) * LRU_CB, :], preferred_element_type=F32)
        m = part if cb == 0 else m + part
        if cb == N_CB - 1:
            store_y(ti, load_x(ti) + _rms(m, gpost_ref[...]))
    return h_last


def _lru_body(x_ref, h0_ref, c0_ref, *rest, chained):
    w, (y_ref, hl_ref, cl_ref, h_s, tail_s) = rest[:-5], rest[-5:]
    t = pl.program_id(1)
    n_tiles = x_ref.shape[1] // LRU_TILE
    blocks = [slice(cb * LRU_CB, (cb + 1) * LRU_CB) for cb in range(N_CB)]
    rows = lambda ti: slice(ti * LRU_TILE, (ti + 1) * LRU_TILE)
    load_x = lambda ti: x_ref[0, rows(ti), :]

    def store_y(ti, y):
        y_ref[0, rows(ti), :] = y

    if not chained:
        tails = [[c0_ref[j, :, cs] for j in range(N_TAIL)] for cs in blocks]
        carries = [h0_ref[0, :, cs] for cs in blocks]
        h_last = _lru_run(load_x, store_y, n_tiles, tails, carries, False, w)
        hl_ref[0] = jnp.concatenate(h_last, axis=1)
        cl_ref[...] = jnp.concatenate(tails, axis=1).reshape(N_TAIL, LRU_SEG, D_LRU)
        return

    @pl.when(t == 0)
    def _():
        h_s[...] = h0_ref[0]
        tail_s[...] = jnp.zeros(tail_s.shape, F32)
        for j in range(N_TAIL):
            tail_s[j * LRU_SEG + LRU_SEG - 1:(j + 1) * LRU_SEG, :] = c0_ref[0, j:j + 1, :]

    tails = [tail_s[:, cs] for cs in blocks]
    carries = [h_s[:, cs] for cs in blocks]
    _lru_run(load_x, store_y, n_tiles, tails, carries, True, w)
    c = jnp.concatenate(carries, axis=1)
    tail = jnp.concatenate(tails, axis=1)
    h_s[...] = c
    tail_s[...] = tail

    @pl.when(t == pl.num_programs(1) - 1)
    def _():
        hl_ref[0] = c
        last = LRU_SEG - 1
        cl_ref[0] = jnp.concatenate(
            [tail[j * LRU_SEG + last:(j + 1) * LRU_SEG] for j in range(N_TAIL)], axis=0)


def _lru(x, h0, c0, weights, chained):
    nb, seq, _ = x.shape
    fixed = lambda a: pl.BlockSpec(a.shape, lambda i, j: (0,) * a.ndim)
    if chained:
        state_specs = [pl.BlockSpec((1, 1, D_LRU), lambda i, j: (i, 0, 0)),
                       pl.BlockSpec((1, N_TAIL, D_LRU), lambda i, j: (i, 0, 0))]
        state_shapes = [jax.ShapeDtypeStruct((nb, 1, D_LRU), F32),
                        jax.ShapeDtypeStruct((nb, N_TAIL, D_LRU), F32)]
    else:
        assert seq == LRU_TILE
        state_specs = [pl.BlockSpec((1, LRU_SEG, D_LRU), lambda i, j: (i, 0, 0)),
                       pl.BlockSpec((N_TAIL, LRU_SEG, D_LRU), lambda i, j: (0, i, 0))]
        state_shapes = [jax.ShapeDtypeStruct((nb, LRU_SEG, D_LRU), F32),
                        jax.ShapeDtypeStruct((N_TAIL, nb * LRU_SEG, D_LRU), F32)]
    rows = LRU_TILE * (LRU_TILES_PER_STEP if chained else 1)
    tile = pl.BlockSpec((1, rows, D_MODEL), lambda i, j: (i, j, 0))
    return pl.pallas_call(
        functools.partial(_lru_body, chained=chained),
        grid=(nb, seq // rows),
        in_specs=[tile] + state_specs + [fixed(a) for a in weights],
        out_specs=[tile] + state_specs,
        out_shape=[jax.ShapeDtypeStruct(x.shape, F32)] + state_shapes,
        scratch_shapes=[pltpu.VMEM((1, D_LRU), F32), pltpu.VMEM((N_TAIL * LRU_SEG, D_LRU), F32)],
        compiler_params=_params(2),
        name="lru_chained" if chained else "lru_batched",
    )(x, h0, c0, *weights)


def _rope_tables(pos):
    half = HEAD_DIM // 2
    inv = ROPE_THETA ** (-jnp.arange(half, dtype=F32) / half)
    ang = pos.astype(F32)[:, None] * inv[None, :]
    cos = jnp.tile(jnp.cos(ang), (1, LANES // half))
    sin = jnp.sin(ang)
    sin = jnp.tile(jnp.concatenate([-sin, sin], axis=1), (1, LANES // HEAD_DIM))
    return cos, sin


def _interleave_groups(w, axis, unit=HEAD_DIM):
    shape = w.shape
    per = H_A // KV_A
    w = w.reshape(shape[:axis] + (KV_A, per, unit) + shape[axis + 1:])
    w = jnp.swapaxes(w, axis, axis + 1)
    return w.reshape(shape)


def _block_diag(w):
    per = MXU_DIM // BLOCK
    w = w.reshape(N_BLOCKS // per, per, BLOCK, BLOCK)
    eye = jnp.eye(per, dtype=w.dtype)
    w = w[:, :, :, None, :] * eye[None, :, None, :, None]
    return w.reshape(N_BLOCKS // per, MXU_DIM, MXU_DIM)


def kernel(x_prompt, x_sample, cache_a_k, cache_a_v, cache_b_k, cache_b_v, state_c_h, state_c_conv,
           ln_pre, ln_post, w_in_ab, sinks_a, relpos_b, w_out_ab, w_in_c, conv_c_w, conv_c_b,
           gate_c_wa, gate_c_ba, gate_c_wx, gate_c_bx, lambda_c, w_out_c):
    bp, s_len, _ = x_prompt.shape
    bs, t_s, _ = x_sample.shape
    wa_rows = cache_a_k.shape[2]
    wb_rows = cache_b_k.shape[2]
    assert wa_rows + t_s <= WIN_A and wb_rows + t_s <= WIN_B and wb_rows == PAD_ROWS

    w_in = w_in_ab[0]
    w_in = jnp.concatenate([_interleave_groups(w_in[:, :QA_W], 1), w_in[:, QA_W:QA_W + 2 * KA_W],
                            _interleave_groups(w_in[:, QA_W + 2 * KA_W:2 * QA_W + 2 * KA_W], 1),
                            w_in[:, 2 * QA_W + 2 * KA_W:]], axis=1).astype(BF16)
    w_out = jnp.concatenate([_interleave_groups(w_out_ab[0, :QA_W], 0), w_out_ab[0, QA_W:]],
                            axis=0).astype(BF16)
    sinks = _interleave_groups(sinks_a[0], 0, unit=1)
    g_pre0 = ln_pre[0].reshape(1, D_MODEL)
    g_post0 = ln_post[0].reshape(1, D_MODEL)
    bias_p, bias_s, band = _build_bias(relpos_b[0], t_s, wb_rows + t_s)

    cos_p, sin_p = _rope_tables(jnp.arange(s_len, dtype=jnp.int32))
    assert wb_rows == PROJ_TM and wa_rows <= PROJ_TM and s_len % PROJ_TM == 0
    proj = _proj0(x_prompt.reshape(bp * s_len, D_MODEL), g_pre0, w_in, cos_p, sin_p, PROJ_TM,
                  seq=s_len, wa_rows=wa_rows)
    qa, ka, va, ga, qb, kb, vb, gb = [a.reshape(bp, s_len, a.shape[-1]) for a in proj[:8]]
    y0_p = _attn_prompt(sinks, x_prompt, qa, ga, qb, gb, ka, va, kb, vb, bias_p, band, w_out,
                        g_post0, ATT_TQ)
    tmaj = lambda c, heads: jnp.transpose(
        c.reshape(c.shape[0], heads, HEAD_DIM, c.shape[2]), (0, 3, 1, 2))[None]
    nak_p, nav_p = tmaj(proj[8], KV_A), tmaj(proj[9], KV_A)
    nbk_p, nbv_p = tmaj(proj[10], H_B), tmaj(proj[11], H_B)

    n_s = bs * t_s
    tm_s = min(PROJ_TM, n_s)
    cos_s, sin_s = _rope_tables(PAST_LEN + jnp.arange(t_s, dtype=jnp.int32))
    cos_s = jnp.tile(cos_s, (tm_s // t_s, 1))
    sin_s = jnp.tile(sin_s, (tm_s // t_s, 1))
    proj = _proj0(x_sample.reshape(n_s, D_MODEL), g_pre0, w_in, cos_s, sin_s, tm_s)
    qa, ka, va, ga, qb, kb, vb, gb = [a.reshape(bs, t_s, a.shape[-1]) for a in proj]
    fmaj = lambda c: jnp.transpose(c[0], (0, 2, 3, 1))
    cak, cav, cbk, cbv = fmaj(cache_a_k), fmaj(cache_a_v), fmaj(cache_b_k), fmaj(cache_b_v)
    flat = lambda c: c.reshape(bs, c.shape[1] * HEAD_DIM, c.shape[3])
    y0_s, nak_s, nav_s, nbk_s, nbv_s = _attn_sample(
        sinks, x_sample, qa, ga, qb, gb, ka, va, kb, vb, flat(cak), flat(cav), flat(cbk), flat(cbv),
        bias_s, w_out, g_post0)
    nak_s, nav_s = tmaj(nak_s, KV_A), tmaj(nav_s, KV_A)
    nbk_s, nbv_s = tmaj(nbk_s, H_B), tmaj(nbv_s, H_B)

    row = lambda v: v.reshape(1, -1)
    lru_w = (row(ln_pre[1]), w_in_c[0].astype(BF16), conv_c_w[0], row(conv_c_b[0]),
             _block_diag(gate_c_wa[0]).astype(BF16), row(gate_c_ba[0]),
             _block_diag(gate_c_wx[0]).astype(BF16), row(gate_c_bx[0]),
             row(lambda_c[0]), w_out_c[0].astype(BF16), row(ln_post[1]))
    y1_p, hl_p, cl_p = _lru(y0_p, jnp.zeros((bp, 1, D_LRU), F32), jnp.zeros((bp, N_TAIL, D_LRU), F32),
                            lru_w, chained=True)
    assert t_s == LRU_STEPS and bs % LRU_SEG == 0
    y1_s, hl_s, cl_s = _lru(y0_s.reshape(bs // LRU_SEG, LRU_TILE, D_MODEL),
                            state_c_h[0].reshape(bs // LRU_SEG, LRU_SEG, D_LRU),
                            jnp.transpose(state_c_conv[0], (1, 0, 2)), lru_w, chained=False)
    y1_s = y1_s.reshape(bs, t_s, D_MODEL)
    cl_s = jnp.transpose(cl_s, (1, 0, 2))

    return (y1_p, y1_s, nak_p, nav_p, nbk_p, nbv_p,
            hl_p.reshape(1, bp, D_LRU), cl_p.reshape(1, bp, N_TAIL, D_LRU),
            nak_s, nav_s, nbk_s, nbv_s,
            hl_s.reshape(1, bs, D_LRU), cl_s.reshape(1, bs, N_TAIL, D_LRU))
```

```python
import functools

import jax
import jax.numpy as jnp
from jax import lax
from jax.experimental import pallas as pl
from jax.experimental.pallas import tpu as pltpu

F32 = jnp.float32
BF16 = jnp.bfloat16

D_MODEL = 1024
CHUNK = 64
HEAD_DIM = 64
H_A = 8
KV_A = 2
N_PREV_A = 2
H_B = 8
N_PREV_B = 8
MAX_REL = 128
ROPE_THETA = 10000.0
D_LRU = D_MODEL
N_BLOCKS = 16
BLOCK = D_LRU // N_BLOCKS
CONV_W = 4
C_GATE = 8.0
EPS = 1e-6
PAST_LEN = 1024
NEG = -1e30
LOG2E = 1.4426950408889634
Q_SCALE = HEAD_DIM ** -0.5 * LOG2E

LANES = 128
MXU_DIM = 256
VMEM_LIMIT = 56 * 1024 * 1024

QA_W = H_A * HEAD_DIM
KA_W = KV_A * HEAD_DIM
B_W = H_B * HEAD_DIM
MIX_W = QA_W + B_W
GRP = 2 * CHUNK
WIN_A = (N_PREV_A + 2) * CHUNK
WIN_B = (N_PREV_B + 2) * CHUNK
PAD_ROWS = N_PREV_B * CHUNK
TOEP_W = WIN_B + LANES

PROJ_TM = 512
ATT_TQ = 512


def _params(n_axes):
    return pltpu.CompilerParams(
        dimension_semantics=("arbitrary",) * n_axes,
        vmem_limit_bytes=VMEM_LIMIT)


def _rms(x, g):
    ms = jnp.mean(x * x, axis=-1, keepdims=True)
    return x * lax.rsqrt(ms + EPS) * g


def _silu(x):
    return x * jax.nn.sigmoid(x)


def _proj0_body(x_ref, g_ref, w_ref, cos_ref, sin_ref,
                qa_ref, ka_ref, va_ref, ga_ref, qb_ref, kb_ref, vb_ref, gb_ref, *cache_refs,
                tiles_per_seq, wa_rows):
    h = _rms(x_ref[...], g_ref[...]).astype(BF16)
    cos = cos_ref[...]
    sin = sin_ref[...]
    tm = cos.shape[0]
    lane = lax.broadcasted_iota(jnp.int32, (tm, LANES), 1)
    lower = (lane & (HEAD_DIM - 1)) < (HEAD_DIM // 2)

    def mm(lo, hi):
        return jnp.dot(h, w_ref[:, lo:hi], preferred_element_type=F32)

    def rope(t):
        partner = jnp.where(lower, pltpu.roll(t, LANES - HEAD_DIM // 2, 1),
                            pltpu.roll(t, HEAD_DIM // 2, 1))
        return t * cos + partner * sin

    o = 0
    t = mm(o, o + QA_W)
    for k in range(QA_W // LANES):
        sl = slice(k * LANES, (k + 1) * LANES)
        qa_ref[:, sl] = (rope(t[:, sl]) * Q_SCALE).astype(BF16)
    o += QA_W
    ka = rope(mm(o, o + KA_W))
    ka_ref[...] = ka.astype(BF16)
    o += KA_W
    va = mm(o, o + KA_W)
    va_ref[...] = va.astype(BF16)
    o += KA_W
    ga_ref[...] = _silu(mm(o, o + QA_W)).astype(BF16)
    o += QA_W
    qb_ref[...] = (mm(o, o + B_W) * Q_SCALE).astype(BF16)
    o += B_W
    kb = mm(o, o + B_W)
    kb_ref[...] = kb.astype(BF16)
    o += B_W
    vb = mm(o, o + B_W)
    vb_ref[...] = vb.astype(BF16)
    o += B_W
    gb_ref[...] = _silu(mm(o, o + B_W)).astype(BF16)

    if cache_refs:
        @pl.when(pl.program_id(0) % tiles_per_seq == tiles_per_seq - 1)
        def _():
            kat_ref, vat_ref, kbt_ref, vbt_ref = cache_refs
            kat_ref[0] = ka[tm - wa_rows:].T
            vat_ref[0] = va[tm - wa_rows:].T
            kbt_ref[0] = kb.T
            vbt_ref[0] = vb.T


def _proj0(x2d, g, w, cos, sin, tm, seq=None, wa_rows=None):
    n = x2d.shape[0]
    n_pos_blocks = cos.shape[0] // tm
    widths = (QA_W, KA_W, KA_W, QA_W, B_W, B_W, B_W, B_W)
    row = lambda i: (i, 0)
    fixed = lambda i: (0, 0)
    pos = lambda i: (i % n_pos_blocks, 0)
    out_specs = [pl.BlockSpec((tm, wd), row) for wd in widths]
    out_shape = [jax.ShapeDtypeStruct((n, wd), BF16) for wd in widths]
    tiles_per_seq = None
    if seq is not None:
        tiles_per_seq = seq // tm
        for feat, keys in ((KA_W, wa_rows), (KA_W, wa_rows), (B_W, tm), (B_W, tm)):
            out_specs.append(pl.BlockSpec((1, feat, keys), lambda i: (i // tiles_per_seq, 0, 0)))
            out_shape.append(jax.ShapeDtypeStruct((n // seq, feat, keys), F32))
    return pl.pallas_call(
        functools.partial(_proj0_body, tiles_per_seq=tiles_per_seq, wa_rows=wa_rows),
        grid=(n // tm,),
        in_specs=[pl.BlockSpec((tm, D_MODEL), row),
                  pl.BlockSpec((1, D_MODEL), fixed),
                  pl.BlockSpec(w.shape, fixed),
                  pl.BlockSpec((tm, LANES), pos),
                  pl.BlockSpec((tm, LANES), pos)],
        out_specs=out_specs,
        out_shape=out_shape,
        compiler_params=_params(1),
        name="proj0",
    )(x2d, g, w, cos, sin)


def _bias_body(r0_ref, bp_ref, bs_ref, band_ref, *, t_sample, keys_sample):
    x = jnp.broadcast_to(r0_ref[0], (GRP, TOEP_W)) * LOG2E
    row = lax.broadcasted_iota(jnp.int32, (GRP, TOEP_W), 0)
    for b in range(GRP.bit_length() - 1):
        x = jnp.where(((row >> b) & 1) == 1, pltpu.roll(x, 1 << b, 1), x)
    x = x[:, :WIN_B]
    r = lax.broadcasted_iota(jnp.int32, (GRP, WIN_B), 0)
    c = lax.broadcasted_iota(jnp.int32, (GRP, WIN_B), 1)
    ci = r // CHUNK
    cj = c // CHUNK
    bp_ref[0] = jnp.where((cj >= ci) & (cj <= ci + N_PREV_B), x, NEG)
    cs = lax.broadcasted_iota(jnp.int32, (t_sample, WIN_B), 1)
    bs_ref[0] = jnp.where(cs < keys_sample, x[:t_sample], NEG)
    ra = lax.broadcasted_iota(jnp.int32, (GRP, WIN_A), 0) // CHUNK
    ca = lax.broadcasted_iota(jnp.int32, (GRP, WIN_A), 1) // CHUNK
    band_ref[...] = jnp.where((ca >= ra) & (ca <= ra + N_PREV_A), 0.0, NEG)


def _build_bias(relpos, t_sample, keys_sample):
    far = relpos[:, 2 * MAX_REL:]
    n_far = PAD_ROWS - MAX_REL + 1
    r0 = jnp.concatenate(
        [jnp.broadcast_to(far, (H_B, n_far)),
         relpos[:, 2 * MAX_REL - 1:0:-1],
         jnp.broadcast_to(far, (H_B, TOEP_W - n_far - (2 * MAX_REL - 1)))], axis=1)
    r0 = r0.reshape(H_B, 1, TOEP_W)
    return pl.pallas_call(
        functools.partial(_bias_body, t_sample=t_sample, keys_sample=keys_sample),
        grid=(H_B,),
        in_specs=[pl.BlockSpec((1, 1, TOEP_W), lambda h: (h, 0, 0))],
        out_specs=[pl.BlockSpec((1, GRP, WIN_B), lambda h: (h, 0, 0)),
                   pl.BlockSpec((1, t_sample, WIN_B), lambda h: (h, 0, 0)),
                   pl.BlockSpec((GRP, WIN_A), lambda h: (0, 0))],
        out_shape=[jax.ShapeDtypeStruct((H_B, GRP, WIN_B), F32),
                   jax.ShapeDtypeStruct((H_B, t_sample, WIN_B), F32),
                   jax.ShapeDtypeStruct((GRP, WIN_A), F32)],
        compiler_params=_params(1),
        name="relbias",
    )(r0)


def _nt_dot(a, b):
    return lax.dot_general(a, b, (((1,), (1,)), ((), ())), preferred_element_type=F32)


def _attend_group(qa, qb, ka, va, kb_ref, vb_ref, win_b, sink_ref, bias_ref, mask_a, col_ok_a,
                  col_ok_b):
    n = qa.shape[0]
    lane = lax.broadcasted_iota(jnp.int32, (n, LANES), 1)
    lo = lane < HEAD_DIM
    zero = jnp.zeros((n, LANES), BF16)
    halves = lambda tile: [jnp.where(lo, tile, zero), jnp.where(lo, zero, tile)]
    n_pairs = B_W // LANES

    def scores(unit):
        if unit == 0:
            lhs = jnp.concatenate([h for p in range(QA_W // LANES)
                                   for h in halves(qa[:, p * LANES:(p + 1) * LANES])], axis=0)
            return _nt_dot(lhs, ka)
        sl = slice((unit - 1) * LANES, unit * LANES)
        return _nt_dot(jnp.concatenate(halves(qb[:, sl]), axis=0), kb_ref[win_b, sl])

    def finish_a(s_all):
        probs, inv = [], []
        for h in range(H_A):
            s = s_all[h * n:(h + 1) * n]
            if mask_a is not None:
                s = s + mask_a
            if col_ok_a is not None:
                s = jnp.where(col_ok_a, s, NEG)
            sink = sink_ref[h] * LOG2E
            m = jnp.maximum(jnp.max(s, axis=-1, keepdims=True), sink)
            e = jnp.exp2(s - m)
            den = jnp.sum(e, axis=-1, keepdims=True) + jnp.exp2(sink - m)
            probs.append(e.astype(BF16))
            inv.append(1.0 / den)
        r_all = jnp.dot(jnp.concatenate(probs, axis=0), va, preferred_element_type=F32)
        out = []
        for p in range(QA_W // LANES):
            r0 = r_all[(2 * p) * n:(2 * p + 1) * n] * inv[2 * p]
            r1 = r_all[(2 * p + 1) * n:(2 * p + 2) * n] * inv[2 * p + 1]
            out.append(jnp.where(lo, r0, r1))
        return jnp.concatenate(out, axis=1)

    def finish_b(p, s2):
        sl = slice(p * LANES, (p + 1) * LANES)
        probs, inv = [], []
        for j in range(2):
            s = s2[j * n:(j + 1) * n] + bias_ref[2 * p + j]
            if col_ok_b is not None:
                s = jnp.where(col_ok_b, s, NEG)
            m = jnp.max(s, axis=-1, keepdims=True)
            e = jnp.exp2(s - m)
            probs.append(e.astype(BF16))
            inv.append(1.0 / jnp.sum(e, axis=-1, keepdims=True))
        r = jnp.dot(jnp.concatenate(probs, axis=0), vb_ref[win_b, sl], preferred_element_type=F32)
        return jnp.where(lo, r[:n] * inv[0], r[n:] * inv[1])

    s_cur = scores(0)
    o_a, o_b = None, []
    for unit in range(n_pairs + 1):
        s_next = scores(unit + 1) if unit < n_pairs else None
        if unit == 0:
            o_a = finish_a(s_cur)
        else:
            o_b.append(finish_b(unit - 1, s_cur))
        s_cur = s_next
    return o_a, jnp.concatenate(o_b, axis=1)


def _mix_out(o_a, o_b, ga, gb, x, w_ref, g_ref):
    mix = jnp.concatenate([o_a * ga.astype(F32), o_b * gb.astype(F32)], axis=1).astype(BF16)
    m = jnp.dot(mix, w_ref[...], preferred_element_type=F32)
    return x + _rms(m, g_ref[...])


def _attn_prompt_body(sink_ref, x_ref, qa_ref, ga_ref, qb_ref, gb_ref, ka_ref, va_ref, kb_ref,
                      vb_ref, bias_ref, band_ref, w_ref, g_ref, y_ref,
                      kap, vap, kbp, vbp, *, seq):
    t = pl.program_id(1)
    tq = x_ref.shape[1]

    @pl.when(t == 0)
    def _():
        for dst, src in ((kap, ka_ref), (vap, va_ref), (kbp, kb_ref), (vbp, vb_ref)):
            dst[:PAD_ROWS, :] = jnp.zeros((PAD_ROWS, dst.shape[1]), BF16)
            dst[PAD_ROWS:, :] = src[0]

    band = band_ref[...]

    def group(sub, masked):
        start = pl.multiple_of(t * tq + sub * GRP, GRP)
        rs = slice(sub * GRP, (sub + 1) * GRP)
        win_a = pl.ds(pl.multiple_of(start + (PAD_ROWS - N_PREV_A * CHUNK), GRP), WIN_A)
        win_b = pl.ds(start, WIN_B)
        ok_a = ok_b = None
        if masked:
            ok_a = lax.broadcasted_iota(jnp.int32, (1, WIN_A), 1) >= N_PREV_A * CHUNK - start
            ok_b = lax.broadcasted_iota(jnp.int32, (1, WIN_B), 1) >= N_PREV_B * CHUNK - start
        o_a, o_b = _attend_group(qa_ref[0, rs, :], qb_ref[0, rs, :], kap[win_a, :], vap[win_a, :],
                                 kbp, vbp, win_b, sink_ref, bias_ref, band, ok_a, ok_b)
        y_ref[0, rs, :] = _mix_out(o_a, o_b, ga_ref[0, rs, :], gb_ref[0, rs, :], x_ref[0, rs, :],
                                   w_ref, g_ref)

    n_masked = PAD_ROWS // tq

    @pl.when(t < n_masked)
    def _():
        for sub in range(tq // GRP):
            group(sub, True)

    @pl.when(t >= n_masked)
    def _():
        for sub in range(tq // GRP):
            group(sub, False)


def _attn_prompt(sinks, x, qa, ga, qb, gb, ka, va, kb, vb, bias, band, w_out, g_post, tq):
    b, seq, _ = x.shape
    tile = lambda w: pl.BlockSpec((1, tq, w), lambda i, j: (i, j, 0))
    whole = lambda w: pl.BlockSpec((1, seq, w), lambda i, j: (i, 0, 0))
    fixed = lambda shape: pl.BlockSpec(shape, lambda i, j: (0,) * len(shape))
    return pl.pallas_call(
        functools.partial(_attn_prompt_body, seq=seq),
        grid=(b, seq // tq),
        in_specs=[pl.BlockSpec(memory_space=pltpu.SMEM),
                  tile(D_MODEL), tile(QA_W), tile(QA_W), tile(B_W), tile(B_W),
                  whole(KA_W), whole(KA_W), whole(B_W), whole(B_W),
                  fixed(bias.shape), fixed(band.shape), fixed(w_out.shape), fixed((1, D_MODEL))],
        out_specs=tile(D_MODEL),
        out_shape=jax.ShapeDtypeStruct(x.shape, F32),
        scratch_shapes=[pltpu.VMEM((PAD_ROWS + seq, KA_W), BF16),
                        pltpu.VMEM((PAD_ROWS + seq, KA_W), BF16),
                        pltpu.VMEM((PAD_ROWS + seq, B_W), BF16),
                        pltpu.VMEM((PAD_ROWS + seq, B_W), BF16)],
        compiler_params=_params(2),
        name="attn_prompt",
    )(sinks, x, qa, ga, qb, gb, ka, va, kb, vb, bias, band, w_out, g_post)


SAMPLE_REQS = 4


def _softmax2(s_c, s_n, sink):
    m = jnp.maximum(jnp.max(s_c, axis=-1, keepdims=True), jnp.max(s_n, axis=-1, keepdims=True))
    if sink is not None:
        m = jnp.maximum(m, sink)
    e_c = jnp.exp2(s_c - m)
    e_n = jnp.exp2(s_n - m)
    den = jnp.sum(e_c, axis=-1, keepdims=True) + jnp.sum(e_n, axis=-1, keepdims=True)
    if sink is not None:
        den = den + jnp.exp2(sink - m)
    return e_c.astype(BF16), e_n.astype(BF16), 1.0 / den


def _roll_in(cache, new):
    t, w = new.shape[0], cache.shape[1]
    lane = lax.broadcasted_iota(jnp.int32, (cache.shape[0], LANES), 1)
    padded = jnp.concatenate([jnp.zeros((LANES - t, new.shape[1]), F32), new.astype(F32)], axis=0)
    rolled = pltpu.roll(cache, w - t, 1)
    last = jnp.where(lane >= LANES - t, padded.T, rolled[:, w - LANES:])
    return last if w == LANES else jnp.concatenate([rolled[:, :w - LANES], last], axis=1)


def _attn_sample_body(sink_ref, x_ref, qa_ref, ga_ref, qb_ref, gb_ref, ka_ref, va_ref, kb_ref,
                      vb_ref, cak_ref, cav_ref, cbk_ref, cbv_ref, bias_ref, w_ref, g_ref,
                      y_ref, nak_ref, nav_ref, nbk_ref, nbv_ref):
    nreq, t, _ = x_ref.shape
    for r in range(nreq):
        for dst, cache, new in ((nak_ref, cak_ref, ka_ref), (nav_ref, cav_ref, va_ref),
                                (nbk_ref, cbk_ref, kb_ref), (nbv_ref, cbv_ref, vb_ref)):
            dst[r] = _roll_in(cache[r], new[r])

    wb = cbk_ref.shape[2]
    lane = lax.broadcasted_iota(jnp.int32, (t, LANES), 1)
    lo = lane < HEAD_DIM
    zero = jnp.zeros((t, LANES), BF16)
    halves = lambda tile: [jnp.where(lo, tile, zero), jnp.where(lo, zero, tile)]
    n_pairs = B_W // LANES

    def scores(r, unit):
        if unit == 0:
            qa = qa_ref[r]
            lhs = jnp.concatenate([h for p in range(QA_W // LANES)
                                   for h in halves(qa[:, p * LANES:(p + 1) * LANES])], axis=0)
            return (jnp.dot(lhs, cak_ref[r].astype(BF16), preferred_element_type=F32),
                    _nt_dot(lhs, ka_ref[r]))
        sl = slice((unit - 1) * LANES, unit * LANES)
        lhs = jnp.concatenate(halves(qb_ref[r][:, sl]), axis=0)
        return (jnp.dot(lhs, cbk_ref[r, sl, :].astype(BF16), preferred_element_type=F32),
                _nt_dot(lhs, kb_ref[r, :, sl]))

    def finish(r, unit, s):
        s_c, s_n = s
        if unit == 0:
            cache_v, new_v = cav_ref[r].astype(BF16), va_ref[r]
            heads = [(h, sink_ref[h] * LOG2E, None) for h in range(H_A)]
        else:
            sl = slice((unit - 1) * LANES, unit * LANES)
            cache_v, new_v = cbv_ref[r, sl, :].astype(BF16), vb_ref[r, :, sl]
            heads = [(j, None, bias_ref[2 * (unit - 1) + j]) for j in range(2)]
        probs_c, probs_n, inv = [], [], []
        for j, sink, bias in heads:
            rs = slice(j * t, (j + 1) * t)
            sc, sn = s_c[rs], s_n[rs]
            if bias is not None:
                sc, sn = sc + bias[:, :wb], sn + bias[:, wb:wb + t]
            e_c, e_n, iv = _softmax2(sc, sn, sink)
            probs_c.append(e_c)
            probs_n.append(e_n)
            inv.append(iv)
        o = (_nt_dot(jnp.concatenate(probs_c, axis=0), cache_v)
             + jnp.dot(jnp.concatenate(probs_n, axis=0), new_v, preferred_element_type=F32))
        tiles = []
        for p in range(len(heads) // 2):
            o0 = o[(2 * p) * t:(2 * p + 1) * t] * inv[2 * p]
            o1 = o[(2 * p + 1) * t:(2 * p + 2) * t] * inv[2 * p + 1]
            tiles.append(jnp.where(lo, o0, o1))
        return tiles

    order = [(r, u) for r in range(nreq) for u in range(n_pairs + 1)]
    pending = {order[0]: scores(*order[0])}
    mix, tiles = [], []
    for i, (r, u) in enumerate(order):
        if i + 1 < len(order):
            pending[order[i + 1]] = scores(*order[i + 1])
        tiles += finish(r, u, pending.pop((r, u)))
        if u == n_pairs:
            gate = jnp.concatenate([ga_ref[r], gb_ref[r]], axis=1).astype(F32)
            mix.append((jnp.concatenate(tiles, axis=1) * gate).astype(BF16))
            tiles = []
    m = jnp.dot(jnp.concatenate(mix, axis=0), w_ref[...], preferred_element_type=F32)
    y = x_ref[...].reshape(nreq * t, D_MODEL) + _rms(m, g_ref[...])
    y_ref[...] = y.reshape(nreq, t, D_MODEL)


def _attn_sample(sinks, x, qa, ga, qb, gb, ka, va, kb, vb, cak, cav, cbk, cbv, bias, w_out, g_post):
    b = x.shape[0]
    per = lambda a: pl.BlockSpec((SAMPLE_REQS,) + a.shape[1:], lambda i: (i, 0, 0))
    fixed = lambda shape: pl.BlockSpec(shape, lambda i: (0,) * len(shape))
    arrs = (x, qa, ga, qb, gb, ka, va, kb, vb, cak, cav, cbk, cbv)
    return pl.pallas_call(
        _attn_sample_body,
        grid=(b // SAMPLE_REQS,),
        in_specs=[pl.BlockSpec(memory_space=pltpu.SMEM)] + [per(a) for a in arrs]
                 + [fixed(bias.shape), fixed(w_out.shape), fixed((1, D_MODEL))],
        out_specs=[per(a) for a in (x, cak, cav, cbk, cbv)],
        out_shape=[jax.ShapeDtypeStruct(a.shape, F32) for a in (x, cak, cav, cbk, cbv)],
        compiler_params=_params(1),
        name="attn_sample",
    )(sinks, *arrs, bias, w_out, g_post)


LRU_SEG = 8
LRU_STEPS = 32
LRU_TILE = LRU_SEG * LRU_STEPS
N_TAIL = CONV_W - 1
LRU_TILES_PER_STEP = 4
LRU_CB = D_LRU
N_CB = D_LRU // LRU_CB


def _lru_block(xb, z, tail, carry_in, cb, chained, w, after_gates):
    cw_ref, cb_ref, wa_ref, ba_ref, wx_ref, bx_ref, lam_ref = w[2:9]
    cs = slice(cb * LRU_CB, (cb + 1) * LRU_CB)
    grp = lambda v, g, n=1: v[g * LRU_SEG:(g + n) * LRU_SEG]

    if chained:
        sub = lax.broadcasted_iota(jnp.int32, (LRU_SEG, LRU_CB), 0)
        before = [jnp.where(sub == 0, pltpu.roll(grp(tail, j), 1, 0),
                            pltpu.roll(grp(xb, LRU_STEPS - N_TAIL + j), 1, 0))
                  for j in range(N_TAIL)]
    else:
        before = tail
    xb_tail = grp(xb, LRU_STEPS - N_TAIL, N_TAIL)

    cw = cw_ref[:, cs]
    xc = cb_ref[:, cs] + xb * cw[CONV_W - 1:CONV_W]
    for k in range(1, CONV_W):
        shifted = jnp.concatenate(before[N_TAIL - k:] + [grp(xb, 0, LRU_STEPS - k)], axis=0)
        xc = xc + shifted * cw[CONV_W - 1 - k:CONV_W - k]

    xcb = xc.astype(BF16)
    per = LRU_CB // MXU_DIM

    both = [jnp.dot(xcb[:, j * MXU_DIM:(j + 1) * MXU_DIM],
                    jnp.concatenate([wa_ref[cb * per + j], wx_ref[cb * per + j]], axis=1),
                    preferred_element_type=F32) for j in range(per)]
    pre_r = jnp.concatenate([b[:, :MXU_DIM] for b in both], axis=1)
    pre_i = jnp.concatenate([b[:, MXU_DIM:] for b in both], axis=1)
    after_gates()
    r = jax.nn.sigmoid(pre_r + ba_ref[:, cs])
    gi = jax.nn.sigmoid(pre_i + bx_ref[:, cs])
    lam = lam_ref[:, cs]
    log_sig = jnp.minimum(lam, 0.0) - jnp.log1p(jnp.exp(-jnp.abs(lam)))
    log_a = r * (C_GATE * log_sig)
    a = jnp.exp(log_a)
    u = jnp.sqrt(jnp.tanh(-log_a) * (a * a + 1.0)) * (gi * xc)

    hc, ac = grp(u, 0), grp(a, 0)
    h_loc, a_loc = [hc], [ac]
    for g in range(1, LRU_STEPS):
        ag = grp(a, g)
        hc = ag * hc + grp(u, g)
        ac = ag * ac
        h_loc.append(hc)
        a_loc.append(ac)

    if chained:
        c = carry_in
        rows = []
        for s in range(LRU_SEG):
            rows.append(c)
            c = ac[s:s + 1] * c + hc[s:s + 1]
        carry = jnp.concatenate(rows, axis=0)
    else:
        c = None
        carry = carry_in
    h_groups = [hl + al * carry for hl, al in zip(h_loc, a_loc)]
    yl = (jnp.concatenate(h_groups, axis=0) * _silu(z)).astype(BF16)
    yl = jnp.swapaxes(yl.reshape(LRU_STEPS, LRU_SEG, LRU_CB), 0, 1)
    return yl.reshape(LRU_TILE, LRU_CB), c, xb_tail, h_groups[-1]


def _lru_run(load_x, store_y, n_tiles, tails, carries, chained, w):
    gpre_ref, win_ref = w[:2]
    wout_ref, gpost_ref = w[9:11]
    hp = {}

    def in_proj(unit):
        ti, cb = divmod(unit, N_CB)
        if ti not in hp:
            h = _rms(load_x(ti), gpre_ref[...]).astype(BF16)
            h = jnp.swapaxes(h.reshape(LRU_SEG, LRU_STEPS, D_MODEL), 0, 1)
            hp[ti] = h.reshape(LRU_TILE, D_MODEL)
        lo = cb * LRU_CB
        xb = jnp.dot(hp[ti], win_ref[:, lo:lo + LRU_CB], preferred_element_type=F32)
        z = jnp.dot(hp[ti], win_ref[:, D_LRU + lo:D_LRU + lo + LRU_CB], preferred_element_type=F32)
        return xb, z

    units = n_tiles * N_CB
    ready = {0: in_proj(0)}
    h_last = [None] * N_CB
    m = None
    for unit in range(units):
        def emit_next(unit=unit):
            if unit + 1 < units:
                ready[unit + 1] = in_proj(unit + 1)

        ti, cb = divmod(unit, N_CB)
        xb, z = ready.pop(unit)
        yl, carries[cb], tails[cb], h_last[cb] = _lru_block(xb, z, tails[cb], carries[cb], cb,
                                                            chained, w, emit_next)
        part = jnp.dot(yl, wout_ref[cb * LRU_CB:(cb + 1) * LRU_CB, :], preferred_element_type=F32)
        m = part if cb == 0 else m + part
        if cb == N_CB - 1:
            store_y(ti, load_x(ti) + _rms(m, gpost_ref[...]))
    return h_last


def _lru_body(x_ref, h0_ref, c0_ref, *rest, chained):
    w, (y_ref, hl_ref, cl_ref, h_s, tail_s) = rest[:-5], rest[-5:]
    t = pl.program_id(1)
    n_tiles = x_ref.shape[1] // LRU_TILE
    blocks = [slice(cb * LRU_CB, (cb + 1) * LRU_CB) for cb in range(N_CB)]
    rows = lambda ti: slice(ti * LRU_TILE, (ti + 1) * LRU_TILE)
    load_x = lambda ti: x_ref[0, rows(ti), :]

    def store_y(ti, y):
        y_ref[0, rows(ti), :] = y

    if not chained:
        tails = [[c0_ref[j, :, cs] for j in range(N_TAIL)] for cs in blocks]
        carries = [h0_ref[0, :, cs] for cs in blocks]
        h_last = _lru_run(load_x, store_y, n_tiles, tails, carries, False, w)
        hl_ref[0] = jnp.concatenate(h_last, axis=1)
        cl_ref[...] = jnp.concatenate(tails, axis=1).reshape(N_TAIL, LRU_SEG, D_LRU)
        return

    @pl.when(t == 0)
    def _():
        h_s[...] = h0_ref[0]
        tail_s[...] = jnp.zeros(tail_s.shape, F32)
        for j in range(N_TAIL):
            tail_s[j * LRU_SEG + LRU_SEG - 1:(j + 1) * LRU_SEG, :] = c0_ref[0, j:j + 1, :]

    tails = [tail_s[:, cs] for cs in blocks]
    carries = [h_s[:, cs] for cs in blocks]
    _lru_run(load_x, store_y, n_tiles, tails, carries, True, w)
    c = jnp.concatenate(carries, axis=1)
    tail = jnp.concatenate(tails, axis=1)
    h_s[...] = c
    tail_s[...] = tail

    @pl.when(t == pl.num_programs(1) - 1)
    def _():
        hl_ref[0] = c
        last = LRU_SEG - 1
        cl_ref[0] = jnp.concatenate(
            [tail[j * LRU_SEG + last:(j + 1) * LRU_SEG] for j in range(N_TAIL)], axis=0)


def _lru(x, h0, c0, weights, chained):
    nb, seq, _ = x.shape
    fixed = lambda a: pl.BlockSpec(a.shape, lambda i, j: (0,) * a.ndim)
    if chained:
        state_specs = [pl.BlockSpec((1, 1, D_LRU), lambda i, j: (i, 0, 0)),
                       pl.BlockSpec((1, N_TAIL, D_LRU), lambda i, j: (i, 0, 0))]
        state_shapes = [jax.ShapeDtypeStruct((nb, 1, D_LRU), F32),
                        jax.ShapeDtypeStruct((nb, N_TAIL, D_LRU), F32)]
    else:
        assert seq == LRU_TILE
        state_specs = [pl.BlockSpec((1, LRU_SEG, D_LRU), lambda i, j: (i, 0, 0)),
                       pl.BlockSpec((N_TAIL, LRU_SEG, D_LRU), lambda i, j: (0, i, 0))]
        state_shapes = [jax.ShapeDtypeStruct((nb, LRU_SEG, D_LRU), F32),
                        jax.ShapeDtypeStruct((N_TAIL, nb * LRU_SEG, D_LRU), F32)]
    rows = LRU_TILE * (LRU_TILES_PER_STEP if chained else 1)
    tile = pl.BlockSpec((1, rows, D_MODEL), lambda i, j: (i, j, 0))
    return pl.pallas_call(
        functools.partial(_lru_body, chained=chained),
        grid=(nb, seq // rows),
        in_specs=[tile] + state_specs + [fixed(a) for a in weights],
        out_specs=[tile] + state_specs,
        out_shape=[jax.ShapeDtypeStruct(x.shape, F32)] + state_shapes,
        scratch_shapes=[pltpu.VMEM((1, D_LRU), F32), pltpu.VMEM((N_TAIL * LRU_SEG, D_LRU), F32)],
        compiler_params=_params(2),
        name="lru_chained" if chained else "lru_batched",
    )(x, h0, c0, *weights)


def _rope_tables(pos):
    half = HEAD_DIM // 2
    inv = ROPE_THETA ** (-jnp.arange(half, dtype=F32) / half)
    ang = pos.astype(F32)[:, None] * inv[None, :]
    cos = jnp.tile(jnp.cos(ang), (1, LANES // half))
    sin = jnp.sin(ang)
    sin = jnp.tile(jnp.concatenate([-sin, sin], axis=1), (1, LANES // HEAD_DIM))
    return cos, sin


def _interleave_groups(w, axis, unit=HEAD_DIM):
    shape = w.shape
    per = H_A // KV_A
    w = w.reshape(shape[:axis] + (KV_A, per, unit) + shape[axis + 1:])
    w = jnp.swapaxes(w, axis, axis + 1)
    return w.reshape(shape)


def _block_diag(w):
    per = MXU_DIM // BLOCK
    w = w.reshape(N_BLOCKS // per, per, BLOCK, BLOCK)
    eye = jnp.eye(per, dtype=w.dtype)
    w = w[:, :, :, None, :] * eye[None, :, None, :, None]
    return w.reshape(N_BLOCKS // per, MXU_DIM, MXU_DIM)


def kernel(x_prompt, x_sample, cache_a_k, cache_a_v, cache_b_k, cache_b_v, state_c_h, state_c_conv,
           ln_pre, ln_post, w_in_ab, sinks_a, relpos_b, w_out_ab, w_in_c, conv_c_w, conv_c_b,
           gate_c_wa, gate_c_ba, gate_c_wx, gate_c_bx, lambda_c, w_out_c):
    bp, s_len, _ = x_prompt.shape
    bs, t_s, _ = x_sample.shape
    wa_rows = cache_a_k.shape[2]
    wb_rows = cache_b_k.shape[2]
    assert wa_rows + t_s <= WIN_A and wb_rows + t_s <= WIN_B and wb_rows == PAD_ROWS

    w_in = w_in_ab[0]
    w_in = jnp.concatenate([_interleave_groups(w_in[:, :QA_W], 1), w_in[:, QA_W:QA_W + 2 * KA_W],
                            _interleave_groups(w_in[:, QA_W + 2 * KA_W:2 * QA_W + 2 * KA_W], 1),
                            w_in[:, 2 * QA_W + 2 * KA_W:]], axis=1).astype(BF16)
    w_out = jnp.concatenate([_interleave_groups(w_out_ab[0, :QA_W], 0), w_out_ab[0, QA_W:]],
                            axis=0).astype(BF16)
    sinks = _interleave_groups(sinks_a[0], 0, unit=1)
    g_pre0 = ln_pre[0].reshape(1, D_MODEL)
    g_post0 = ln_post[0].reshape(1, D_MODEL)
    bias_p, bias_s, band = _build_bias(relpos_b[0], t_s, wb_rows + t_s)

    cos_p, sin_p = _rope_tables(jnp.arange(s_len, dtype=jnp.int32))
    assert wb_rows == PROJ_TM and wa_rows <= PROJ_TM and s_len % PROJ_TM == 0
    proj = _proj0(x_prompt.reshape(bp * s_len, D_MODEL), g_pre0, w_in, cos_p, sin_p, PROJ_TM,
                  seq=s_len, wa_rows=wa_rows)
    qa, ka, va, ga, qb, kb, vb, gb = [a.reshape(bp, s_len, a.shape[-1]) for a in proj[:8]]
    y0_p = _attn_prompt(sinks, x_prompt, qa, ga, qb, gb, ka, va, kb, vb, bias_p, band, w_out,
                        g_post0, ATT_TQ)
    tmaj = lambda c, heads: jnp.transpose(
        c.reshape(c.shape[0], heads, HEAD_DIM, c.shape[2]), (0, 3, 1, 2))[None]
    nak_p, nav_p = tmaj(proj[8], KV_A), tmaj(proj[9], KV_A)
    nbk_p, nbv_p = tmaj(proj[10], H_B), tmaj(proj[11], H_B)

    n_s = bs * t_s
    tm_s = min(PROJ_TM, n_s)
    cos_s, sin_s = _rope_tables(PAST_LEN + jnp.arange(t_s, dtype=jnp.int32))
    cos_s = jnp.tile(cos_s, (tm_s // t_s, 1))
    sin_s = jnp.tile(sin_s, (tm_s // t_s, 1))
    proj = _proj0(x_sample.reshape(n_s, D_MODEL), g_pre0, w_in, cos_s, sin_s, tm_s)
    qa, ka, va, ga, qb, kb, vb, gb = [a.reshape(bs, t_s, a.shape[-1]) for a in proj]
    fmaj = lambda c: jnp.transpose(c[0], (0, 2, 3, 1))
    cak, cav, cbk, cbv = fmaj(cache_a_k), fmaj(cache_a_v), fmaj(cache_b_k), fmaj(cache_b_v)
    flat = lambda c: c.reshape(bs, c.shape[1] * HEAD_DIM, c.shape[3])
    y0_s, nak_s, nav_s, nbk_s, nbv_s = _attn_sample(
        sinks, x_sample, qa, ga, qb, gb, ka, va, kb, vb, flat(cak), flat(cav), flat(cbk), flat(cbv),
        bias_s, w_out, g_post0)
    nak_s, nav_s = tmaj(nak_s, KV_A), tmaj(nav_s, KV_A)
    nbk_s, nbv_s = tmaj(nbk_s, H_B), tmaj(nbv_s, H_B)

    row = lambda v: v.reshape(1, -1)
    lru_w = (row(ln_pre[1]), w_in_c[0].astype(BF16), conv_c_w[0], row(conv_c_b[0]),
             _block_diag(gate_c_wa[0]).astype(BF16), row(gate_c_ba[0]),
             _block_diag(gate_c_wx[0]).astype(BF16), row(gate_c_bx[0]),
             row(lambda_c[0]), w_out_c[0].astype(BF16), row(ln_post[1]))
    y1_p, hl_p, cl_p = _lru(y0_p, jnp.zeros((bp, 1, D_LRU), F32), jnp.zeros((bp, N_TAIL, D_LRU), F32),
                            lru_w, chained=True)
    assert t_s == LRU_STEPS and bs % LRU_SEG == 0
    y1_s, hl_s, cl_s = _lru(y0_s.reshape(bs // LRU_SEG, LRU_TILE, D_MODEL),
                            state_c_h[0].reshape(bs // LRU_SEG, LRU_SEG, D_LRU),
                            jnp.transpose(state_c_conv[0], (1, 0, 2)), lru_w, chained=False)
    y1_s = y1_s.reshape(bs, t_s, D_MODEL)
    cl_s = jnp.transpose(cl_s, (1, 0, 2))

    return (y1_p, y1_s, nak_p, nav_p, nbk_p, nbv_p,
            hl_p.reshape(1, bp, D_LRU), cl_p.reshape(1, bp, N_TAIL, D_LRU),
            nak_s, nav_s, nbk_s, nbv_s,
            hl_s.reshape(1, bs, D_LRU), cl_s.reshape(1, bs, N_TAIL, D_LRU))
```

```python
import functools

import jax
import jax.numpy as jnp
from jax import lax
from jax.experimental import pallas as pl
from jax.experimental.pallas import tpu as pltpu

F32 = jnp.float32
BF16 = jnp.bfloat16

D_MODEL = 1024
CHUNK = 64
HEAD_DIM = 64
H_A = 8
KV_A = 2
N_PREV_A = 2
H_B = 8
N_PREV_B = 8
MAX_REL = 128
ROPE_THETA = 10000.0
D_LRU = D_MODEL
N_BLOCKS = 16
BLOCK = D_LRU // N_BLOCKS
CONV_W = 4
C_GATE = 8.0
EPS = 1e-6
PAST_LEN = 1024
NEG = -1e30
LOG2E = 1.4426950408889634
Q_SCALE = HEAD_DIM ** -0.5 * LOG2E

LANES = 128
MXU_DIM = 256
VMEM_LIMIT = 56 * 1024 * 1024

QA_W = H_A * HEAD_DIM
KA_W = KV_A * HEAD_DIM
B_W = H_B * HEAD_DIM
GRP = 2 * CHUNK
WIN_A = (N_PREV_A + 2) * CHUNK
WIN_B = (N_PREV_B + 2) * CHUNK
PAD_ROWS = N_PREV_B * CHUNK
TOEP_W = WIN_B + LANES

PROJ_TM = 512
ATT_TQ = 512


def _params(n_axes):
    return pltpu.CompilerParams(
        dimension_semantics=("arbitrary",) * n_axes,
        vmem_limit_bytes=VMEM_LIMIT)


def _rms(x, g):
    ms = jnp.mean(x * x, axis=-1, keepdims=True)
    return x * lax.rsqrt(ms + EPS) * g


def _silu(x):
    return x * jax.nn.sigmoid(x)


def _proj0_body(x_ref, g_ref, w_ref, cos_ref, sin_ref,
                qa_ref, ka_ref, va_ref, ga_ref, qb_ref, kb_ref, vb_ref, gb_ref, *cache_refs,
                tiles_per_seq, wa_rows):
    h = _rms(x_ref[...], g_ref[...]).astype(BF16)
    cos = cos_ref[...]
    sin = sin_ref[...]
    tm = cos.shape[0]
    lane = lax.broadcasted_iota(jnp.int32, (tm, LANES), 1)
    lower = (lane & (HEAD_DIM - 1)) < (HEAD_DIM // 2)

    def mm(lo, hi):
        return jnp.dot(h, w_ref[:, lo:hi], preferred_element_type=F32)

    def rope(t):
        partner = jnp.where(lower, pltpu.roll(t, LANES - HEAD_DIM // 2, 1),
                            pltpu.roll(t, HEAD_DIM // 2, 1))
        return t * cos + partner * sin

    o = 0
    t = mm(o, o + QA_W)
    for k in range(QA_W // LANES):
        sl = slice(k * LANES, (k + 1) * LANES)
        qa_ref[:, sl] = (rope(t[:, sl]) * Q_SCALE).astype(BF16)
    o += QA_W
    ka = rope(mm(o, o + KA_W))
    ka_ref[...] = ka.astype(BF16)
    o += KA_W
    va = mm(o, o + KA_W)
    va_ref[...] = va.astype(BF16)
    o += KA_W
    ga_ref[...] = _silu(mm(o, o + QA_W)).astype(BF16)
    o += QA_W
    qb_ref[...] = (mm(o, o + B_W) * Q_SCALE).astype(BF16)
    o += B_W
    kb = mm(o, o + B_W)
    kb_ref[...] = kb.astype(BF16)
    o += B_W
    vb = mm(o, o + B_W)
    vb_ref[...] = vb.astype(BF16)
    o += B_W
    gb_ref[...] = _silu(mm(o, o + B_W)).astype(BF16)

    if cache_refs:
        @pl.when(pl.program_id(0) % tiles_per_seq == tiles_per_seq - 1)
        def _():
            kat_ref, vat_ref, kbt_ref, vbt_ref = cache_refs
            kat_ref[0] = ka[tm - wa_rows:].T
            vat_ref[0] = va[tm - wa_rows:].T
            kbt_ref[0] = kb.T
            vbt_ref[0] = vb.T


def _proj0(x2d, g, w, cos, sin, tm, seq=None, wa_rows=None):
    n = x2d.shape[0]
    n_pos_blocks = cos.shape[0] // tm
    widths = (QA_W, KA_W, KA_W, QA_W, B_W, B_W, B_W, B_W)
    row = lambda i: (i, 0)
    fixed = lambda i: (0, 0)
    pos = lambda i: (i % n_pos_blocks, 0)
    out_specs = [pl.BlockSpec((tm, wd), row) for wd in widths]
    out_shape = [jax.ShapeDtypeStruct((n, wd), BF16) for wd in widths]
    tiles_per_seq = None
    if seq is not None:
        tiles_per_seq = seq // tm
        for feat, keys in ((KA_W, wa_rows), (KA_W, wa_rows), (B_W, tm), (B_W, tm)):
            out_specs.append(pl.BlockSpec((1, feat, keys), lambda i: (i // tiles_per_seq, 0, 0)))
            out_shape.append(jax.ShapeDtypeStruct((n // seq, feat, keys), F32))
    return pl.pallas_call(
        functools.partial(_proj0_body, tiles_per_seq=tiles_per_seq, wa_rows=wa_rows),
        grid=(n // tm,),
        in_specs=[pl.BlockSpec((tm, D_MODEL), row),
                  pl.BlockSpec((1, D_MODEL), fixed),
                  pl.BlockSpec(w.shape, fixed),
                  pl.BlockSpec((tm, LANES), pos),
                  pl.BlockSpec((tm, LANES), pos)],
        out_specs=out_specs,
        out_shape=out_shape,
        compiler_params=_params(1),
        name="proj0",
    )(x2d, g, w, cos, sin)


def _bias_body(r0_ref, bp_ref, bs_ref, band_ref, *, t_sample, keys_sample):
    x = jnp.broadcast_to(r0_ref[0], (GRP, TOEP_W)) * LOG2E
    row = lax.broadcasted_iota(jnp.int32, (GRP, TOEP_W), 0)
    for b in range(GRP.bit_length() - 1):
        x = jnp.where(((row >> b) & 1) == 1, pltpu.roll(x, 1 << b, 1), x)
    x = x[:, :WIN_B]
    r = lax.broadcasted_iota(jnp.int32, (GRP, WIN_B), 0)
    c = lax.broadcasted_iota(jnp.int32, (GRP, WIN_B), 1)
    ci = r // CHUNK
    cj = c // CHUNK
    bp_ref[0] = jnp.where((cj >= ci) & (cj <= ci + N_PREV_B), x, NEG)
    cs = lax.broadcasted_iota(jnp.int32, (t_sample, WIN_B), 1)
    bs_ref[0] = jnp.where(cs < keys_sample, x[:t_sample], NEG)
    ra = lax.broadcasted_iota(jnp.int32, (GRP, WIN_A), 0) // CHUNK
    ca = lax.broadcasted_iota(jnp.int32, (GRP, WIN_A), 1) // CHUNK
    band_ref[...] = jnp.where((ca >= ra) & (ca <= ra + N_PREV_A), 0.0, NEG)


def _build_bias(relpos, t_sample, keys_sample):
    far = relpos[:, 2 * MAX_REL:]
    n_far = PAD_ROWS - MAX_REL + 1
    r0 = jnp.concatenate(
        [jnp.broadcast_to(far, (H_B, n_far)),
         relpos[:, 2 * MAX_REL - 1:0:-1],
         jnp.broadcast_to(far, (H_B, TOEP_W - n_far - (2 * MAX_REL - 1)))], axis=1)
    r0 = r0.reshape(H_B, 1, TOEP_W)
    return pl.pallas_call(
        functools.partial(_bias_body, t_sample=t_sample, keys_sample=keys_sample),
        grid=(H_B,),
        in_specs=[pl.BlockSpec((1, 1, TOEP_W), lambda h: (h, 0, 0))],
        out_specs=[pl.BlockSpec((1, GRP, WIN_B), lambda h: (h, 0, 0)),
                   pl.BlockSpec((1, t_sample, WIN_B), lambda h: (h, 0, 0)),
                   pl.BlockSpec((GRP, WIN_A), lambda h: (0, 0))],
        out_shape=[jax.ShapeDtypeStruct((H_B, GRP, WIN_B), F32),
                   jax.ShapeDtypeStruct((H_B, t_sample, WIN_B), F32),
                   jax.ShapeDtypeStruct((GRP, WIN_A), F32)],
        compiler_params=_params(1),
        name="relbias",
    )(r0)


def _nt_dot(a, b):
    return lax.dot_general(a, b, (((1,), (1,)), ((), ())), preferred_element_type=F32)


def _attend_group(qa, qb, ka, va, kb_ref, vb_ref, win_b, sink_ref, bias_ref, mask_a, col_ok_a,
                  col_ok_b):
    n = qa.shape[0]
    lane = lax.broadcasted_iota(jnp.int32, (n, LANES), 1)
    lo = lane < HEAD_DIM
    zero = jnp.zeros((n, LANES), BF16)
    halves = lambda tile: [jnp.where(lo, tile, zero), jnp.where(lo, zero, tile)]
    n_pairs = B_W // LANES

    def scores(unit):
        if unit == 0:
            lhs = jnp.concatenate([h for p in range(QA_W // LANES)
                                   for h in halves(qa[:, p * LANES:(p + 1) * LANES])], axis=0)
            return _nt_dot(lhs, ka)
        sl = slice((unit - 1) * LANES, unit * LANES)
        return _nt_dot(jnp.concatenate(halves(qb[:, sl]), axis=0), kb_ref[win_b, sl])

    def finish_a(s_all):
        probs, inv = [], []
        for h in range(H_A):
            s = s_all[h * n:(h + 1) * n]
            if mask_a is not None:
                s = s + mask_a
            if col_ok_a is not None:
                s = jnp.where(col_ok_a, s, NEG)
            sink = sink_ref[h] * LOG2E
            m = jnp.maximum(jnp.max(s, axis=-1, keepdims=True), sink)
            e = jnp.exp2(s - m)
            den = jnp.sum(e, axis=-1, keepdims=True) + jnp.exp2(sink - m)
            probs.append(e.astype(BF16))
            inv.append(1.0 / den)
        r_all = jnp.dot(jnp.concatenate(probs, axis=0), va, preferred_element_type=F32)
        out = []
        for p in range(QA_W // LANES):
            r0 = r_all[(2 * p) * n:(2 * p + 1) * n] * inv[2 * p]
            r1 = r_all[(2 * p + 1) * n:(2 * p + 2) * n] * inv[2 * p + 1]
            out.append(jnp.where(lo, r0, r1))
        return jnp.concatenate(out, axis=1)

    def finish_b(p, s2):
        sl = slice(p * LANES, (p + 1) * LANES)
        probs, inv = [], []
        for j in range(2):
            s = s2[j * n:(j + 1) * n] + bias_ref[2 * p + j]
            if col_ok_b is not None:
                s = jnp.where(col_ok_b, s, NEG)
            m = jnp.max(s, axis=-1, keepdims=True)
            e = jnp.exp2(s - m)
            probs.append(e.astype(BF16))
            inv.append(1.0 / jnp.sum(e, axis=-1, keepdims=True))
        r = jnp.dot(jnp.concatenate(probs, axis=0), vb_ref[win_b, sl], preferred_element_type=F32)
        return jnp.where(lo, r[:n] * inv[0], r[n:] * inv[1])

    s_cur = scores(0)
    o_a, o_b = None, []
    for unit in range(n_pairs + 1):
        s_next = scores(unit + 1) if unit < n_pairs else None
        if unit == 0:
            o_a = finish_a(s_cur)
        else:
            o_b.append(finish_b(unit - 1, s_cur))
        s_cur = s_next
    return o_a, jnp.concatenate(o_b, axis=1)


def _mix_out(o_a, o_b, ga, gb, x, w_ref, g_ref):
    mix = jnp.concatenate([o_a * ga.astype(F32), o_b * gb.astype(F32)], axis=1).astype(BF16)
    m = jnp.dot(mix, w_ref[...], preferred_element_type=F32)
    return x + _rms(m, g_ref[...])


def _attn_prompt_body(sink_ref, x_ref, qa_ref, ga_ref, qb_ref, gb_ref, ka_ref, va_ref, kb_ref,
                      vb_ref, bias_ref, band_ref, w_ref, g_ref, y_ref,
                      kap, vap, kbp, vbp):
    t = pl.program_id(1)
    tq = x_ref.shape[1]

    @pl.when(t == 0)
    def _():
        for dst, src in ((kap, ka_ref), (vap, va_ref), (kbp, kb_ref), (vbp, vb_ref)):
            dst[:PAD_ROWS, :] = jnp.zeros((PAD_ROWS, dst.shape[1]), BF16)
            dst[PAD_ROWS:, :] = src[0]

    band = band_ref[...]

    def group(sub, masked):
        start = pl.multiple_of(t * tq + sub * GRP, GRP)
        rs = slice(sub * GRP, (sub + 1) * GRP)
        win_a = pl.ds(pl.multiple_of(start + (PAD_ROWS - N_PREV_A * CHUNK), GRP), WIN_A)
        win_b = pl.ds(start, WIN_B)
        ok_a = ok_b = None
        if masked:
            ok_a = lax.broadcasted_iota(jnp.int32, (1, WIN_A), 1) >= N_PREV_A * CHUNK - start
            ok_b = lax.broadcasted_iota(jnp.int32, (1, WIN_B), 1) >= N_PREV_B * CHUNK - start
        o_a, o_b = _attend_group(qa_ref[0, rs, :], qb_ref[0, rs, :], kap[win_a, :], vap[win_a, :],
                                 kbp, vbp, win_b, sink_ref, bias_ref, band, ok_a, ok_b)
        y_ref[0, rs, :] = _mix_out(o_a, o_b, ga_ref[0, rs, :], gb_ref[0, rs, :], x_ref[0, rs, :],
                                   w_ref, g_ref)

    n_masked = PAD_ROWS // tq

    @pl.when(t < n_masked)
    def _():
        for sub in range(tq // GRP):
            group(sub, True)

    @pl.when(t >= n_masked)
    def _():
        for sub in range(tq // GRP):
            group(sub, False)


def _attn_prompt(sinks, x, qa, ga, qb, gb, ka, va, kb, vb, bias, band, w_out, g_post, tq):
    b, seq, _ = x.shape
    tile = lambda w: pl.BlockSpec((1, tq, w), lambda i, j: (i, j, 0))
    whole = lambda w: pl.BlockSpec((1, seq, w), lambda i, j: (i, 0, 0))
    fixed = lambda shape: pl.BlockSpec(shape, lambda i, j: (0,) * len(shape))
    return pl.pallas_call(
        _attn_prompt_body,
        grid=(b, seq // tq),
        in_specs=[pl.BlockSpec(memory_space=pltpu.SMEM),
                  tile(D_MODEL), tile(QA_W), tile(QA_W), tile(B_W), tile(B_W),
                  whole(KA_W), whole(KA_W), whole(B_W), whole(B_W),
                  fixed(bias.shape), fixed(band.shape), fixed(w_out.shape), fixed((1, D_MODEL))],
        out_specs=tile(D_MODEL),
        out_shape=jax.ShapeDtypeStruct(x.shape, F32),
        scratch_shapes=[pltpu.VMEM((PAD_ROWS + seq, KA_W), BF16),
                        pltpu.VMEM((PAD_ROWS + seq, KA_W), BF16),
                        pltpu.VMEM((PAD_ROWS + seq, B_W), BF16),
                        pltpu.VMEM((PAD_ROWS + seq, B_W), BF16)],
        compiler_params=_params(2),
        name="attn_prompt",
    )(sinks, x, qa, ga, qb, gb, ka, va, kb, vb, bias, band, w_out, g_post)


SAMPLE_REQS = 4


def _softmax2(s_c, s_n, sink):
    m = jnp.maximum(jnp.max(s_c, axis=-1, keepdims=True), jnp.max(s_n, axis=-1, keepdims=True))
    if sink is not None:
        m = jnp.maximum(m, sink)
    e_c = jnp.exp2(s_c - m)
    e_n = jnp.exp2(s_n - m)
    den = jnp.sum(e_c, axis=-1, keepdims=True) + jnp.sum(e_n, axis=-1, keepdims=True)
    if sink is not None:
        den = den + jnp.exp2(sink - m)
    return e_c.astype(BF16), e_n.astype(BF16), 1.0 / den


def _roll_in(cache, new):
    t, w = new.shape[0], cache.shape[1]
    lane = lax.broadcasted_iota(jnp.int32, (cache.shape[0], LANES), 1)
    padded = jnp.concatenate([jnp.zeros((LANES - t, new.shape[1]), F32), new.astype(F32)], axis=0)
    rolled = pltpu.roll(cache, w - t, 1)
    last = jnp.where(lane >= LANES - t, padded.T, rolled[:, w - LANES:])
    return last if w == LANES else jnp.concatenate([rolled[:, :w - LANES], last], axis=1)


def _attn_sample_body(sink_ref, x_ref, qa_ref, ga_ref, qb_ref, gb_ref, ka_ref, va_ref, kb_ref,
                      vb_ref, cak_ref, cav_ref, cbk_ref, cbv_ref, bias_ref, w_ref, g_ref,
                      y_ref, nak_ref, nav_ref, nbk_ref, nbv_ref):
    nreq, t, _ = x_ref.shape
    for r in range(nreq):
        for dst, cache, new in ((nak_ref, cak_ref, ka_ref), (nav_ref, cav_ref, va_ref),
                                (nbk_ref, cbk_ref, kb_ref), (nbv_ref, cbv_ref, vb_ref)):
            dst[r] = _roll_in(cache[r], new[r])

    wb = cbk_ref.shape[2]
    lane = lax.broadcasted_iota(jnp.int32, (t, LANES), 1)
    lo = lane < HEAD_DIM
    zero = jnp.zeros((t, LANES), BF16)
    halves = lambda tile: [jnp.where(lo, tile, zero), jnp.where(lo, zero, tile)]
    n_pairs = B_W // LANES

    def scores(r, unit):
        if unit == 0:
            qa = qa_ref[r]
            lhs = jnp.concatenate([h for p in range(QA_W // LANES)
                                   for h in halves(qa[:, p * LANES:(p + 1) * LANES])], axis=0)
            return (jnp.dot(lhs, cak_ref[r].astype(BF16), preferred_element_type=F32),
                    _nt_dot(lhs, ka_ref[r]))
        sl = slice((unit - 1) * LANES, unit * LANES)
        lhs = jnp.concatenate(halves(qb_ref[r][:, sl]), axis=0)
        return (jnp.dot(lhs, cbk_ref[r, sl, :].astype(BF16), preferred_element_type=F32),
                _nt_dot(lhs, kb_ref[r, :, sl]))

    def finish(r, unit, s):
        s_c, s_n = s
        if unit == 0:
            cache_v, new_v = cav_ref[r].astype(BF16), va_ref[r]
            heads = [(h, sink_ref[h] * LOG2E, None) for h in range(H_A)]
        else:
            sl = slice((unit - 1) * LANES, unit * LANES)
            cache_v, new_v = cbv_ref[r, sl, :].astype(BF16), vb_ref[r, :, sl]
            heads = [(j, None, bias_ref[2 * (unit - 1) + j]) for j in range(2)]
        probs_c, probs_n, inv = [], [], []
        for j, sink, bias in heads:
            rs = slice(j * t, (j + 1) * t)
            sc, sn = s_c[rs], s_n[rs]
            if bias is not None:
                sc, sn = sc + bias[:, :wb], sn + bias[:, wb:wb + t]
            e_c, e_n, iv = _softmax2(sc, sn, sink)
            probs_c.append(e_c)
            probs_n.append(e_n)
            inv.append(iv)
        o = (_nt_dot(jnp.concatenate(probs_c, axis=0), cache_v)
             + jnp.dot(jnp.concatenate(probs_n, axis=0), new_v, preferred_element_type=F32))
        tiles = []
        for p in range(len(heads) // 2):
            o0 = o[(2 * p) * t:(2 * p + 1) * t] * inv[2 * p]
            o1 = o[(2 * p + 1) * t:(2 * p + 2) * t] * inv[2 * p + 1]
            tiles.append(jnp.where(lo, o0, o1))
        return tiles

    order = [(r, u) for r in range(nreq) for u in range(n_pairs + 1)]
    pending = {order[0]: scores(*order[0])}
    mix, tiles = [], []
    for i, (r, u) in enumerate(order):
        if i + 1 < len(order):
            pending[order[i + 1]] = scores(*order[i + 1])
        tiles += finish(r, u, pending.pop((r, u)))
        if u == n_pairs:
            gate = jnp.concatenate([ga_ref[r], gb_ref[r]], axis=1).astype(F32)
            mix.append((jnp.concatenate(tiles, axis=1) * gate).astype(BF16))
            tiles = []
    m = jnp.dot(jnp.concatenate(mix, axis=0), w_ref[...], preferred_element_type=F32)
    y = x_ref[...].reshape(nreq * t, D_MODEL) + _rms(m, g_ref[...])
    y_ref[...] = y.reshape(nreq, t, D_MODEL)


def _attn_sample(sinks, x, qa, ga, qb, gb, ka, va, kb, vb, cak, cav, cbk, cbv, bias, w_out, g_post):
    b = x.shape[0]
    per = lambda a: pl.BlockSpec((SAMPLE_REQS,) + a.shape[1:], lambda i: (i, 0, 0))
    fixed = lambda shape: pl.BlockSpec(shape, lambda i: (0,) * len(shape))
    arrs = (x, qa, ga, qb, gb, ka, va, kb, vb, cak, cav, cbk, cbv)
    return pl.pallas_call(
        _attn_sample_body,
        grid=(b // SAMPLE_REQS,),
        in_specs=[pl.BlockSpec(memory_space=pltpu.SMEM)] + [per(a) for a in arrs]
                 + [fixed(bias.shape), fixed(w_out.shape), fixed((1, D_MODEL))],
        out_specs=[per(a) for a in (x, cak, cav, cbk, cbv)],
        out_shape=[jax.ShapeDtypeStruct(a.shape, F32) for a in (x, cak, cav, cbk, cbv)],
        compiler_params=_params(1),
        name="attn_sample",
    )(sinks, *arrs, bias, w_out, g_post)


LRU_SEG = 8
LRU_STEPS = 32
LRU_TILE = LRU_SEG * LRU_STEPS
N_TAIL = CONV_W - 1
LRU_TILES_PER_STEP = 8
LRU_CB = D_LRU
N_CB = D_LRU // LRU_CB


def _lru_block(xb, z, tail, carry_in, cb, chained, w, after_gates):
    cw_ref, cb_ref, wa_ref, ba_ref, wx_ref, bx_ref, lam_ref = w[2:9]
    cs = slice(cb * LRU_CB, (cb + 1) * LRU_CB)
    grp = lambda v, g, n=1: v[g * LRU_SEG:(g + n) * LRU_SEG]

    if chained:
        sub = lax.broadcasted_iota(jnp.int32, (LRU_SEG, LRU_CB), 0)
        before = [jnp.where(sub == 0, pltpu.roll(grp(tail, j), 1, 0),
                            pltpu.roll(grp(xb, LRU_STEPS - N_TAIL + j), 1, 0))
                  for j in range(N_TAIL)]
    else:
        before = tail
    xb_tail = grp(xb, LRU_STEPS - N_TAIL, N_TAIL)

    cw = cw_ref[:, cs]
    xc = cb_ref[:, cs] + xb * cw[CONV_W - 1:CONV_W]
    for k in range(1, CONV_W):
        shifted = jnp.concatenate(before[N_TAIL - k:] + [grp(xb, 0, LRU_STEPS - k)], axis=0)
        xc = xc + shifted * cw[CONV_W - 1 - k:CONV_W - k]

    xcb = xc.astype(BF16)
    per = LRU_CB // MXU_DIM

    both = [jnp.dot(xcb[:, j * MXU_DIM:(j + 1) * MXU_DIM],
                    jnp.concatenate([wa_ref[cb * per + j], wx_ref[cb * per + j]], axis=1),
                    preferred_element_type=F32) for j in range(per)]
    pre_r = jnp.concatenate([b[:, :MXU_DIM] for b in both], axis=1)
    pre_i = jnp.concatenate([b[:, MXU_DIM:] for b in both], axis=1)
    after_gates()
    r = jax.nn.sigmoid(pre_r + ba_ref[:, cs])
    gi = jax.nn.sigmoid(pre_i + bx_ref[:, cs])
    lam = lam_ref[:, cs]
    log_sig = jnp.minimum(lam, 0.0) - jnp.log1p(jnp.exp(-jnp.abs(lam)))
    log_a = r * (C_GATE * log_sig)
    a = jnp.exp(log_a)
    u = jnp.sqrt(jnp.tanh(-log_a) * (a * a + 1.0)) * (gi * xc)

    hc, ac = grp(u, 0), grp(a, 0)
    h_loc, a_loc = [hc], [ac]
    for g in range(1, LRU_STEPS):
        ag = grp(a, g)
        hc = ag * hc + grp(u, g)
        ac = ag * ac
        h_loc.append(hc)
        a_loc.append(ac)

    if chained:
        c = carry_in
        rows = []
        for s in range(LRU_SEG):
            rows.append(c)
            c = ac[s:s + 1] * c + hc[s:s + 1]
        carry = jnp.concatenate(rows, axis=0)
    else:
        c = None
        carry = carry_in
    h_groups = [hl + al * carry for hl, al in zip(h_loc, a_loc)]
    yl = (jnp.concatenate(h_groups, axis=0) * _silu(z)).astype(BF16)
    yl = jnp.swapaxes(yl.reshape(LRU_STEPS, LRU_SEG, LRU_CB), 0, 1)
    return yl.reshape(LRU_TILE, LRU_CB), c, xb_tail, h_groups[-1]


def _lru_run(load_x, store_y, n_tiles, tails, carries, chained, w):
    gpre_ref, win_ref = w[:2]
    wout_ref, gpost_ref = w[9:11]
    hp = {}

    def in_proj(unit):
        ti, cb = divmod(unit, N_CB)
        if ti not in hp:
            h = _rms(load_x(ti), gpre_ref[...]).astype(BF16)
            h = jnp.swapaxes(h.reshape(LRU_SEG, LRU_STEPS, D_MODEL), 0, 1)
            hp[ti] = h.reshape(LRU_TILE, D_MODEL)
        lo = cb * LRU_CB
        xb = jnp.dot(hp[ti], win_ref[:, lo:lo + LRU_CB], preferred_element_type=F32)
        z = jnp.dot(hp[ti], win_ref[:, D_LRU + lo:D_LRU + lo + LRU_CB], preferred_element_type=F32)
        return xb, z

    units = n_tiles * N_CB
    ready = {0: in_proj(0)}
    h_last = [None] * N_CB
    m = None
    for unit in range(units):
        def emit_next(unit=unit):
            if unit + 1 < units:
                ready[unit + 1] = in_proj(unit + 1)

        ti, cb = divmod(unit, N_CB)
        xb, z = ready.pop(unit)
        yl, carries[cb], tails[cb], h_last[cb] = _lru_block(xb, z, tails[cb], carries[cb], cb,
                                                            chained, w, emit_next)
        part = jnp.dot(yl, wout_ref[cb * LRU_CB:(cb + 1) * LRU_CB, :], preferred_element_type=F32)
        m = part if cb == 0 else m + part
        if cb == N_CB - 1:
            store_y(ti, load_x(ti) + _rms(m, gpost_ref[...]))
    return h_last


def _lru_body(x_ref, h0_ref, c0_ref, *rest, chained):
    w, (y_ref, hl_ref, cl_ref, h_s, tail_s) = rest[:-5], rest[-5:]
    t = pl.program_id(1)
    n_tiles = x_ref.shape[1] // LRU_TILE
    blocks = [slice(cb * LRU_CB, (cb + 1) * LRU_CB) for cb in range(N_CB)]
    rows = lambda ti: slice(ti * LRU_TILE, (ti + 1) * LRU_TILE)
    load_x = lambda ti: x_ref[0, rows(ti), :]

    def store_y(ti, y):
        y_ref[0, rows(ti), :] = y

    if not chained:
        tails = [[c0_ref[j, :, cs] for j in range(N_TAIL)] for cs in blocks]
        carries = [h0_ref[0, :, cs] for cs in blocks]
        h_last = _lru_run(load_x, store_y, n_tiles, tails, carries, False, w)
        hl_ref[0] = jnp.concatenate(h_last, axis=1)
        cl_ref[...] = jnp.concatenate(tails, axis=1).reshape(N_TAIL, LRU_SEG, D_LRU)
        return

    @pl.when(t == 0)
    def _():
        h_s[...] = h0_ref[0]
        tail_s[...] = jnp.zeros(tail_s.shape, F32)
        for j in range(N_TAIL):
            tail_s[j * LRU_SEG + LRU_SEG - 1:(j + 1) * LRU_SEG, :] = c0_ref[0, j:j + 1, :]

    tails = [tail_s[:, cs] for cs in blocks]
    carries = [h_s[:, cs] for cs in blocks]
    _lru_run(load_x, store_y, n_tiles, tails, carries, True, w)
    c = jnp.concatenate(carries, axis=1)
    tail = jnp.concatenate(tails, axis=1)
    h_s[...] = c
    tail_s[...] = tail

    @pl.when(t == pl.num_programs(1) - 1)
    def _():
        hl_ref[0] = c
        last = LRU_SEG - 1
        cl_ref[0] = jnp.concatenate(
            [tail[j * LRU_SEG + last:(j + 1) * LRU_SEG] for j in range(N_TAIL)], axis=0)


def _lru(x, h0, c0, weights, chained):
    nb, seq, _ = x.shape
    fixed = lambda a: pl.BlockSpec(a.shape, lambda i, j: (0,) * a.ndim)
    if chained:
        state_specs = [pl.BlockSpec((1, 1, D_LRU), lambda i, j: (i, 0, 0)),
                       pl.BlockSpec((1, N_TAIL, D_LRU), lambda i, j: (i, 0, 0))]
        state_shapes = [jax.ShapeDtypeStruct((nb, 1, D_LRU), F32),
                        jax.ShapeDtypeStruct((nb, N_TAIL, D_LRU), F32)]
    else:
        assert seq == LRU_TILE
        state_specs = [pl.BlockSpec((1, LRU_SEG, D_LRU), lambda i, j: (i, 0, 0)),
                       pl.BlockSpec((N_TAIL, LRU_SEG, D_LRU), lambda i, j: (0, i, 0))]
        state_shapes = [jax.ShapeDtypeStruct((nb, LRU_SEG, D_LRU), F32),
                        jax.ShapeDtypeStruct((N_TAIL, nb * LRU_SEG, D_LRU), F32)]
    rows = LRU_TILE * (LRU_TILES_PER_STEP if chained else 1)
    tile = pl.BlockSpec((1, rows, D_MODEL), lambda i, j: (i, j, 0))
    return pl.pallas_call(
        functools.partial(_lru_body, chained=chained),
        grid=(nb, seq // rows),
        in_specs=[tile] + state_specs + [fixed(a) for a in weights],
        out_specs=[tile] + state_specs,
        out_shape=[jax.ShapeDtypeStruct(x.shape, F32)] + state_shapes,
        scratch_shapes=[pltpu.VMEM((1, D_LRU), F32), pltpu.VMEM((N_TAIL * LRU_SEG, D_LRU), F32)],
        compiler_params=_params(2),
        name="lru_chained" if chained else "lru_batched",
    )(x, h0, c0, *weights)


def _rope_tables(pos):
    half = HEAD_DIM // 2
    inv = ROPE_THETA ** (-jnp.arange(half, dtype=F32) / half)
    ang = pos.astype(F32)[:, None] * inv[None, :]
    cos = jnp.tile(jnp.cos(ang), (1, LANES // half))
    sin = jnp.sin(ang)
    sin = jnp.tile(jnp.concatenate([-sin, sin], axis=1), (1, LANES // HEAD_DIM))
    return cos, sin


def _interleave_groups(w, axis, unit=HEAD_DIM):
    shape = w.shape
    per = H_A // KV_A
    w = w.reshape(shape[:axis] + (KV_A, per, unit) + shape[axis + 1:])
    w = jnp.swapaxes(w, axis, axis + 1)
    return w.reshape(shape)


def _block_diag(w):
    per = MXU_DIM // BLOCK
    w = w.reshape(N_BLOCKS // per, per, BLOCK, BLOCK)
    eye = jnp.eye(per, dtype=w.dtype)
    w = w[:, :, :, None, :] * eye[None, :, None, :, None]
    return w.reshape(N_BLOCKS // per, MXU_DIM, MXU_DIM)


def kernel(x_prompt, x_sample, cache_a_k, cache_a_v, cache_b_k, cache_b_v, state_c_h, state_c_conv,
           ln_pre, ln_post, w_in_ab, sinks_a, relpos_b, w_out_ab, w_in_c, conv_c_w, conv_c_b,
           gate_c_wa, gate_c_ba, gate_c_wx, gate_c_bx, lambda_c, w_out_c):
    bp, s_len, _ = x_prompt.shape
    bs, t_s, _ = x_sample.shape
    wa_rows = cache_a_k.shape[2]
    wb_rows = cache_b_k.shape[2]
    assert wa_rows + t_s <= WIN_A and wb_rows + t_s <= WIN_B and wb_rows == PAD_ROWS

    w_in = w_in_ab[0]
    w_in = jnp.concatenate([_interleave_groups(w_in[:, :QA_W], 1), w_in[:, QA_W:QA_W + 2 * KA_W],
                            _interleave_groups(w_in[:, QA_W + 2 * KA_W:2 * QA_W + 2 * KA_W], 1),
                            w_in[:, 2 * QA_W + 2 * KA_W:]], axis=1).astype(BF16)
    w_out = jnp.concatenate([_interleave_groups(w_out_ab[0, :QA_W], 0), w_out_ab[0, QA_W:]],
                            axis=0).astype(BF16)
    sinks = _interleave_groups(sinks_a[0], 0, unit=1)
    g_pre0 = ln_pre[0].reshape(1, D_MODEL)
    g_post0 = ln_post[0].reshape(1, D_MODEL)
    bias_p, bias_s, band = _build_bias(relpos_b[0], t_s, wb_rows + t_s)

    cos_p, sin_p = _rope_tables(jnp.arange(s_len, dtype=jnp.int32))
    assert wb_rows == PROJ_TM and wa_rows <= PROJ_TM and s_len % PROJ_TM == 0
    proj = _proj0(x_prompt.reshape(bp * s_len, D_MODEL), g_pre0, w_in, cos_p, sin_p, PROJ_TM,
                  seq=s_len, wa_rows=wa_rows)
    qa, ka, va, ga, qb, kb, vb, gb = [a.reshape(bp, s_len, a.shape[-1]) for a in proj[:8]]
    y0_p = _attn_prompt(sinks, x_prompt, qa, ga, qb, gb, ka, va, kb, vb, bias_p, band, w_out,
                        g_post0, ATT_TQ)
    tmaj = lambda c, heads: jnp.transpose(
        c.reshape(c.shape[0], heads, HEAD_DIM, c.shape[2]), (0, 3, 1, 2))[None]
    nak_p, nav_p = tmaj(proj[8], KV_A), tmaj(proj[9], KV_A)
    nbk_p, nbv_p = tmaj(proj[10], H_B), tmaj(proj[11], H_B)

    n_s = bs * t_s
    tm_s = min(PROJ_TM, n_s)
    cos_s, sin_s = _rope_tables(PAST_LEN + jnp.arange(t_s, dtype=jnp.int32))
    cos_s = jnp.tile(cos_s, (tm_s // t_s, 1))
    sin_s = jnp.tile(sin_s, (tm_s // t_s, 1))
    proj = _proj0(x_sample.reshape(n_s, D_MODEL), g_pre0, w_in, cos_s, sin_s, tm_s)
    qa, ka, va, ga, qb, kb, vb, gb = [a.reshape(bs, t_s, a.shape[-1]) for a in proj]
    fmaj = lambda c: jnp.transpose(c[0], (0, 2, 3, 1))
    cak, cav, cbk, cbv = fmaj(cache_a_k), fmaj(cache_a_v), fmaj(cache_b_k), fmaj(cache_b_v)
    flat = lambda c: c.reshape(bs, c.shape[1] * HEAD_DIM, c.shape[3])
    y0_s, nak_s, nav_s, nbk_s, nbv_s = _attn_sample(
        sinks, x_sample, qa, ga, qb, gb, ka, va, kb, vb, flat(cak), flat(cav), flat(cbk), flat(cbv),
        bias_s, w_out, g_post0)
    nak_s, nav_s = tmaj(nak_s, KV_A), tmaj(nav_s, KV_A)
    nbk_s, nbv_s = tmaj(nbk_s, H_B), tmaj(nbv_s, H_B)

    row = lambda v: v.reshape(1, -1)
    lru_w = (row(ln_pre[1]), w_in_c[0].astype(BF16), conv_c_w[0], row(conv_c_b[0]),
             _block_diag(gate_c_wa[0]).astype(BF16), row(gate_c_ba[0]),
             _block_diag(gate_c_wx[0]).astype(BF16), row(gate_c_bx[0]),
             row(lambda_c[0]), w_out_c[0].astype(BF16), row(ln_post[1]))
    y1_p, hl_p, cl_p = _lru(y0_p, jnp.zeros((bp, 1, D_LRU), F32), jnp.zeros((bp, N_TAIL, D_LRU), F32),
                            lru_w, chained=True)
    assert t_s == LRU_STEPS and bs % LRU_SEG == 0
    y1_s, hl_s, cl_s = _lru(y0_s.reshape(bs // LRU_SEG, LRU_TILE, D_MODEL),
                            state_c_h[0].reshape(bs // LRU_SEG, LRU_SEG, D_LRU),
                            jnp.transpose(state_c_conv[0], (1, 0, 2)), lru_w, chained=False)
    y1_s = y1_s.reshape(bs, t_s, D_MODEL)
    cl_s = jnp.transpose(cl_s, (1, 0, 2))

    return (y1_p, y1_s, nak_p, nav_p, nbk_p, nbv_p,
            hl_p.reshape(1, bp, D_LRU), cl_p.reshape(1, bp, N_TAIL, D_LRU),
            nak_s, nav_s, nbk_s, nbv_s,
            hl_s.reshape(1, bs, D_LRU), cl_s.reshape(1, bs, N_TAIL, D_LRU))
```

```python
import functools

import jax
import jax.numpy as jnp
from jax import lax
from jax.experimental import pallas as pl
from jax.experimental.pallas import tpu as pltpu

F32 = jnp.float32
BF16 = jnp.bfloat16

D_MODEL = 1024
CHUNK = 64
HEAD_DIM = 64
H_A = 8
KV_A = 2
N_PREV_A = 2
H_B = 8
N_PREV_B = 8
MAX_REL = 128
ROPE_THETA = 10000.0
D_LRU = D_MODEL
N_BLOCKS = 16
BLOCK = D_LRU // N_BLOCKS
CONV_W = 4
C_GATE = 8.0
EPS = 1e-6
PAST_LEN = 1024
NEG = -1e30
LOG2E = 1.4426950408889634
Q_SCALE = HEAD_DIM ** -0.5 * LOG2E

LANES = 128
MXU_DIM = 256
VMEM_LIMIT = 56 * 1024 * 1024

QA_W = H_A * HEAD_DIM
KA_W = KV_A * HEAD_DIM
B_W = H_B * HEAD_DIM
GRP = 2 * CHUNK
WIN_A = (N_PREV_A + 2) * CHUNK
WIN_B = (N_PREV_B + 2) * CHUNK
PAD_ROWS = N_PREV_B * CHUNK
TOEP_W = WIN_B + LANES

PROJ_TM = 512
ATT_TQ = 512


def _params(n_axes):
    return pltpu.CompilerParams(
        dimension_semantics=("arbitrary",) * n_axes,
        vmem_limit_bytes=VMEM_LIMIT)


def _rms(x, g):
    ms = jnp.mean(x * x, axis=-1, keepdims=True)
    return x * lax.rsqrt(ms + EPS) * g


def _silu(x):
    return x * jax.nn.sigmoid(x)


def _proj0_body(x_ref, g_ref, w_ref, cos_ref, sin_ref,
                qa_ref, ka_ref, va_ref, ga_ref, qb_ref, kb_ref, vb_ref, gb_ref, *cache_refs,
                tiles_per_seq, wa_rows):
    h = _rms(x_ref[...], g_ref[...]).astype(BF16)
    cos = cos_ref[...]
    sin = sin_ref[...]
    tm = cos.shape[0]
    lane = lax.broadcasted_iota(jnp.int32, (tm, LANES), 1)
    lower = (lane & (HEAD_DIM - 1)) < (HEAD_DIM // 2)

    def mm(lo, hi):
        return jnp.dot(h, w_ref[:, lo:hi], preferred_element_type=F32)

    def rope(t):
        partner = jnp.where(lower, pltpu.roll(t, LANES - HEAD_DIM // 2, 1),
                            pltpu.roll(t, HEAD_DIM // 2, 1))
        return t * cos + partner * sin

    o = 0
    t = mm(o, o + QA_W)
    for k in range(QA_W // LANES):
        sl = slice(k * LANES, (k + 1) * LANES)
        qa_ref[:, sl] = (rope(t[:, sl]) * Q_SCALE).astype(BF16)
    o += QA_W
    ka = rope(mm(o, o + KA_W))
    ka_ref[...] = ka.astype(BF16)
    o += KA_W
    va = mm(o, o + KA_W)
    va_ref[...] = va.astype(BF16)
    o += KA_W
    ga_ref[...] = _silu(mm(o, o + QA_W)).astype(BF16)
    o += QA_W
    qb_ref[...] = (mm(o, o + B_W) * Q_SCALE).astype(BF16)
    o += B_W
    kb = mm(o, o + B_W)
    kb_ref[...] = kb.astype(BF16)
    o += B_W
    vb = mm(o, o + B_W)
    vb_ref[...] = vb.astype(BF16)
    o += B_W
    gb_ref[...] = _silu(mm(o, o + B_W)).astype(BF16)

    if cache_refs:
        @pl.when(pl.program_id(0) % tiles_per_seq == tiles_per_seq - 1)
        def _():
            kat_ref, vat_ref, kbt_ref, vbt_ref = cache_refs
            kat_ref[0] = ka[tm - wa_rows:].T
            vat_ref[0] = va[tm - wa_rows:].T
            kbt_ref[0] = kb.T
            vbt_ref[0] = vb.T


def _proj0(x2d, g, w, cos, sin, tm, seq=None, wa_rows=None):
    n = x2d.shape[0]
    n_pos_blocks = cos.shape[0] // tm
    widths = (QA_W, KA_W, KA_W, QA_W, B_W, B_W, B_W, B_W)
    row = lambda i: (i, 0)
    fixed = lambda i: (0, 0)
    pos = lambda i: (i % n_pos_blocks, 0)
    out_specs = [pl.BlockSpec((tm, wd), row) for wd in widths]
    out_shape = [jax.ShapeDtypeStruct((n, wd), BF16) for wd in widths]
    tiles_per_seq = None
    if seq is not None:
        tiles_per_seq = seq // tm
        for feat, keys in ((KA_W, wa_rows), (KA_W, wa_rows), (B_W, tm), (B_W, tm)):
            out_specs.append(pl.BlockSpec((1, feat, keys), lambda i: (i // tiles_per_seq, 0, 0)))
            out_shape.append(jax.ShapeDtypeStruct((n // seq, feat, keys), F32))
    return pl.pallas_call(
        functools.partial(_proj0_body, tiles_per_seq=tiles_per_seq, wa_rows=wa_rows),
        grid=(n // tm,),
        in_specs=[pl.BlockSpec((tm, D_MODEL), row),
                  pl.BlockSpec((1, D_MODEL), fixed),
                  pl.BlockSpec(w.shape, fixed),
                  pl.BlockSpec((tm, LANES), pos),
                  pl.BlockSpec((tm, LANES), pos)],
        out_specs=out_specs,
        out_shape=out_shape,
        compiler_params=_params(1),
        name="proj0",
    )(x2d, g, w, cos, sin)


def _bias_body(r0_ref, bp_ref, bs_ref, band_ref, *, t_sample, keys_sample):
    x = jnp.broadcast_to(r0_ref[0], (GRP, TOEP_W)) * LOG2E
    row = lax.broadcasted_iota(jnp.int32, (GRP, TOEP_W), 0)
    for b in range(GRP.bit_length() - 1):
        x = jnp.where(((row >> b) & 1) == 1, pltpu.roll(x, 1 << b, 1), x)
    x = x[:, :WIN_B]
    r = lax.broadcasted_iota(jnp.int32, (GRP, WIN_B), 0)
    c = lax.broadcasted_iota(jnp.int32, (GRP, WIN_B), 1)
    ci = r // CHUNK
    cj = c // CHUNK
    bp_ref[0] = jnp.where((cj >= ci) & (cj <= ci + N_PREV_B), x, NEG)
    cs = lax.broadcasted_iota(jnp.int32, (t_sample, WIN_B), 1)
    bs_ref[0] = jnp.where(cs < keys_sample, x[:t_sample], NEG)
    ra = lax.broadcasted_iota(jnp.int32, (GRP, WIN_A), 0) // CHUNK
    ca = lax.broadcasted_iota(jnp.int32, (GRP, WIN_A), 1) // CHUNK
    band_ref[...] = jnp.where((ca >= ra) & (ca <= ra + N_PREV_A), 0.0, NEG)


def _build_bias(relpos, t_sample, keys_sample):
    far = relpos[:, 2 * MAX_REL:]
    n_far = PAD_ROWS - MAX_REL + 1
    r0 = jnp.concatenate(
        [jnp.broadcast_to(far, (H_B, n_far)),
         relpos[:, 2 * MAX_REL - 1:0:-1],
         jnp.broadcast_to(far, (H_B, TOEP_W - n_far - (2 * MAX_REL - 1)))], axis=1)
    r0 = r0.reshape(H_B, 1, TOEP_W)
    return pl.pallas_call(
        functools.partial(_bias_body, t_sample=t_sample, keys_sample=keys_sample),
        grid=(H_B,),
        in_specs=[pl.BlockSpec((1, 1, TOEP_W), lambda h: (h, 0, 0))],
        out_specs=[pl.BlockSpec((1, GRP, WIN_B), lambda h: (h, 0, 0)),
                   pl.BlockSpec((1, t_sample, WIN_B), lambda h: (h, 0, 0)),
                   pl.BlockSpec((GRP, WIN_A), lambda h: (0, 0))],
        out_shape=[jax.ShapeDtypeStruct((H_B, GRP, WIN_B), F32),
                   jax.ShapeDtypeStruct((H_B, t_sample, WIN_B), F32),
                   jax.ShapeDtypeStruct((GRP, WIN_A), F32)],
        compiler_params=_params(1),
        name="relbias",
    )(r0)


def _nt_dot(a, b):
    return lax.dot_general(a, b, (((1,), (1,)), ((), ())), preferred_element_type=F32)


def _attend_group(qa, qb, ka, va, kb_ref, vb_ref, win_b, sink_ref, bias_ref, mask_a, col_ok_a,
                  col_ok_b):
    n = qa.shape[0]
    lane = lax.broadcasted_iota(jnp.int32, (n, LANES), 1)
    lo = lane < HEAD_DIM
    zero = jnp.zeros((n, LANES), BF16)
    halves = lambda tile: [jnp.where(lo, tile, zero), jnp.where(lo, zero, tile)]
    n_pairs = B_W // LANES

    def scores(unit):
        if unit == 0:
            lhs = jnp.concatenate([h for p in range(QA_W // LANES)
                                   for h in halves(qa[:, p * LANES:(p + 1) * LANES])], axis=0)
            return _nt_dot(lhs, ka)
        sl = slice((unit - 1) * LANES, unit * LANES)
        return _nt_dot(jnp.concatenate(halves(qb[:, sl]), axis=0), kb_ref[win_b, sl])

    def finish_a(s_all):
        probs, inv = [], []
        for h in range(H_A):
            s = s_all[h * n:(h + 1) * n]
            if mask_a is not None:
                s = s + mask_a
            if col_ok_a is not None:
                s = jnp.where(col_ok_a, s, NEG)
            sink = sink_ref[h] * LOG2E
            m = jnp.maximum(jnp.max(s, axis=-1, keepdims=True), sink)
            e = jnp.exp2(s - m)
            den = jnp.sum(e, axis=-1, keepdims=True) + jnp.exp2(sink - m)
            probs.append(e.astype(BF16))
            inv.append(1.0 / den)
        r_all = jnp.dot(jnp.concatenate(probs, axis=0), va, preferred_element_type=F32)
        out = []
        for p in range(QA_W // LANES):
            r0 = r_all[(2 * p) * n:(2 * p + 1) * n] * inv[2 * p]
            r1 = r_all[(2 * p + 1) * n:(2 * p + 2) * n] * inv[2 * p + 1]
            out.append(jnp.where(lo, r0, r1))
        return jnp.concatenate(out, axis=1)

    def finish_b(p, s2):
        sl = slice(p * LANES, (p + 1) * LANES)
        probs, inv = [], []
        for j in range(2):
            s = s2[j * n:(j + 1) * n] + bias_ref[2 * p + j]
            if col_ok_b is not None:
                s = jnp.where(col_ok_b, s, NEG)
            m = jnp.max(s, axis=-1, keepdims=True)
            e = jnp.exp2(s - m)
            probs.append(e.astype(BF16))
            inv.append(1.0 / jnp.sum(e, axis=-1, keepdims=True))
        r = jnp.dot(jnp.concatenate(probs, axis=0), vb_ref[win_b, sl], preferred_element_type=F32)
        return jnp.where(lo, r[:n] * inv[0], r[n:] * inv[1])

    s_cur = scores(0)
    o_a, o_b = None, []
    for unit in range(n_pairs + 1):
        s_next = scores(unit + 1) if unit < n_pairs else None
        if unit == 0:
            o_a = finish_a(s_cur)
        else:
            o_b.append(finish_b(unit - 1, s_cur))
        s_cur = s_next
    return o_a, jnp.concatenate(o_b, axis=1)


def _mix_out(o_a, o_b, ga, gb, x, w_ref, g_ref):
    mix = jnp.concatenate([o_a * ga.astype(F32), o_b * gb.astype(F32)], axis=1).astype(BF16)
    m = jnp.dot(mix, w_ref[...], preferred_element_type=F32)
    return x + _rms(m, g_ref[...])


def _attn_prompt_body(sink_ref, x_ref, qa_ref, ga_ref, qb_ref, gb_ref, ka_ref, va_ref, kb_ref,
                      vb_ref, bias_ref, band_ref, w_ref, g_ref, y_ref,
                      kap, vap, kbp, vbp):
    t = pl.program_id(1)
    tq = x_ref.shape[1]

    @pl.when(t == 0)
    def _():
        for dst, src in ((kap, ka_ref), (vap, va_ref), (kbp, kb_ref), (vbp, vb_ref)):
            dst[:PAD_ROWS, :] = jnp.zeros((PAD_ROWS, dst.shape[1]), BF16)
            dst[PAD_ROWS:, :] = src[0]

    band = band_ref[...]

    def group(sub, masked):
        start = pl.multiple_of(t * tq + sub * GRP, GRP)
        rs = slice(sub * GRP, (sub + 1) * GRP)
        win_a = pl.ds(pl.multiple_of(start + (PAD_ROWS - N_PREV_A * CHUNK), GRP), WIN_A)
        win_b = pl.ds(start, WIN_B)
        ok_a = ok_b = None
        if masked:
            ok_a = lax.broadcasted_iota(jnp.int32, (1, WIN_A), 1) >= N_PREV_A * CHUNK - start
            ok_b = lax.broadcasted_iota(jnp.int32, (1, WIN_B), 1) >= N_PREV_B * CHUNK - start
        o_a, o_b = _attend_group(qa_ref[0, rs, :], qb_ref[0, rs, :], kap[win_a, :], vap[win_a, :],
                                 kbp, vbp, win_b, sink_ref, bias_ref, band, ok_a, ok_b)
        y_ref[0, rs, :] = _mix_out(o_a, o_b, ga_ref[0, rs, :], gb_ref[0, rs, :], x_ref[0, rs, :],
                                   w_ref, g_ref)

    n_masked = PAD_ROWS // tq

    @pl.when(t < n_masked)
    def _():
        for sub in range(tq // GRP):
            group(sub, True)

    @pl.when(t >= n_masked)
    def _():
        for sub in range(tq // GRP):
            group(sub, False)


def _attn_prompt(sinks, x, qa, ga, qb, gb, ka, va, kb, vb, bias, band, w_out, g_post, tq):
    b, seq, _ = x.shape
    tile = lambda w: pl.BlockSpec((1, tq, w), lambda i, j: (i, j, 0))
    whole = lambda w: pl.BlockSpec((1, seq, w), lambda i, j: (i, 0, 0))
    fixed = lambda shape: pl.BlockSpec(shape, lambda i, j: (0,) * len(shape))
    return pl.pallas_call(
        _attn_prompt_body,
        grid=(b, seq // tq),
        in_specs=[pl.BlockSpec(memory_space=pltpu.SMEM),
                  tile(D_MODEL), tile(QA_W), tile(QA_W), tile(B_W), tile(B_W),
                  whole(KA_W), whole(KA_W), whole(B_W), whole(B_W),
                  fixed(bias.shape), fixed(band.shape), fixed(w_out.shape), fixed((1, D_MODEL))],
        out_specs=tile(D_MODEL),
        out_shape=jax.ShapeDtypeStruct(x.shape, F32),
        scratch_shapes=[pltpu.VMEM((PAD_ROWS + seq, KA_W), BF16),
                        pltpu.VMEM((PAD_ROWS + seq, KA_W), BF16),
                        pltpu.VMEM((PAD_ROWS + seq, B_W), BF16),
                        pltpu.VMEM((PAD_ROWS + seq, B_W), BF16)],
        compiler_params=_params(2),
        name="attn_prompt",
    )(sinks, x, qa, ga, qb, gb, ka, va, kb, vb, bias, band, w_out, g_post)


SAMPLE_REQS = 4


def _softmax2(s_c, s_n, sink):
    m = jnp.maximum(jnp.max(s_c, axis=-1, keepdims=True), jnp.max(s_n, axis=-1, keepdims=True))
    if sink is not None:
        m = jnp.maximum(m, sink)
    e_c = jnp.exp2(s_c - m)
    e_n = jnp.exp2(s_n - m)
    den = jnp.sum(e_c, axis=-1, keepdims=True) + jnp.sum(e_n, axis=-1, keepdims=True)
    if sink is not None:
        den = den + jnp.exp2(sink - m)
    return e_c.astype(BF16), e_n.astype(BF16), 1.0 / den


def _roll_in(cache, new):
    t, w = new.shape[0], cache.shape[1]
    lane = lax.broadcasted_iota(jnp.int32, (cache.shape[0], LANES), 1)
    padded = jnp.concatenate([jnp.zeros((LANES - t, new.shape[1]), F32), new.astype(F32)], axis=0)
    rolled = pltpu.roll(cache, w - t, 1)
    last = jnp.where(lane >= LANES - t, padded.T, rolled[:, w - LANES:])
    return last if w == LANES else jnp.concatenate([rolled[:, :w - LANES], last], axis=1)


def _attn_sample_body(sink_ref, x_ref, qa_ref, ga_ref, qb_ref, gb_ref, ka_ref, va_ref, kb_ref,
                      vb_ref, cak_ref, cav_ref, cbk_ref, cbv_ref, bias_ref, w_ref, g_ref,
                      y_ref, nak_ref, nav_ref, nbk_ref, nbv_ref):
    nreq, t, _ = x_ref.shape
    for r in range(nreq):
        for dst, cache, new in ((nak_ref, cak_ref, ka_ref), (nav_ref, cav_ref, va_ref),
                                (nbk_ref, cbk_ref, kb_ref), (nbv_ref, cbv_ref, vb_ref)):
            dst[r] = _roll_in(cache[r], new[r])

    wb = cbk_ref.shape[2]
    lane = lax.broadcasted_iota(jnp.int32, (t, LANES), 1)
    lo = lane < HEAD_DIM
    zero = jnp.zeros((t, LANES), BF16)
    halves = lambda tile: [jnp.where(lo, tile, zero), jnp.where(lo, zero, tile)]
    n_pairs = B_W // LANES

    def scores(r, unit):
        if unit == 0:
            qa = qa_ref[r]
            lhs = jnp.concatenate([h for p in range(QA_W // LANES)
                                   for h in halves(qa[:, p * LANES:(p + 1) * LANES])], axis=0)
            return (jnp.dot(lhs, cak_ref[r].astype(BF16), preferred_element_type=F32),
                    _nt_dot(lhs, ka_ref[r]))
        sl = slice((unit - 1) * LANES, unit * LANES)
        lhs = jnp.concatenate(halves(qb_ref[r][:, sl]), axis=0)
        return (jnp.dot(lhs, cbk_ref[r, sl, :].astype(BF16), preferred_element_type=F32),
                _nt_dot(lhs, kb_ref[r, :, sl]))

    def finish(r, unit, s):
        s_c, s_n = s
        if unit == 0:
            cache_v, new_v = cav_ref[r].astype(BF16), va_ref[r]
            heads = [(h, sink_ref[h] * LOG2E, None) for h in range(H_A)]
        else:
            sl = slice((unit - 1) * LANES, unit * LANES)
            cache_v, new_v = cbv_ref[r, sl, :].astype(BF16), vb_ref[r, :, sl]
            heads = [(j, None, bias_ref[2 * (unit - 1) + j]) for j in range(2)]
        probs_c, probs_n, inv = [], [], []
        for j, sink, bias in heads:
            rs = slice(j * t, (j + 1) * t)
            sc, sn = s_c[rs], s_n[rs]
            if bias is not None:
                sc, sn = sc + bias[:, :wb], sn + bias[:, wb:wb + t]
            e_c, e_n, iv = _softmax2(sc, sn, sink)
            probs_c.append(e_c)
            probs_n.append(e_n)
            inv.append(iv)
        o = (_nt_dot(jnp.concatenate(probs_c, axis=0), cache_v)
             + jnp.dot(jnp.concatenate(probs_n, axis=0), new_v, preferred_element_type=F32))
        tiles = []
        for p in range(len(heads) // 2):
            o0 = o[(2 * p) * t:(2 * p + 1) * t] * inv[2 * p]
            o1 = o[(2 * p + 1) * t:(2 * p + 2) * t] * inv[2 * p + 1]
            tiles.append(jnp.where(lo, o0, o1))
        return tiles

    order = [(r, u) for r in range(nreq) for u in range(n_pairs + 1)]
    pending = {order[0]: scores(*order[0])}
    mix, tiles = [], []
    for i, (r, u) in enumerate(order):
        if i + 1 < len(order):
            pending[order[i + 1]] = scores(*order[i + 1])
        tiles += finish(r, u, pending.pop((r, u)))
        if u == n_pairs:
            gate = jnp.concatenate([ga_ref[r], gb_ref[r]], axis=1).astype(F32)
            mix.append((jnp.concatenate(tiles, axis=1) * gate).astype(BF16))
            tiles = []
    m = jnp.dot(jnp.concatenate(mix, axis=0), w_ref[...], preferred_element_type=F32)
    y = x_ref[...].reshape(nreq * t, D_MODEL) + _rms(m, g_ref[...])
    y_ref[...] = y.reshape(nreq, t, D_MODEL)


def _attn_sample(sinks, x, qa, ga, qb, gb, ka, va, kb, vb, cak, cav, cbk, cbv, bias, w_out, g_post):
    b = x.shape[0]
    per = lambda a: pl.BlockSpec((SAMPLE_REQS,) + a.shape[1:], lambda i: (i, 0, 0))
    fixed = lambda shape: pl.BlockSpec(shape, lambda i: (0,) * len(shape))
    arrs = (x, qa, ga, qb, gb, ka, va, kb, vb, cak, cav, cbk, cbv)
    return pl.pallas_call(
        _attn_sample_body,
        grid=(b // SAMPLE_REQS,),
        in_specs=[pl.BlockSpec(memory_space=pltpu.SMEM)] + [per(a) for a in arrs]
                 + [fixed(bias.shape), fixed(w_out.shape), fixed((1, D_MODEL))],
        out_specs=[per(a) for a in (x, cak, cav, cbk, cbv)],
        out_shape=[jax.ShapeDtypeStruct(a.shape, F32) for a in (x, cak, cav, cbk, cbv)],
        compiler_params=_params(1),
        name="attn_sample",
    )(sinks, *arrs, bias, w_out, g_post)


LRU_SEG = 8
LRU_STEPS = 32
LRU_TILE = LRU_SEG * LRU_STEPS
N_TAIL = CONV_W - 1
LRU_TILES_PER_STEP = 4
LRU_CB = D_LRU
N_CB = D_LRU // LRU_CB


def _lru_block(xb, z, tail, carry_in, cb, chained, w, after_gates):
    cw_ref, cb_ref, wa_ref, ba_ref, wx_ref, bx_ref, lam_ref = w[2:9]
    cs = slice(cb * LRU_CB, (cb + 1) * LRU_CB)
    grp = lambda v, g, n=1: v[g * LRU_SEG:(g + n) * LRU_SEG]

    if chained:
        sub = lax.broadcasted_iota(jnp.int32, (LRU_SEG, LRU_CB), 0)
        before = [jnp.where(sub == 0, pltpu.roll(grp(tail, j), 1, 0),
                            pltpu.roll(grp(xb, LRU_STEPS - N_TAIL + j), 1, 0))
                  for j in range(N_TAIL)]
    else:
        before = tail
    xb_tail = grp(xb, LRU_STEPS - N_TAIL, N_TAIL)

    cw = cw_ref[:, cs]
    xc = cb_ref[:, cs] + xb * cw[CONV_W - 1:CONV_W]
    for k in range(1, CONV_W):
        shifted = jnp.concatenate(before[N_TAIL - k:] + [grp(xb, 0, LRU_STEPS - k)], axis=0)
        xc = xc + shifted * cw[CONV_W - 1 - k:CONV_W - k]

    xcb = xc.astype(BF16)
    per = LRU_CB // MXU_DIM

    both = [jnp.dot(xcb[:, j * MXU_DIM:(j + 1) * MXU_DIM],
                    jnp.concatenate([wa_ref[cb * per + j], wx_ref[cb * per + j]], axis=1),
                    preferred_element_type=F32) for j in range(per)]
    pre_r = jnp.concatenate([b[:, :MXU_DIM] for b in both], axis=1)
    pre_i = jnp.concatenate([b[:, MXU_DIM:] for b in both], axis=1)
    after_gates()
    r = jax.nn.sigmoid(pre_r + ba_ref[:, cs])
    gi = jax.nn.sigmoid(pre_i + bx_ref[:, cs])
    lam = lam_ref[:, cs]
    log_sig = jnp.minimum(lam, 0.0) - jnp.log1p(jnp.exp(-jnp.abs(lam)))
    log_a = r * (C_GATE * log_sig)
    a = jnp.exp(log_a)
    u = jnp.sqrt(jnp.tanh(-log_a) * (a * a + 1.0)) * (gi * xc)

    hc, ac = grp(u, 0), grp(a, 0)
    h_loc, a_loc = [hc], [ac]
    for g in range(1, LRU_STEPS):
        ag = grp(a, g)
        hc = ag * hc + grp(u, g)
        ac = ag * ac
        h_loc.append(hc)
        a_loc.append(ac)

    if chained:
        c = carry_in
        rows = []
        for s in range(LRU_SEG):
            rows.append(c)
            c = ac[s:s + 1] * c + hc[s:s + 1]
        carry = jnp.concatenate(rows, axis=0)
    else:
        c = None
        carry = carry_in
    h_groups = [hl + al * carry for hl, al in zip(h_loc, a_loc)]
    yl = (jnp.concatenate(h_groups, axis=0) * _silu(z)).astype(BF16)
    yl = jnp.swapaxes(yl.reshape(LRU_STEPS, LRU_SEG, LRU_CB), 0, 1)
    return yl.reshape(LRU_TILE, LRU_CB), c, xb_tail, h_groups[-1]


def _lru_run(load_x, store_y, n_tiles, tails, carries, chained, w):
    gpre_ref, win_ref = w[:2]
    wout_ref, gpost_ref = w[9:11]
    hp = {}

    def in_proj(unit):
        ti, cb = divmod(unit, N_CB)
        if ti not in hp:
            h = _rms(load_x(ti), gpre_ref[...]).astype(BF16)
            h = jnp.swapaxes(h.reshape(LRU_SEG, LRU_STEPS, D_MODEL), 0, 1)
            hp[ti] = h.reshape(LRU_TILE, D_MODEL)
        lo = cb * LRU_CB
        xb = jnp.dot(hp[ti], win_ref[:, lo:lo + LRU_CB], preferred_element_type=F32)
        z = jnp.dot(hp[ti], win_ref[:, D_LRU + lo:D_LRU + lo + LRU_CB], preferred_element_type=F32)
        return xb, z

    units = n_tiles * N_CB
    ready = {0: in_proj(0)}
    h_last = [None] * N_CB
    m = None
    for unit in range(units):
        def emit_next(unit=unit):
            if unit + 1 < units:
                ready[unit + 1] = in_proj(unit + 1)

        ti, cb = divmod(unit, N_CB)
        xb, z = ready.pop(unit)
        yl, carries[cb], tails[cb], h_last[cb] = _lru_block(xb, z, tails[cb], carries[cb], cb,
                                                            chained, w, emit_next)
        part = jnp.dot(yl, wout_ref[cb * LRU_CB:(cb + 1) * LRU_CB, :], preferred_element_type=F32)
        m = part if cb == 0 else m + part
        if cb == N_CB - 1:
            store_y(ti, load_x(ti) + _rms(m, gpost_ref[...]))
    return h_last


def _lru_body(x_ref, h0_ref, c0_ref, *rest, chained):
    w, (y_ref, hl_ref, cl_ref, h_s, tail_s) = rest[:-5], rest[-5:]
    t = pl.program_id(1)
    n_tiles = x_ref.shape[1] // LRU_TILE
    blocks = [slice(cb * LRU_CB, (cb + 1) * LRU_CB) for cb in range(N_CB)]
    rows = lambda ti: slice(ti * LRU_TILE, (ti + 1) * LRU_TILE)
    load_x = lambda ti: x_ref[0, rows(ti), :]

    def store_y(ti, y):
        y_ref[0, rows(ti), :] = y

    if not chained:
        tails = [[c0_ref[j, :, cs] for j in range(N_TAIL)] for cs in blocks]
        carries = [h0_ref[0, :, cs] for cs in blocks]
        h_last = _lru_run(load_x, store_y, n_tiles, tails, carries, False, w)
        hl_ref[0] = jnp.concatenate(h_last, axis=1)
        cl_ref[...] = jnp.concatenate(tails, axis=1).reshape(N_TAIL, LRU_SEG, D_LRU)
        return

    @pl.when(t == 0)
    def _():
        h_s[...] = h0_ref[0]
        tail_s[...] = jnp.zeros(tail_s.shape, F32)
        for j in range(N_TAIL):
            tail_s[j * LRU_SEG + LRU_SEG - 1:(j + 1) * LRU_SEG, :] = c0_ref[0, j:j + 1, :]

    tails = [tail_s[:, cs] for cs in blocks]
    carries = [h_s[:, cs] for cs in blocks]
    _lru_run(load_x, store_y, n_tiles, tails, carries, True, w)
    c = jnp.concatenate(carries, axis=1)
    tail = jnp.concatenate(tails, axis=1)
    h_s[...] = c
    tail_s[...] = tail

    @pl.when(t == pl.num_programs(1) - 1)
    def _():
        hl_ref[0] = c
        last = LRU_SEG - 1
        cl_ref[0] = jnp.concatenate(
            [tail[j * LRU_SEG + last:(j + 1) * LRU_SEG] for j in range(N_TAIL)], axis=0)


def _lru(x, h0, c0, weights, chained):
    nb, seq, _ = x.shape
    fixed = lambda a: pl.BlockSpec(a.shape, lambda i, j: (0,) * a.ndim)
    if chained:
        state_specs = [pl.BlockSpec((1, 1, D_LRU), lambda i, j: (i, 0, 0)),
                       pl.BlockSpec((1, N_TAIL, D_LRU), lambda i, j: (i, 0, 0))]
        state_shapes = [jax.ShapeDtypeStruct((nb, 1, D_LRU), F32),
                        jax.ShapeDtypeStruct((nb, N_TAIL, D_LRU), F32)]
    else:
        assert seq == LRU_TILE
        state_specs = [pl.BlockSpec((1, LRU_SEG, D_LRU), lambda i, j: (i, 0, 0)),
                       pl.BlockSpec((N_TAIL, LRU_SEG, D_LRU), lambda i, j: (0, i, 0))]
        state_shapes = [jax.ShapeDtypeStruct((nb, LRU_SEG, D_LRU), F32),
                        jax.ShapeDtypeStruct((N_TAIL, nb * LRU_SEG, D_LRU), F32)]
    rows = LRU_TILE * (LRU_TILES_PER_STEP if chained else 1)
    tile = pl.BlockSpec((1, rows, D_MODEL), lambda i, j: (i, j, 0))
    return pl.pallas_call(
        functools.partial(_lru_body, chained=chained),
        grid=(nb, seq // rows),
        in_specs=[tile] + state_specs + [fixed(a) for a in weights],
        out_specs=[tile] + state_specs,
        out_shape=[jax.ShapeDtypeStruct(x.shape, F32)] + state_shapes,
        scratch_shapes=[pltpu.VMEM((1, D_LRU), F32), pltpu.VMEM((N_TAIL * LRU_SEG, D_LRU), F32)],
        compiler_params=_params(2),
        name="lru_chained" if chained else "lru_batched",
    )(x, h0, c0, *weights)


def _rope_tables(pos):
    half = HEAD_DIM // 2
    inv = ROPE_THETA ** (-jnp.arange(half, dtype=F32) / half)
    ang = pos.astype(F32)[:, None] * inv[None, :]
    cos = jnp.tile(jnp.cos(ang), (1, LANES // half))
    sin = jnp.sin(ang)
    sin = jnp.tile(jnp.concatenate([-sin, sin], axis=1), (1, LANES // HEAD_DIM))
    return cos, sin


def _interleave_groups(w, axis, unit=HEAD_DIM):
    shape = w.shape
    per = H_A // KV_A
    w = w.reshape(shape[:axis] + (KV_A, per, unit) + shape[axis + 1:])
    w = jnp.swapaxes(w, axis, axis + 1)
    return w.reshape(shape)


def _block_diag(w):
    per = MXU_DIM // BLOCK
    w = w.reshape(N_BLOCKS // per, per, BLOCK, BLOCK)
    eye = jnp.eye(per, dtype=w.dtype)
    w = w[:, :, :, None, :] * eye[None, :, None, :, None]
    return w.reshape(N_BLOCKS // per, MXU_DIM, MXU_DIM)


def kernel(x_prompt, x_sample, cache_a_k, cache_a_v, cache_b_k, cache_b_v, state_c_h, state_c_conv,
           ln_pre, ln_post, w_in_ab, sinks_a, relpos_b, w_out_ab, w_in_c, conv_c_w, conv_c_b,
           gate_c_wa, gate_c_ba, gate_c_wx, gate_c_bx, lambda_c, w_out_c):
    bp, s_len, _ = x_prompt.shape
    bs, t_s, _ = x_sample.shape
    wa_rows = cache_a_k.shape[2]
    wb_rows = cache_b_k.shape[2]
    assert wa_rows + t_s <= WIN_A and wb_rows + t_s <= WIN_B and wb_rows == PAD_ROWS

    w_in = w_in_ab[0]
    w_in = jnp.concatenate([_interleave_groups(w_in[:, :QA_W], 1), w_in[:, QA_W:QA_W + 2 * KA_W],
                            _interleave_groups(w_in[:, QA_W + 2 * KA_W:2 * QA_W + 2 * KA_W], 1),
                            w_in[:, 2 * QA_W + 2 * KA_W:]], axis=1).astype(BF16)
    w_out = jnp.concatenate([_interleave_groups(w_out_ab[0, :QA_W], 0), w_out_ab[0, QA_W:]],
                            axis=0).astype(BF16)
    sinks = _interleave_groups(sinks_a[0], 0, unit=1)
    g_pre0 = ln_pre[0].reshape(1, D_MODEL)
    g_post0 = ln_post[0].reshape(1, D_MODEL)
    bias_p, bias_s, band = _build_bias(relpos_b[0], t_s, wb_rows + t_s)

    cos_p, sin_p = _rope_tables(jnp.arange(s_len, dtype=jnp.int32))
    assert wb_rows == PROJ_TM and wa_rows <= PROJ_TM and s_len % PROJ_TM == 0
    proj = _proj0(x_prompt.reshape(bp * s_len, D_MODEL), g_pre0, w_in, cos_p, sin_p, PROJ_TM,
                  seq=s_len, wa_rows=wa_rows)
    qa, ka, va, ga, qb, kb, vb, gb = [a.reshape(bp, s_len, a.shape[-1]) for a in proj[:8]]
    y0_p = _attn_prompt(sinks, x_prompt, qa, ga, qb, gb, ka, va, kb, vb, bias_p, band, w_out,
                        g_post0, ATT_TQ)
    tmaj = lambda c, heads: jnp.transpose(
        c.reshape(c.shape[0], heads, HEAD_DIM, c.shape[2]), (0, 3, 1, 2))[None]
    nak_p, nav_p = tmaj(proj[8], KV_A), tmaj(proj[9], KV_A)
    nbk_p, nbv_p = tmaj(proj[10], H_B), tmaj(proj[11], H_B)

    n_s = bs * t_s
    tm_s = min(PROJ_TM, n_s)
    cos_s, sin_s = _rope_tables(PAST_LEN + jnp.arange(t_s, dtype=jnp.int32))
    cos_s = jnp.tile(cos_s, (tm_s // t_s, 1))
    sin_s = jnp.tile(sin_s, (tm_s // t_s, 1))
    proj = _proj0(x_sample.reshape(n_s, D_MODEL), g_pre0, w_in, cos_s, sin_s, tm_s)
    qa, ka, va, ga, qb, kb, vb, gb = [a.reshape(bs, t_s, a.shape[-1]) for a in proj]
    fmaj = lambda c: jnp.transpose(c[0], (0, 2, 3, 1))
    cak, cav, cbk, cbv = fmaj(cache_a_k), fmaj(cache_a_v), fmaj(cache_b_k), fmaj(cache_b_v)
    flat = lambda c: c.reshape(bs, c.shape[1] * HEAD_DIM, c.shape[3])
    y0_s, nak_s, nav_s, nbk_s, nbv_s = _attn_sample(
        sinks, x_sample, qa, ga, qb, gb, ka, va, kb, vb, flat(cak), flat(cav), flat(cbk), flat(cbv),
        bias_s, w_out, g_post0)
    nak_s, nav_s = tmaj(nak_s, KV_A), tmaj(nav_s, KV_A)
    nbk_s, nbv_s = tmaj(nbk_s, H_B), tmaj(nbv_s, H_B)

    row = lambda v: v.reshape(1, -1)
    lru_w = (row(ln_pre[1]), w_in_c[0].astype(BF16), conv_c_w[0], row(conv_c_b[0]),
             _block_diag(gate_c_wa[0]).astype(BF16), row(gate_c_ba[0]),
             _block_diag(gate_c_wx[0]).astype(BF16), row(gate_c_bx[0]),
             row(lambda_c[0]), w_out_c[0].astype(BF16), row(ln_post[1]))
    y1_p, hl_p, cl_p = _lru(y0_p, jnp.zeros((bp, 1, D_LRU), F32), jnp.zeros((bp, N_TAIL, D_LRU), F32),
                            lru_w, chained=True)
    assert t_s == LRU_STEPS and bs % LRU_SEG == 0
    y1_s, hl_s, cl_s = _lru(y0_s.reshape(bs // LRU_SEG, LRU_TILE, D_MODEL),
                            state_c_h[0].reshape(bs // LRU_SEG, LRU_SEG, D_LRU),
                            jnp.transpose(state_c_conv[0], (1, 0, 2)), lru_w, chained=False)
    y1_s = y1_s.reshape(bs, t_s, D_MODEL)
    cl_s = jnp.transpose(cl_s, (1, 0, 2))

    return (y1_p, y1_s, nak_p, nav_p, nbk_p, nbv_p,
            hl_p.reshape(1, bp, D_LRU), cl_p.reshape(1, bp, N_TAIL, D_LRU),
            nak_s, nav_s, nbk_s, nbv_s,
            hl_s.reshape(1, bs, D_LRU), cl_s.reshape(1, bs, N_TAIL, D_LRU))
```
